```python
import jax, jax.numpy as jnp
from jax import lax
import numpy as np

D_MODEL = 1024
BATCH = 4
SEQ = 4096
DEPTH = 2
DEC_BATCH = 32
DEC_SEQ = 64
PAST_LEN = 2048

CHUNK = 64
N_EVEN = (DEPTH + 1) // 2
N_ODD = DEPTH // 2
A_HEADS = 8
A_HEAD_DIM = 64
A_WIDTH = A_HEADS * A_HEAD_DIM
Q_BLOCK = 2 * CHUNK
B_WIDTH = D_MODEL // 2
B_BLOCKS = 8
B_BLOCK_DIM = B_WIDTH // B_BLOCKS
RG_CONV = 4
RG_C = 8.0
E_SPLITS = (A_WIDTH, 2 * A_WIDTH, 3 * A_WIDTH, 3 * A_WIDTH + A_HEADS, 3 * A_WIDTH + A_HEADS + B_WIDTH)
E_COLS = 3 * A_WIDTH + A_HEADS + 2 * B_WIDTH
C_WIDTH = D_MODEL
C_GROUPS = 8
C_GROUP_DIM = C_WIDTH // C_GROUPS
C_LEN = 2 * CHUNK
D_FF = 2816
FFN_CONV = 3
ALPHA = (2 * DEPTH) ** 0.25
BETA = (8 * DEPTH) ** -0.25
LN_EPS = 1e-5

kernel_name = 'fox_rglru_gmlp_convffn_stream_step'


def layer_norm(x, g, b):
    xf = x.astype(jnp.float32)
    mu = jnp.mean(xf, axis=-1, keepdims=True)
    var = jnp.mean(jnp.square(xf - mu), axis=-1, keepdims=True)
    return ((xf - mu) * lax.rsqrt(var + LN_EPS) * g + b).astype(x.dtype)


def causal_dwconv(x, state, w, b):
    width = w.shape[0]
    t = x.shape[1]
    xp = jnp.concatenate([state.astype(x.dtype), x], axis=1)
    y = b
    for j in range(width):
        y = y + w[j] * xp[:, j:j + t]
    return y, xp[:, -(width - 1):]


def fox_attend(q, k, v, cq, ck, q_pos, k_pos):
    s = jnp.einsum('bqhd,bkhd->bhqk', q, k).astype(jnp.float32) * (A_HEAD_DIM ** -0.5)
    s = s + (jnp.transpose(cq, (0, 2, 1))[..., :, None] - jnp.transpose(ck, (0, 2, 1))[..., None, :])
    mask = k_pos[None, :] <= q_pos[:, None]
    s = jnp.where(mask, s, -jnp.inf)
    p = jax.nn.softmax(s, axis=-1).astype(v.dtype)
    return jnp.einsum('bhqk,bkhd->bqhd', p, v)


def fox_prompt(q, k, v, c):
    bsz, t, h, d = q.shape
    pos = jnp.arange(t)

    def one_block(i):
        start = i * Q_BLOCK
        qb = lax.dynamic_slice_in_dim(q, start, Q_BLOCK, axis=1)
        cb = lax.dynamic_slice_in_dim(c, start, Q_BLOCK, axis=1)
        return fox_attend(qb, k, v, cb, c, start + jnp.arange(Q_BLOCK), pos)

    out = lax.map(one_block, jnp.arange(t // Q_BLOCK))
    return jnp.transpose(out, (1, 0, 2, 3, 4)).reshape(bsz, t, h, d)


def rg_lru(xb, conv_state, h0, conv_w, conv_b, wa, ba, wx, bx, lam):
    xc, conv_new = causal_dwconv(xb, conv_state, conv_w, conv_b)
    bsz, t, _ = xc.shape
    xg = xc.reshape(bsz, t, B_BLOCKS, B_BLOCK_DIM)
    gate_r = jax.nn.sigmoid(jnp.einsum('btnc,ncd->btnd', xg, wa).reshape(bsz, t, B_WIDTH) + ba)
    gate_i = jax.nn.sigmoid(jnp.einsum('btnc,ncd->btnd', xg, wx).reshape(bsz, t, B_WIDTH) + bx)
    log_a = -RG_C * gate_r.astype(jnp.float32) * jax.nn.softplus(-lam.astype(jnp.float32))
    a = jnp.exp(log_a)
    u = jnp.sqrt(-jnp.expm1(2.0 * log_a)) * (gate_i * xc).astype(jnp.float32)
    u = u.at[:, 0].add(a[:, 0] * h0.astype(jnp.float32))

    def combine(left, right):
        a1, b1 = left
        a2, b2 = right
        return a1 * a2, a2 * b1 + b2

    _, h = lax.associative_scan(combine, (a, u), axis=1)
    return h.astype(xb.dtype), conv_new, h[:, -1].astype(xb.dtype)


def even_mixer(x, past, w_in, b_f, rg_conv_w, rg_conv_b, rg_wa, rg_ba, rg_wx, rg_bx, rg_lam, w_out):
    bsz, t, _ = x.shape
    q, k, v, f, xb, gb = jnp.split(x @ w_in, list(E_SPLITS), axis=-1)
    q = q.reshape(bsz, t, A_HEADS, A_HEAD_DIM)
    k = k.reshape(bsz, t, A_HEADS, A_HEAD_DIM)
    v = v.reshape(bsz, t, A_HEADS, A_HEAD_DIM)
    logf = jax.nn.log_sigmoid((f + b_f).astype(jnp.float32))
    cum = jnp.cumsum(logf, axis=1)
    if past is None:
        attn = fox_prompt(q, k, v, cum)
        conv_s = jnp.zeros((bsz, RG_CONV - 1, B_WIDTH), x.dtype)
        h0 = jnp.zeros((bsz, B_WIDTH), jnp.float32)
    else:
        k_c, v_c, lf_c, conv_s, h0 = past
        plen = k_c.shape[1]
        lf_c = lf_c.astype(jnp.float32)
        rev = jnp.flip(jnp.cumsum(jnp.flip(lf_c, 1), axis=1), 1)
        ck = jnp.concatenate([lf_c - rev, cum], axis=1)
        k_all = jnp.concatenate([k_c.astype(k.dtype), k], axis=1)
        v_all = jnp.concatenate([v_c.astype(v.dtype), v], axis=1)
        attn = fox_attend(q, k_all, v_all, cum, ck, plen + jnp.arange(t), jnp.arange(plen + t))
    h, conv_new, h_last = rg_lru(xb, conv_s, h0, rg_conv_w, rg_conv_b, rg_wa, rg_ba, rg_wx, rg_bx, rg_lam)
    y = jnp.concatenate([attn.reshape(bsz, t, A_WIDTH), jax.nn.gelu(gb) * h], axis=-1) @ w_out
    return y, (k, v, logf, conv_new, h_last)


def odd_mixer(x, w_in, sgu_g, sgu_b, sgu_w, sgu_bias, w_out):
    bsz, t, _ = x.shape
    z = jax.nn.gelu(x @ w_in)
    u, v = jnp.split(z, 2, axis=-1)
    v = layer_norm(v, sgu_g, sgu_b)
    length = min(t, C_LEN)
    nc = t // length
    vg = v.reshape(bsz, nc, length, C_GROUPS, C_GROUP_DIM)
    ws = sgu_w[:, :length, :length] * jnp.tril(jnp.ones((length, length), sgu_w.dtype))
    mixed = jnp.einsum('gts,bnsgc->bntgc', ws, vg) + sgu_bias[:, :length].T[None, None, :, :, None]
    y = (u * mixed.reshape(bsz, t, C_WIDTH)) @ w_out
    return y, v


def conv_ffn(x, conv_state, w_up, conv_w, conv_b, w_down):
    hc, new_state = causal_dwconv(x @ w_up, conv_state, conv_w, conv_b)
    g, u = jnp.split(hc, 2, axis=-1)
    return (jax.nn.gelu(g) * u) @ w_down, new_state


def setup_inputs(seed: int = 0) -> dict:
    key = jax.random.key(seed)
    ks = jax.random.split(key, 40)
    nrm = lambda i, shape: jax.random.normal(ks[i], shape, jnp.float32)
    w_in_e = nrm(10, (N_EVEN, D_MODEL, E_COLS)) * D_MODEL ** -0.5
    w_in_e = w_in_e.at[..., 2 * A_WIDTH:3 * A_WIDTH].multiply(BETA)
    u_a = jax.random.uniform(ks[18], (N_EVEN, B_WIDTH), jnp.float32, minval=0.9, maxval=0.999)
    a_base = u_a ** (1.0 / RG_C)
    return {
        'x_prompt': nrm(0, (BATCH, SEQ, D_MODEL)),
        'x_sample': nrm(1, (DEC_BATCH, DEC_SEQ, D_MODEL)),
        'cache_k': nrm(2, (N_EVEN, DEC_BATCH, PAST_LEN, A_HEADS, A_HEAD_DIM)),
        'cache_v': nrm(3, (N_EVEN, DEC_BATCH, PAST_LEN, A_HEADS, A_HEAD_DIM)) * BETA,
        'cache_logf': jax.nn.log_sigmoid(2.0 + nrm(4, (N_EVEN, DEC_BATCH, PAST_LEN, A_HEADS))),
        'state_rglru_conv': nrm(5, (N_EVEN, DEC_BATCH, RG_CONV - 1, B_WIDTH)),
        'state_rglru_h': nrm(6, (N_EVEN, DEC_BATCH, B_WIDTH)) * 0.5,
        'state_ffn_conv': nrm(7, (DEPTH, DEC_BATCH, FFN_CONV - 1, 2 * D_FF)),
        'w_in_e': w_in_e,
        'b_f': 2.0 + 0.1 * nrm(11, (N_EVEN, A_HEADS)),
        'rg_conv_w': nrm(12, (N_EVEN, RG_CONV, B_WIDTH)) * RG_CONV ** -0.5,
        'rg_conv_b': 0.01 * nrm(13, (N_EVEN, B_WIDTH)),
        'rg_wa': nrm(14, (N_EVEN, B_BLOCKS, B_BLOCK_DIM, B_BLOCK_DIM)) * B_BLOCK_DIM ** -0.5,
        'rg_ba': 0.01 * nrm(15, (N_EVEN, B_WIDTH)),
        'rg_wx': nrm(16, (N_EVEN, B_BLOCKS, B_BLOCK_DIM, B_BLOCK_DIM)) * B_BLOCK_DIM ** -0.5,
        'rg_bx': 0.01 * nrm(17, (N_EVEN, B_WIDTH)),
        'rg_lam': jnp.log(a_base) - jnp.log1p(-a_base),
        'w_out_e': nrm(19, (N_EVEN, A_WIDTH + B_WIDTH, D_MODEL)) * (A_WIDTH + B_WIDTH) ** -0.5 * BETA,
        'w_in_o': nrm(20, (N_ODD, D_MODEL, 2 * C_WIDTH)) * D_MODEL ** -0.5,
        'sgu_g': 1.0 + 0.1 * nrm(21, (N_ODD, C_WIDTH)),
        'sgu_b': 0.01 * nrm(22, (N_ODD, C_WIDTH)),
        'sgu_w': nrm(23, (N_ODD, C_GROUPS, C_LEN, C_LEN)) * C_LEN ** -0.5,
        'sgu_bias': 1.0 + 0.1 * nrm(24, (N_ODD, C_GROUPS, C_LEN)),
        'w_out_o': nrm(25, (N_ODD, C_WIDTH, D_MODEL)) * C_WIDTH ** -0.5 * BETA,
        'ln_mix_g': 1.0 + 0.1 * nrm(26, (DEPTH, D_MODEL)),
        'ln_mix_b': 0.01 * nrm(27, (DEPTH, D_MODEL)),
        'ln_ffn_g': 1.0 + 0.1 * nrm(28, (DEPTH, D_MODEL)),
        'ln_ffn_b': 0.01 * nrm(29, (DEPTH, D_MODEL)),
        'ffn_w_up': nrm(30, (DEPTH, D_MODEL, 2 * D_FF)) * D_MODEL ** -0.5,
        'ffn_conv_w': nrm(31, (DEPTH, FFN_CONV, 2 * D_FF)) * FFN_CONV ** -0.5,
        'ffn_conv_b': 0.01 * nrm(32, (DEPTH, 2 * D_FF)),
        'ffn_w_down': nrm(33, (DEPTH, D_FF, D_MODEL)) * D_FF ** -0.5 * BETA,
    }


def reference(x_prompt, x_sample, cache_k, cache_v, cache_logf, state_rglru_conv, state_rglru_h,
              state_ffn_conv, w_in_e, b_f, rg_conv_w, rg_conv_b, rg_wa, rg_ba, rg_wx, rg_bx, rg_lam,
              w_out_e, w_in_o, sgu_g, sgu_b, sgu_w, sgu_bias, w_out_o, ln_mix_g, ln_mix_b,
              ln_ffn_g, ln_ffn_b, ffn_w_up, ffn_conv_w, ffn_conv_b, ffn_w_down):
    xp, xs = x_prompt, x_sample
    kp_l, vp_l, lfp_l, rcp_l, rhp_l = [], [], [], [], []
    ks_l, vs_l, lfs_l, rcs_l, rhs_l = [], [], [], [], []
    fcp_l, fcs_l, sgv_l = [], [], []
    for layer in range(DEPTH):
        if layer % 2 == 0:
            e = layer // 2
            prm = (w_in_e[e], b_f[e], rg_conv_w[e], rg_conv_b[e], rg_wa[e], rg_ba[e],
                   rg_wx[e], rg_bx[e], rg_lam[e], w_out_e[e])
            mp, (kp, vp, lfp, rcp, rhp) = even_mixer(xp, None, *prm)
            past = (cache_k[e], cache_v[e], cache_logf[e], state_rglru_conv[e], state_rglru_h[e])
            ms, (ks_, vs_, lfs, rcs, rhs) = even_mixer(xs, past, *prm)
            kp_l.append(kp); vp_l.append(vp); lfp_l.append(lfp); rcp_l.append(rcp); rhp_l.append(rhp)
            ks_l.append(ks_); vs_l.append(vs_); lfs_l.append(lfs); rcs_l.append(rcs); rhs_l.append(rhs)
        else:
            o = layer // 2
            prm = (w_in_o[o], sgu_g[o], sgu_b[o], sgu_w[o], sgu_bias[o], w_out_o[o])
            mp, _ = odd_mixer(xp, *prm)
            ms, sv = odd_mixer(xs, *prm)
            sgv_l.append(sv)
        xp = layer_norm(ALPHA * xp + mp, ln_mix_g[layer], ln_mix_b[layer])
        xs = layer_norm(ALPHA * xs + ms, ln_mix_g[layer], ln_mix_b[layer])
        fprm = (ffn_w_up[layer], ffn_conv_w[layer], ffn_conv_b[layer], ffn_w_down[layer])
        zero_fc = jnp.zeros((xp.shape[0], FFN_CONV - 1, 2 * D_FF), xp.dtype)
        fp, fcp = conv_ffn(xp, zero_fc, *fprm)
        fs, fcs = conv_ffn(xs, state_ffn_conv[layer], *fprm)
        fcp_l.append(fcp); fcs_l.append(fcs)
        xp = layer_norm(ALPHA * xp + fp, ln_ffn_g[layer], ln_ffn_b[layer])
        xs = layer_norm(ALPHA * xs + fs, ln_ffn_g[layer], ln_ffn_b[layer])
    return (xp, xs,
            jnp.stack(kp_l), jnp.stack(vp_l), jnp.stack(lfp_l),
            jnp.stack(ks_l), jnp.stack(vs_l), jnp.stack(lfs_l),
            jnp.stack(rcp_l), jnp.stack(rhp_l), jnp.stack(rcs_l), jnp.stack(rhs_l),
            jnp.stack(fcp_l), jnp.stack(fcs_l), jnp.stack(sgv_l))
```

```python
import functools

import jax
import jax.numpy as jnp
from jax import lax
from jax.experimental import pallas as pl
from jax.experimental.pallas import tpu as pltpu

F32 = jnp.float32
BF16 = jnp.bfloat16

D_MODEL = 1024
N_HEADS = 8
HEAD_DIM = 64
A_WIDTH = N_HEADS * HEAD_DIM
B_WIDTH = 512
B_BLOCKS = 8
RG_CONV = 4
RG_C = 8.0
C_WIDTH = 1024
C_GROUPS = 8
C_LEN = 128
D_FF = 2816
FFN_CONV = 3
DEPTH = 2
ALPHA = (2 * DEPTH) ** 0.25
LN_EPS = 1e-5

LANES = 128
SUBLANES = 8
HALO = SUBLANES
NEG = -1e30
ROW_BLOCK = 512
FFN_CHUNK = 256
VMEM_LIMIT = 48 * 1024 * 1024


def _layer_norm(x, g, b):
    mu = jnp.mean(x, axis=-1, keepdims=True)
    xc = x - mu
    var = jnp.mean(xc * xc, axis=-1, keepdims=True)
    return xc * lax.rsqrt(var + LN_EPS) * g + b


def _log_sigmoid(x):
    return jnp.minimum(x, 0.0) - jnp.log1p(jnp.exp(-jnp.abs(x)))


def _softplus(x):
    return jnp.maximum(x, 0.0) + jnp.log1p(jnp.exp(-jnp.abs(x)))


def _split3(x):
    hi = x.astype(BF16)
    r1 = x - hi.astype(F32)
    mid = r1.astype(BF16)
    lo = (r1 - mid.astype(F32)).astype(BF16)
    return hi, mid, lo


def _dot(a, b):
    return jnp.dot(a, b, preferred_element_type=F32)


def _dot_nt(a, b):
    return lax.dot_general(a, b, (((1,), (1,)), ((), ())), preferred_element_type=F32)


def _dot3(pieces, ones_matrix, left):
    out = None
    for p in pieces:
        t = _dot(ones_matrix, p) if left else _dot(p, ones_matrix)
        out = t if out is None else out + t
    return out


def _const_spec(shape):
    nd = len(shape)
    return pl.BlockSpec(shape, lambda *_: (0,) * nd, pipeline_mode=pl.Buffered(1))


def _params(sem):
    return pltpu.CompilerParams(dimension_semantics=sem, vmem_limit_bytes=VMEM_LIMIT)


def _plan(n_rows, seq_len):
    tm = min(ROW_BLOCK, n_rows)
    nseq = max(1, tm // seq_len)
    bps = max(1, seq_len // tm)
    assert n_rows % tm == 0 and (seq_len % tm == 0 or tm % seq_len == 0)
    return tm, nseq, bps


def _even_in_kernel(x_ref, wqkv_ref, wf_ref, wxg_ref, bf_ref, tri_ref,
                    qkv16_ref, k_ref, v_ref, logf_ref, cum_ref, xb_ref, gb_ref,
                    carry_ref, *, tm, bps):
    i = pl.program_id(0)
    x16 = x_ref[...].astype(BF16)
    qkv = _dot(x16, wqkv_ref[...])
    qkv16_ref[:, :A_WIDTH] = (qkv[:, :A_WIDTH] * (HEAD_DIM ** -0.5)).astype(BF16)
    qkv16_ref[:, A_WIDTH:] = qkv[:, A_WIDTH:].astype(BF16)
    k_ref[...] = qkv[:, A_WIDTH:2 * A_WIDTH]
    v_ref[...] = qkv[:, 2 * A_WIDTH:]
    logf = _log_sigmoid(_dot(x16, wf_ref[...]) + bf_ref[...])
    logf_ref[...] = logf
    cs = _dot3(_split3(logf), tri_ref[...], left=True)
    if bps > 1:
        @pl.when(i % bps == 0)
        def _():
            carry_ref[...] = jnp.zeros_like(carry_ref)
        cs = cs + carry_ref[...]
        carry_ref[...] = cs[tm - 1:tm, :]
    cum_ref[...] = cs
    xg = _dot(x16, wxg_ref[...])
    xb_ref[...] = xg[:, :B_WIDTH]
    gb_ref[...] = xg[:, B_WIDTH:]


def _even_in(x2d, seq_len, wqkv16, wf16, wxg16, bf_pad):
    n = x2d.shape[0]
    tm, nseq, bps = _plan(n, seq_len)
    r = jnp.arange(tm)
    tri = ((r[None, :] <= r[:, None]) & ((r[None, :] // seq_len) == (r[:, None] // seq_len))).astype(BF16)
    row = lambda w: pl.BlockSpec((tm, w), lambda i: (i, 0))
    out_shapes = (
        jax.ShapeDtypeStruct((n, 3 * A_WIDTH), BF16),
        jax.ShapeDtypeStruct((n, A_WIDTH), F32),
        jax.ShapeDtypeStruct((n, A_WIDTH), F32),
        jax.ShapeDtypeStruct((n, LANES), F32),
        jax.ShapeDtypeStruct((n, LANES), F32),
        jax.ShapeDtypeStruct((n, B_WIDTH), F32),
        jax.ShapeDtypeStruct((n, B_WIDTH), F32),
    )
    return pl.pallas_call(
        functools.partial(_even_in_kernel, tm=tm, bps=bps),
        grid=(n // tm,),
        in_specs=[row(D_MODEL), _const_spec(wqkv16.shape), _const_spec(wf16.shape),
                  _const_spec(wxg16.shape), _const_spec(bf_pad.shape), _const_spec(tri.shape)],
        out_specs=(row(3 * A_WIDTH), row(A_WIDTH), row(A_WIDTH), row(LANES), row(LANES),
                   row(B_WIDTH), row(B_WIDTH)),
        out_shape=out_shapes,
        scratch_shapes=[pltpu.VMEM((1, LANES), F32)],
        compiler_params=_params(("arbitrary",)),
        name="even_in_proj",
    )(x2d, wqkv16, wf16, wxg16, bf_pad, tri)


def _softmax_step(s, cq, m_old, l_old, acc_old, v16):
    m_new = jnp.maximum(m_old, jnp.max(s, axis=1, keepdims=True) + cq)
    p = jnp.exp(s - (m_new - cq))
    alpha = jnp.exp(m_old - m_new)
    l_new = alpha * l_old + jnp.sum(p, axis=1, keepdims=True)
    acc_new = alpha * acc_old + _dot(p.astype(BF16), v16)
    return m_new, l_new, acc_new


def _fox_prompt_kernel(q_ref, k_ref, v_ref, cq_ref, ck_ref, o_ref,
                       qm_ref, m_ref, l_ref, acc_ref, *, blk):
    i = pl.program_id(1)
    j = pl.program_id(2)
    lane = lax.broadcasted_iota(jnp.int32, (blk, LANES), 1)

    @pl.when(j == 0)
    def _():
        for h in range(N_HEADS):
            hp, hh = divmod(h, 2)
            q = q_ref[0, :, hp * LANES:(hp + 1) * LANES]
            keep = (lane < HEAD_DIM) if hh == 0 else (lane >= HEAD_DIM)
            qm_ref[h] = jnp.where(keep, q, jnp.zeros_like(q))
        m_ref[...] = jnp.full(m_ref.shape, NEG, F32)
        l_ref[...] = jnp.zeros_like(l_ref)
        acc_ref[...] = jnp.zeros_like(acc_ref)

    def update(diagonal):
        if diagonal:
            rows = lax.broadcasted_iota(jnp.int32, (blk, blk), 0)
            cols = lax.broadcasted_iota(jnp.int32, (blk, blk), 1)
            visible = cols <= rows
        for h in range(N_HEADS):
            hp = h // 2
            k16 = k_ref[0, :, hp * LANES:(hp + 1) * LANES]
            v16 = v_ref[0, :, hp * LANES:(hp + 1) * LANES]
            s = _dot_nt(qm_ref[h], k16) - ck_ref[0, h:h + 1, :]
            if diagonal:
                s = jnp.where(visible, s, NEG)
            cq = cq_ref[0, :, h:h + 1]
            m_new, l_new, acc_new = _softmax_step(s, cq, m_ref[h], l_ref[h], acc_ref[h], v16)
            m_ref[h] = m_new
            l_ref[h] = l_new
            acc_ref[h] = acc_new

    @pl.when(j < i)
    def _():
        update(False)

    @pl.when(j == i)
    def _():
        update(True)
        for hp in range(N_HEADS // 2):
            o0 = acc_ref[2 * hp] / l_ref[2 * hp]
            o1 = acc_ref[2 * hp + 1] / l_ref[2 * hp + 1]
            o_ref[0, :, hp * LANES:(hp + 1) * LANES] = jnp.where(lane < HEAD_DIM, o0, o1).astype(o_ref.dtype)


def _fox_prompt(qkv16, cum, cum_t, blk):
    bsz, t, _ = qkv16.shape
    nb = t // blk
    kv_map = lambda col: (lambda b, i, j: (b, jnp.minimum(j, i), col))
    return pl.pallas_call(
        functools.partial(_fox_prompt_kernel, blk=blk),
        grid=(bsz, nb, nb),
        in_specs=[pl.BlockSpec((1, blk, A_WIDTH), lambda b, i, j: (b, i, 0)),
                  pl.BlockSpec((1, blk, A_WIDTH), kv_map(1)),
                  pl.BlockSpec((1, blk, A_WIDTH), kv_map(2)),
                  pl.BlockSpec((1, blk, LANES), lambda b, i, j: (b, i, 0)),
                  pl.BlockSpec((1, N_HEADS, blk), lambda b, i, j: (b, 0, jnp.minimum(j, i)))],
        out_specs=pl.BlockSpec((1, blk, A_WIDTH), lambda b, i, j: (b, i, 0)),
        out_shape=jax.ShapeDtypeStruct((bsz, t, A_WIDTH), BF16),
        scratch_shapes=[pltpu.VMEM((N_HEADS, blk, LANES), BF16),
                        pltpu.VMEM((N_HEADS, blk, 1), F32),
                        pltpu.VMEM((N_HEADS, blk, 1), F32),
                        pltpu.VMEM((N_HEADS, blk, LANES), F32)],
        compiler_params=_params(("arbitrary", "arbitrary", "arbitrary")),
        name="fox_prompt",
    )(qkv16, qkv16, qkv16, cum, cum_t)


def _fox_sample_kernel(qkv_ref, cq_ref, ckn_ref, kc_ref, vc_ref, lf_ref, upper_ref, o_ref,
                       ckc_ref, *, t, plen, chunk):
    lane = lax.broadcasted_iota(jnp.int32, (t, LANES), 1)
    carry = jnp.zeros((N_HEADS, 1), F32)
    for c in reversed(range(plen // chunk)):
        lf = lf_ref[0, :, c * chunk:(c + 1) * chunk]
        after = _dot3(_split3(lf), upper_ref[...], left=False)
        ckc_ref[:, c * chunk:(c + 1) * chunk] = -(after + carry)
        carry = carry + jnp.sum(lf, axis=1, keepdims=True)
    rows = lax.broadcasted_iota(jnp.int32, (t, t), 0)
    cols = lax.broadcasted_iota(jnp.int32, (t, t), 1)
    visible = cols <= rows
    for hp in range(N_HEADS // 2):
        sl = slice(hp * LANES, (hp + 1) * LANES)
        q = qkv_ref[0, :, hp * LANES:(hp + 1) * LANES]
        kn16 = qkv_ref[0, :, A_WIDTH + hp * LANES:A_WIDTH + (hp + 1) * LANES]
        vn16 = qkv_ref[0, :, 2 * A_WIDTH + hp * LANES:2 * A_WIDTH + (hp + 1) * LANES]
        kc16 = kc_ref[0, :, sl].astype(BF16)
        vc16 = vc_ref[0, :, sl].astype(BF16)
        outs = []
        for hh in range(2):
            h = 2 * hp + hh
            keep = (lane < HEAD_DIM) if hh == 0 else (lane >= HEAD_DIM)
            qm = jnp.where(keep, q, jnp.zeros_like(q))
            cq = cq_ref[0, :, h:h + 1]
            s_new = jnp.where(visible, _dot_nt(qm, kn16) - ckn_ref[0, h:h + 1, :], NEG)
            m0 = jnp.full((t, 1), NEG, F32)
            z = jnp.zeros((t, 1), F32)
            m, l, acc = _softmax_step(s_new, cq, m0, z, jnp.zeros((t, LANES), F32), vn16)
            s_old = _dot_nt(qm, kc16) - ckc_ref[h:h + 1, :]
            m, l, acc = _softmax_step(s_old, cq, m, l, acc, vc16)
            outs.append(acc / l)
        o_ref[0, :, sl] = jnp.where(lane < HEAD_DIM, outs[0], outs[1]).astype(o_ref.dtype)


def _fox_sample(qkv16, cum, cum_t, kc, vc, lf_t, chunk=512):
    bsz, t, _ = qkv16.shape
    plen = kc.shape[1]
    r = jnp.arange(chunk)
    upper = (r[:, None] > r[None, :]).astype(BF16)
    per_b = lambda shape: pl.BlockSpec((1,) + shape, lambda b: (b, 0, 0))
    return pl.pallas_call(
        functools.partial(_fox_sample_kernel, t=t, plen=plen, chunk=chunk),
        grid=(bsz,),
        in_specs=[per_b((t, 3 * A_WIDTH)), per_b((t, LANES)), per_b((N_HEADS, t)),
                  per_b((plen, A_WIDTH)), per_b((plen, A_WIDTH)), per_b((N_HEADS, plen)),
                  _const_spec(upper.shape)],
        out_specs=per_b((t, A_WIDTH)),
        out_shape=jax.ShapeDtypeStruct((bsz, t, A_WIDTH), BF16),
        scratch_shapes=[pltpu.VMEM((N_HEADS, plen), F32)],
        compiler_params=_params(("arbitrary",)),
        name="fox_sample",
    )(qkv16, cum, cum_t, kc, vc, lf_t, upper)


def _even_tail_kernel(x_ref, attn_ref, xb_ref, gb_ref, cs_ref, h0_ref, cw_ref, cb_ref,
                      wa_ref, ba_ref, wx_ref, bx_ref, lam_ref, woa_ref, wob_ref, g_ref, b_ref,
                      y_ref, hl_ref, s_ref, ccarry_ref, hcarry_ref, *, tm, nseq, bps):
    i = pl.program_id(0)
    lb = tm // nseq
    xb = xb_ref[...]
    if bps > 1:
        first = (i % bps) == 0
        halos = [jnp.where(first, cs_ref[0], ccarry_ref[...])]
        hinit = jnp.where(first, h0_ref[0], hcarry_ref[...])
        ccarry_ref[...] = xb[tm - HALO:, :]
    else:
        halos = [cs_ref[s] for s in range(nseq)]
        hinit = jnp.concatenate(
            [jnp.broadcast_to(h0_ref[s], (lb, B_WIDTH)) for s in range(nseq)], axis=0)
    for s in range(nseq):
        base = s * (lb + HALO)
        s_ref[base:base + HALO, :] = halos[s]
        s_ref[base + HALO:base + HALO + lb, :] = xb[s * lb:(s + 1) * lb, :]
    pieces = []
    for s in range(nseq):
        base = s * (lb + HALO) + HALO
        xc = cb_ref[...] + cw_ref[RG_CONV - 1:RG_CONV, :] * xb[s * lb:(s + 1) * lb, :]
        for d in range(1, RG_CONV):
            xc = xc + cw_ref[RG_CONV - 1 - d:RG_CONV - d, :] * s_ref[base - d:base - d + lb, :]
        pieces.append(xc)
    xc = pieces[0] if nseq == 1 else jnp.concatenate(pieces, axis=0)
    xc16 = xc.astype(BF16)
    gate_r = jax.nn.sigmoid(_dot(xc16, wa_ref[...]) + ba_ref[...])
    gate_i = jax.nn.sigmoid(_dot(xc16, wx_ref[...]) + bx_ref[...])
    log_a = -RG_C * gate_r * _softplus(-lam_ref[...])
    a = jnp.exp(log_a)
    u = jnp.sqrt(jnp.tanh(-log_a) * (a * a + 1.0)) * (gate_i * xc)
    pos = lax.broadcasted_iota(jnp.int32, (tm, B_WIDTH), 0) % lb
    d = 1
    while d < lb:
        inside = pos >= d
        u = jnp.where(inside, a * pltpu.roll(u, d, axis=0) + u, u)
        a = jnp.where(inside, a * pltpu.roll(a, d, axis=0), a)
        d *= 2
    h = u + a * hinit
    if bps > 1:
        hcarry_ref[...] = h[tm - 1:tm, :]
    for s in range(nseq):
        hl_ref[0, s:s + 1, :] = h[(s + 1) * lb - 1:(s + 1) * lb, :]
    gated = (jax.nn.gelu(gb_ref[...]) * h).astype(BF16)
    y = _dot(attn_ref[...], woa_ref[...]) + _dot(gated, wob_ref[...])
    y_ref[...] = _layer_norm(ALPHA * x_ref[...] + y, g_ref[...], b_ref[...])


def _even_tail(x2d, attn16, xb, gb, conv_state, h0, seq_len, prm):
    n = x2d.shape[0]
    tm, nseq, bps = _plan(n, seq_len)
    nblk = n // tm
    cs = jnp.pad(conv_state, ((0, 0), (HALO - (RG_CONV - 1), 0), (0, 0)))
    h03 = h0.astype(F32)[:, None, :]
    row = lambda w: pl.BlockSpec((tm, w), lambda i: (i, 0))
    state = lambda r: pl.BlockSpec((nseq, r, B_WIDTH), lambda i: (i // bps, 0, 0))
    consts = [prm[k] for k in ("rg_cw", "rg_cb", "rg_wa", "rg_ba", "rg_wx", "rg_bx", "rg_lam",
                                "wo_a", "wo_b", "ln_g", "ln_b")]
    y, hl = pl.pallas_call(
        functools.partial(_even_tail_kernel, tm=tm, nseq=nseq, bps=bps),
        grid=(nblk,),
        in_specs=[row(D_MODEL), row(A_WIDTH), row(B_WIDTH), row(B_WIDTH), state(HALO), state(1)]
                 + [_const_spec(c.shape) for c in consts],
        out_specs=(row(D_MODEL), pl.BlockSpec((1, nseq, B_WIDTH), lambda i: (i, 0, 0))),
        out_shape=(jax.ShapeDtypeStruct((n, D_MODEL), F32),
                   jax.ShapeDtypeStruct((nblk, nseq, B_WIDTH), F32)),
        scratch_shapes=[pltpu.VMEM((nseq * (tm // nseq + HALO), B_WIDTH), F32),
                        pltpu.VMEM((HALO, B_WIDTH), F32),
                        pltpu.VMEM((1, B_WIDTH), F32)],
        compiler_params=_params(("arbitrary",)),
        name="even_tail",
    )(x2d, attn16, xb, gb, cs, h03, *consts)
    h_last = hl.reshape(n // seq_len, -1, B_WIDTH)[:, -1, :]
    return y, h_last


def _ffn_kernel(x_ref, st_ref, wg_ref, wu_ref, cwg_ref, cwu_ref, wd_ref, g_ref, b_ref,
                y_ref, so_ref, sg_ref, su_ref, a_ref, cg_ref, cu_ref, acc_ref,
                *, tm, nseq, bps, nchunk):
    i = pl.program_id(0)
    lb = tm // nseq
    x = x_ref[...]
    x16 = x.astype(BF16)
    acc_ref[...] = jnp.zeros_like(acc_ref)
    first = (i % bps) == 0

    def conv_branch(c, h, w_ref, carry_ref, s_ref, slot):
        taps = w_ref[c]
        for s in range(nseq):
            base = s * (lb + HALO)
            if bps > 1:
                halo = jnp.where(first, st_ref[0, slot], carry_ref[c])
            else:
                halo = st_ref[s, slot]
            s_ref[base:base + HALO, :] = halo
            s_ref[base + HALO:base + HALO + lb, :] = h[s * lb:(s + 1) * lb, :]
            so_ref[s, slot] = h[(s + 1) * lb - HALO:(s + 1) * lb, :]
        if bps > 1:
            carry_ref[c] = h[tm - HALO:, :]
        outs = []
        for s in range(nseq):
            base = s * (lb + HALO) + HALO
            y = taps[3:4, :] + taps[2:3, :] * h[s * lb:(s + 1) * lb, :]
            y = y + taps[1:2, :] * s_ref[base - 1:base - 1 + lb, :]
            y = y + taps[0:1, :] * s_ref[base - 2:base - 2 + lb, :]
            outs.append(y)
        return outs

    def chunk(c, carry):
        hg = _dot(x16, wg_ref[c])
        hu = _dot(x16, wu_ref[c])
        yg = conv_branch(c, hg, cwg_ref, cg_ref, sg_ref, c)
        yu = conv_branch(c, hu, cwu_ref, cu_ref, su_ref, nchunk + c)
        for s in range(nseq):
            a_ref[s * lb:(s + 1) * lb, :] = (jax.nn.gelu(yg[s]) * yu[s]).astype(BF16)
        acc_ref[...] += _dot(a_ref[...], wd_ref[c])
        return carry

    lax.fori_loop(0, nchunk, chunk, 0)
    y_ref[...] = _layer_norm(ALPHA * x + acc_ref[...], g_ref[...], b_ref[...])


def _ffn(x2d, state, seq_len, prm):
    n = x2d.shape[0]
    tm, nseq, bps = _plan(n, seq_len)
    nblk = n // tm
    tn = FFN_CHUNK
    nchunk = D_FF // tn
    bsz = state.shape[0]
    st = jnp.pad(state, ((0, 0), (HALO - (FFN_CONV - 1), 0), (0, 0)))
    st = st.reshape(bsz, HALO, 2 * nchunk, tn).transpose(0, 2, 1, 3)
    row = pl.BlockSpec((tm, D_MODEL), lambda i: (i, 0))
    consts = [prm[k] for k in ("w_g", "w_u", "cw_g", "cw_u", "w_d", "ln_g", "ln_b")]
    y, so = pl.pallas_call(
        functools.partial(_ffn_kernel, tm=tm, nseq=nseq, bps=bps, nchunk=nchunk),
        grid=(nblk,),
        in_specs=[row, pl.BlockSpec((nseq, 2 * nchunk, HALO, tn), lambda i: (i // bps, 0, 0, 0))]
                 + [_const_spec(c.shape) for c in consts],
        out_specs=(row, pl.BlockSpec((nseq, 2 * nchunk, HALO, tn), lambda i: (i, 0, 0, 0))),
        out_shape=(jax.ShapeDtypeStruct((n, D_MODEL), F32),
                   jax.ShapeDtypeStruct((nblk * nseq, 2 * nchunk, HALO, tn), F32)),
        scratch_shapes=[pltpu.VMEM((nseq * (tm // nseq + HALO), tn), F32),
                        pltpu.VMEM((nseq * (tm // nseq + HALO), tn), F32),
                        pltpu.VMEM((tm, tn), BF16),
                        pltpu.VMEM((nchunk, HALO, tn), F32),
                        pltpu.VMEM((nchunk, HALO, tn), F32),
                        pltpu.VMEM((tm, D_MODEL), F32)],
        compiler_params=_params(("arbitrary",)),
        name="conv_ffn",
    )(x2d, st, *consts)
    so = so.reshape(bsz, -1, 2 * nchunk, HALO, tn)[:, -1]
    new_state = so[:, :, HALO - (FFN_CONV - 1):, :].transpose(0, 2, 1, 3).reshape(bsz, FFN_CONV - 1, 2 * D_FF)
    return y, new_state


def _odd_kernel(x_ref, wu_ref, wv_ref, sg_ref, sb_ref, ws_ref, bias_ref, wo_ref, g_ref, b_ref,
                y_ref, v_out_ref, gate_ref, *, tm, lc):
    x = x_ref[...]
    x16 = x.astype(BF16)
    u = jax.nn.gelu(_dot(x16, wu_ref[...]))
    v = _layer_norm(jax.nn.gelu(_dot(x16, wv_ref[...])), sg_ref[...], sb_ref[...])
    v_out_ref[...] = v
    v16 = v.astype(BF16)
    gw = C_WIDTH // C_GROUPS
    for c in range(tm // lc):
        rows = slice(c * lc, (c + 1) * lc)
        for g in range(C_GROUPS):
            cols = slice(g * gw, (g + 1) * gw)
            mixed = _dot(ws_ref[g], v16[rows, cols]) + bias_ref[:, cols]
            gate_ref[rows, cols] = (u[rows, cols] * mixed).astype(BF16)
    y = _dot(gate_ref[...], wo_ref[...])
    y_ref[...] = _layer_norm(ALPHA * x + y, g_ref[...], b_ref[...])


def _odd(x2d, seq_len, prm):
    n = x2d.shape[0]
    tm = min(ROW_BLOCK, n)
    lc = min(seq_len, C_LEN)
    gw = C_WIDTH // C_GROUPS
    ws16 = (prm["sgu_w"][:, :lc, :lc] * jnp.tril(jnp.ones((lc, lc), F32))).astype(BF16)
    bias = jnp.repeat(prm["sgu_bias"][:, :lc].T, gw, axis=1)
    consts = [prm["w_u"], prm["w_v"], prm["sgu_g"], prm["sgu_b"], ws16, bias,
              prm["w_o"], prm["ln_g"], prm["ln_b"]]
    row = pl.BlockSpec((tm, D_MODEL), lambda i: (i, 0))
    return pl.pallas_call(
        functools.partial(_odd_kernel, tm=tm, lc=lc),
        grid=(n // tm,),
        in_specs=[row] + [_const_spec(c.shape) for c in consts],
        out_specs=(row, pl.BlockSpec((tm, C_WIDTH), lambda i: (i, 0))),
        out_shape=(jax.ShapeDtypeStruct((n, D_MODEL), F32),
                   jax.ShapeDtypeStruct((n, C_WIDTH), F32)),
        scratch_shapes=[pltpu.VMEM((tm, C_WIDTH), BF16)],
        compiler_params=_params(("arbitrary",)),
        name="odd_mixer",
    )(x2d, *consts)


def _row(v):
    return v.astype(F32)[None, :]


def _block_diag(w):
    nb, bd, _ = w.shape
    eye = jnp.eye(nb, dtype=w.dtype)
    return jnp.einsum("ncd,nm->ncmd", w, eye).reshape(nb * bd, nb * bd)


def _even_params(w_in, b_f, rg_conv_w, rg_conv_b, rg_wa, rg_ba, rg_wx, rg_bx, rg_lam, w_out, ln_g, ln_b):
    nf = 3 * A_WIDTH
    wf = jnp.pad(w_in[:, nf:nf + N_HEADS], ((0, 0), (0, LANES - N_HEADS)))
    return {
        "wqkv": w_in[:, :nf].astype(BF16),
        "wf": wf.astype(BF16),
        "wxg": w_in[:, nf + N_HEADS:].astype(BF16),
        "bf": jnp.pad(b_f, (0, LANES - N_HEADS))[None, :],
        "rg_cw": rg_conv_w, "rg_cb": _row(rg_conv_b),
        "rg_wa": _block_diag(rg_wa).astype(BF16), "rg_ba": _row(rg_ba),
        "rg_wx": _block_diag(rg_wx).astype(BF16), "rg_bx": _row(rg_bx),
        "rg_lam": _row(rg_lam),
        "wo_a": w_out[:A_WIDTH].astype(BF16), "wo_b": w_out[A_WIDTH:].astype(BF16),
        "ln_g": _row(ln_g), "ln_b": _row(ln_b),
    }


def _ffn_params(w_up, conv_w, conv_b, w_down, ln_g, ln_b):
    tn = FFN_CHUNK
    nchunk = D_FF // tn
    chunked = lambda w: w.reshape(D_MODEL, nchunk, tn).transpose(1, 0, 2).astype(BF16)
    taps = jnp.concatenate([conv_w, conv_b[None, :],
                            jnp.zeros((HALO - FFN_CONV - 1, 2 * D_FF), F32)], axis=0)
    taps_c = lambda w: w.reshape(HALO, nchunk, tn).transpose(1, 0, 2)
    return {
        "w_g": chunked(w_up[:, :D_FF]), "w_u": chunked(w_up[:, D_FF:]),
        "cw_g": taps_c(taps[:, :D_FF]), "cw_u": taps_c(taps[:, D_FF:]),
        "w_d": w_down.reshape(nchunk, tn, D_MODEL).astype(BF16),
        "ln_g": _row(ln_g), "ln_b": _row(ln_b),
    }


def _odd_params(w_in, sgu_g, sgu_b, sgu_w, sgu_bias, w_out, ln_g, ln_b):
    return {
        "w_u": w_in[:, :C_WIDTH].astype(BF16), "w_v": w_in[:, C_WIDTH:].astype(BF16),
        "sgu_g": _row(sgu_g), "sgu_b": _row(sgu_b), "sgu_w": sgu_w, "sgu_bias": sgu_bias,
        "w_o": w_out.astype(BF16), "ln_g": _row(ln_g), "ln_b": _row(ln_b),
    }


def _even_layer(x3d, past, prm):
    bsz, t, _ = x3d.shape
    x2d = x3d.reshape(bsz * t, D_MODEL)
    qkv16, k, v, logf, cum, xb, gb = _even_in(x2d, t, prm["wqkv"], prm["wf"], prm["wxg"], prm["bf"])
    qkv16 = qkv16.reshape(bsz, t, 3 * A_WIDTH)
    cum3 = cum.reshape(bsz, t, LANES)
    cum_t = cum3[:, :, :N_HEADS].transpose(0, 2, 1)
    if past is None:
        attn = _fox_prompt(qkv16, cum3, cum_t, ROW_BLOCK)
        conv_s = jnp.zeros((bsz, RG_CONV - 1, B_WIDTH), F32)
        h0 = jnp.zeros((bsz, B_WIDTH), F32)
    else:
        k_c, v_c, lf_c, conv_s, h0 = past
        plen = k_c.shape[1]
        attn = _fox_sample(qkv16, cum3, cum_t, k_c.reshape(bsz, plen, A_WIDTH),
                           v_c.reshape(bsz, plen, A_WIDTH), lf_c.transpose(0, 2, 1))
    y, h_last = _even_tail(x2d, attn.reshape(bsz * t, A_WIDTH), xb, gb, conv_s, h0, t, prm)
    new_k = k.reshape(bsz, t, N_HEADS, HEAD_DIM)
    new_v = v.reshape(bsz, t, N_HEADS, HEAD_DIM)
    new_logf = logf.reshape(bsz, t, LANES)[:, :, :N_HEADS]
    xb3 = xb.reshape(bsz, t, B_WIDTH)
    conv_new = jnp.concatenate([conv_s, xb3], axis=1)[:, -(RG_CONV - 1):] if t < RG_CONV - 1 \
        else xb3[:, -(RG_CONV - 1):]
    return y.reshape(bsz, t, D_MODEL), (new_k, new_v, new_logf, conv_new, h_last)


def _ffn_layer(x3d, state, prm):
    bsz, t, _ = x3d.shape
    y, new_state = _ffn(x3d.reshape(bsz * t, D_MODEL), state, t, prm)
    return y.reshape(bsz, t, D_MODEL), new_state


def _odd_layer(x3d, prm):
    bsz, t, _ = x3d.shape
    y, v = _odd(x3d.reshape(bsz * t, D_MODEL), t, prm)
    return y.reshape(bsz, t, D_MODEL), v.reshape(bsz, t, C_WIDTH)


def kernel(x_prompt, x_sample, cache_k, cache_v, cache_logf, state_rglru_conv, state_rglru_h, state_ffn_conv, w_in_e, b_f, rg_conv_w, rg_conv_b, rg_wa, rg_ba, rg_wx, rg_bx, rg_lam, w_out_e, w_in_o, sgu_g, sgu_b, sgu_w, sgu_bias, w_out_o, ln_mix_g, ln_mix_b, ln_ffn_g, ln_ffn_b, ffn_w_up, ffn_conv_w, ffn_conv_b, ffn_w_down):
    xp, xs = x_prompt, x_sample
    even_p, even_s, ffn_p, ffn_s, sgu_v = [], [], [], [], []
    for layer in range(DEPTH):
        if layer % 2 == 0:
            e = layer // 2
            prm = _even_params(w_in_e[e], b_f[e], rg_conv_w[e], rg_conv_b[e], rg_wa[e], rg_ba[e],
                               rg_wx[e], rg_bx[e], rg_lam[e], w_out_e[e],
                               ln_mix_g[layer], ln_mix_b[layer])
            xp, st_p = _even_layer(xp, None, prm)
            past = (cache_k[e], cache_v[e], cache_logf[e], state_rglru_conv[e], state_rglru_h[e])
            xs, st_s = _even_layer(xs, past, prm)
            even_p.append(st_p)
            even_s.append(st_s)
        else:
            o = layer // 2
            prm = _odd_params(w_in_o[o], sgu_g[o], sgu_b[o], sgu_w[o], sgu_bias[o], w_out_o[o],
                              ln_mix_g[layer], ln_mix_b[layer])
            xp, _ = _odd_layer(xp, prm)
            xs, sv = _odd_layer(xs, prm)
            sgu_v.append(sv)
        fprm = _ffn_params(ffn_w_up[layer], ffn_conv_w[layer], ffn_conv_b[layer], ffn_w_down[layer],
                           ln_ffn_g[layer], ln_ffn_b[layer])
        zero_fc = jnp.zeros((xp.shape[0], FFN_CONV - 1, 2 * D_FF), F32)
        xp, fcp = _ffn_layer(xp, zero_fc, fprm)
        xs, fcs = _ffn_layer(xs, state_ffn_conv[layer], fprm)
        ffn_p.append(fcp)
        ffn_s.append(fcs)
    stack = lambda lst, idx: jnp.stack([item[idx] for item in lst])
    return (xp, xs,
            stack(even_p, 0), stack(even_p, 1), stack(even_p, 2),
            stack(even_s, 0), stack(even_s, 1), stack(even_s, 2),
            stack(even_p, 3), stack(even_p, 4), stack(even_s, 3), stack(even_s, 4),
            jnp.stack(ffn_p), jnp.stack(ffn_s), jnp.stack(sgu_v))
```

```python
import functools

import jax
import jax.numpy as jnp
from jax import lax
from jax.experimental import pallas as pl
from jax.experimental.pallas import tpu as pltpu

F32 = jnp.float32
BF16 = jnp.bfloat16

D_MODEL = 1024
N_HEADS = 8
HEAD_DIM = 64
A_WIDTH = N_HEADS * HEAD_DIM
B_WIDTH = 512
B_BLOCKS = 8
RG_CONV = 4
RG_C = 8.0
C_WIDTH = 1024
C_GROUPS = 8
C_LEN = 128
D_FF = 2816
FFN_CONV = 3
DEPTH = 2
ALPHA = (2 * DEPTH) ** 0.25
LN_EPS = 1e-5

LANES = 128
SUBLANES = 8
HALO = SUBLANES
NEG = -1e30
ROW_BLOCK = 512
FFN_CHUNK = 256
VMEM_LIMIT = 48 * 1024 * 1024


def _layer_norm(x, g, b):
    mu = jnp.mean(x, axis=-1, keepdims=True)
    xc = x - mu
    var = jnp.mean(xc * xc, axis=-1, keepdims=True)
    return xc * lax.rsqrt(var + LN_EPS) * g + b


def _log_sigmoid(x):
    return jnp.minimum(x, 0.0) - jnp.log1p(jnp.exp(-jnp.abs(x)))


def _softplus(x):
    return jnp.maximum(x, 0.0) + jnp.log1p(jnp.exp(-jnp.abs(x)))


def _split3(x):
    hi = x.astype(BF16)
    r1 = x - hi.astype(F32)
    mid = r1.astype(BF16)
    lo = (r1 - mid.astype(F32)).astype(BF16)
    return hi, mid, lo


def _dot(a, b):
    return jnp.dot(a, b, preferred_element_type=F32)


def _dot_nt(a, b):
    return lax.dot_general(a, b, (((1,), (1,)), ((), ())), preferred_element_type=F32)


def _dot3(pieces, ones_matrix, left):
    out = None
    for p in pieces:
        t = _dot(ones_matrix, p) if left else _dot(p, ones_matrix)
        out = t if out is None else out + t
    return out


def _const_spec(shape):
    nd = len(shape)
    return pl.BlockSpec(shape, lambda *_: (0,) * nd, pipeline_mode=pl.Buffered(1))


def _params(sem):
    return pltpu.CompilerParams(dimension_semantics=sem, vmem_limit_bytes=VMEM_LIMIT)


def _plan(n_rows, seq_len):
    tm = min(ROW_BLOCK, n_rows)
    nseq = max(1, tm // seq_len)
    bps = max(1, seq_len // tm)
    assert n_rows % tm == 0 and (seq_len % tm == 0 or tm % seq_len == 0)
    return tm, nseq, bps


def _even_in_kernel(x_ref, wqkv_ref, wf_ref, wxg_ref, bf_ref, tri_ref,
                    qkv16_ref, k_ref, v_ref, logf_ref, cum_ref, xb_ref, gb_ref,
                    carry_ref, *, tm, bps):
    i = pl.program_id(0)
    x16 = x_ref[...].astype(BF16)
    qkv = _dot(x16, wqkv_ref[...])
    qkv16_ref[:, :A_WIDTH] = (qkv[:, :A_WIDTH] * (HEAD_DIM ** -0.5)).astype(BF16)
    qkv16_ref[:, A_WIDTH:] = qkv[:, A_WIDTH:].astype(BF16)
    k_ref[...] = qkv[:, A_WIDTH:2 * A_WIDTH]
    v_ref[...] = qkv[:, 2 * A_WIDTH:]
    logf = _log_sigmoid(_dot(x16, wf_ref[...]) + bf_ref[...])
    logf_ref[...] = logf
    cs = _dot3(_split3(logf), tri_ref[...], left=True)
    if bps > 1:
        @pl.when(i % bps == 0)
        def _():
            carry_ref[...] = jnp.zeros_like(carry_ref)
        cs = cs + carry_ref[...]
        carry_ref[...] = cs[tm - 1:tm, :]
    cum_ref[...] = cs
    xg = _dot(x16, wxg_ref[...])
    xb_ref[...] = xg[:, :B_WIDTH]
    gb_ref[...] = xg[:, B_WIDTH:]


def _even_in(x2d, seq_len, wqkv16, wf16, wxg16, bf_pad):
    n = x2d.shape[0]
    tm, nseq, bps = _plan(n, seq_len)
    r = jnp.arange(tm)
    tri = ((r[None, :] <= r[:, None]) & ((r[None, :] // seq_len) == (r[:, None] // seq_len))).astype(BF16)
    row = lambda w: pl.BlockSpec((tm, w), lambda i: (i, 0))
    out_shapes = (
        jax.ShapeDtypeStruct((n, 3 * A_WIDTH), BF16),
        jax.ShapeDtypeStruct((n, A_WIDTH), F32),
        jax.ShapeDtypeStruct((n, A_WIDTH), F32),
        jax.ShapeDtypeStruct((n, LANES), F32),
        jax.ShapeDtypeStruct((n, LANES), F32),
        jax.ShapeDtypeStruct((n, B_WIDTH), F32),
        jax.ShapeDtypeStruct((n, B_WIDTH), F32),
    )
    return pl.pallas_call(
        functools.partial(_even_in_kernel, tm=tm, bps=bps),
        grid=(n // tm,),
        in_specs=[row(D_MODEL), _const_spec(wqkv16.shape), _const_spec(wf16.shape),
                  _const_spec(wxg16.shape), _const_spec(bf_pad.shape), _const_spec(tri.shape)],
        out_specs=(row(3 * A_WIDTH), row(A_WIDTH), row(A_WIDTH), row(LANES), row(LANES),
                   row(B_WIDTH), row(B_WIDTH)),
        out_shape=out_shapes,
        scratch_shapes=[pltpu.VMEM((1, LANES), F32)],
        compiler_params=_params(("arbitrary",)),
        name="even_in_proj",
    )(x2d, wqkv16, wf16, wxg16, bf_pad, tri)


def _softmax_step(s, cq, m_old, l_old, acc_old, v16):
    m_new = jnp.maximum(m_old, jnp.max(s, axis=1, keepdims=True) + cq)
    p = jnp.exp(s - (m_new - cq))
    alpha = jnp.exp(m_old - m_new)
    l_new = alpha * l_old + jnp.sum(p, axis=1, keepdims=True)
    acc_new = alpha * acc_old + _dot(p.astype(BF16), v16)
    return m_new, l_new, acc_new


def _fox_prompt_kernel(q_ref, k_ref, v_ref, cq_ref, ck_ref, o_ref,
                       qm_ref, cqr_ref, m_ref, l_ref, acc_ref, *, blk):
    i = pl.program_id(1)
    lane = lax.broadcasted_iota(jnp.int32, (blk, LANES), 1)
    ngrp = blk // LANES

    for h in range(N_HEADS):
        hp, hh = divmod(h, 2)
        q = q_ref[0, :, hp * LANES:(hp + 1) * LANES]
        keep = (lane < HEAD_DIM) if hh == 0 else (lane >= HEAD_DIM)
        qm_ref[h] = jnp.where(keep, q, jnp.zeros_like(q))
        cqr_ref[h] = jnp.broadcast_to(cq_ref[0, :, h:h + 1], (blk, LANES))
    m_ref[...] = jnp.full(m_ref.shape, NEG, F32)
    l_ref[...] = jnp.zeros_like(l_ref)
    acc_ref[...] = jnp.zeros_like(acc_ref)

    def update(j, diagonal):
        koff = pl.multiple_of(j * blk, blk)
        if diagonal:
            rows = lax.broadcasted_iota(jnp.int32, (blk, blk), 0)
            cols = lax.broadcasted_iota(jnp.int32, (blk, blk), 1)
            visible = cols <= rows
        for h in range(N_HEADS):
            hp = h // 2
            k16 = k_ref[0, pl.ds(koff, blk), hp * LANES:(hp + 1) * LANES]
            v16 = v_ref[0, pl.ds(koff, blk), hp * LANES:(hp + 1) * LANES]
            s = _dot_nt(qm_ref[h], k16) - ck_ref[0, j, h:h + 1, :]
            if diagonal:
                s = jnp.where(visible, s, NEG)
            grp = [s[:, g * LANES:(g + 1) * LANES] for g in range(ngrp)]
            mc = grp[0]
            for g in grp[1:]:
                mc = jnp.maximum(mc, g)
            cqr = cqr_ref[h]
            m_old = m_ref[h]
            m_new = jnp.maximum(m_old, jnp.max(mc, axis=1, keepdims=True) + cqr)
            shift = m_new - cqr
            ps = [jnp.exp(g - shift) for g in grp]
            psum = ps[0]
            for p in ps[1:]:
                psum = psum + p
            alpha = jnp.exp(m_old - m_new)
            l_ref[h] = alpha * l_ref[h] + psum
            p16 = jnp.concatenate([p.astype(BF16) for p in ps], axis=1)
            acc_ref[h] = alpha * acc_ref[h] + _dot(p16, v16)
            m_ref[h] = m_new

    def body(j, carry):
        update(j, False)
        return carry

    lax.fori_loop(0, i, body, 0)
    update(i, True)
    for hp in range(N_HEADS // 2):
        l0 = jnp.sum(l_ref[2 * hp], axis=1, keepdims=True)
        l1 = jnp.sum(l_ref[2 * hp + 1], axis=1, keepdims=True)
        o = jnp.where(lane < HEAD_DIM, acc_ref[2 * hp] / l0, acc_ref[2 * hp + 1] / l1)
        o_ref[0, :, hp * LANES:(hp + 1) * LANES] = o.astype(o_ref.dtype)


def _fox_prompt(qkv16, cum, cum_t, blk):
    bsz, t, _ = qkv16.shape
    nb = t // blk
    ck = cum_t.reshape(bsz, N_HEADS, nb, blk).transpose(0, 2, 1, 3)
    return pl.pallas_call(
        functools.partial(_fox_prompt_kernel, blk=blk),
        grid=(bsz, nb),
        in_specs=[pl.BlockSpec((1, blk, A_WIDTH), lambda b, i: (b, i, 0)),
                  pl.BlockSpec((1, t, A_WIDTH), lambda b, i: (b, 0, 1)),
                  pl.BlockSpec((1, t, A_WIDTH), lambda b, i: (b, 0, 2)),
                  pl.BlockSpec((1, blk, LANES), lambda b, i: (b, i, 0)),
                  pl.BlockSpec((1, nb, N_HEADS, blk), lambda b, i: (b, 0, 0, 0))],
        out_specs=pl.BlockSpec((1, blk, A_WIDTH), lambda b, i: (b, i, 0)),
        out_shape=jax.ShapeDtypeStruct((bsz, t, A_WIDTH), BF16),
        scratch_shapes=[pltpu.VMEM((N_HEADS, blk, LANES), BF16),
                        pltpu.VMEM((N_HEADS, blk, LANES), F32),
                        pltpu.VMEM((N_HEADS, blk, LANES), F32),
                        pltpu.VMEM((N_HEADS, blk, LANES), F32),
                        pltpu.VMEM((N_HEADS, blk, LANES), F32)],
        compiler_params=_params(("arbitrary", "arbitrary")),
        name="fox_prompt",
    )(qkv16, qkv16, qkv16, cum, ck)


def _fox_sample_kernel(qkv_ref, cq_ref, ckn_ref, kc_ref, vc_ref, lf_ref, upper_ref, o_ref,
                       ckc_ref, *, t, plen, chunk):
    lane = lax.broadcasted_iota(jnp.int32, (t, LANES), 1)
    carry = jnp.zeros((N_HEADS, 1), F32)
    for c in reversed(range(plen // chunk)):
        lf = lf_ref[0, :, c * chunk:(c + 1) * chunk]
        after = _dot3(_split3(lf), upper_ref[...], left=False)
        ckc_ref[:, c * chunk:(c + 1) * chunk] = -(after + carry)
        carry = carry + jnp.sum(lf, axis=1, keepdims=True)
    rows = lax.broadcasted_iota(jnp.int32, (t, t), 0)
    cols = lax.broadcasted_iota(jnp.int32, (t, t), 1)
    visible = cols <= rows
    for hp in range(N_HEADS // 2):
        sl = slice(hp * LANES, (hp + 1) * LANES)
        q = qkv_ref[0, :, hp * LANES:(hp + 1) * LANES]
        kn16 = qkv_ref[0, :, A_WIDTH + hp * LANES:A_WIDTH + (hp + 1) * LANES]
        vn16 = qkv_ref[0, :, 2 * A_WIDTH + hp * LANES:2 * A_WIDTH + (hp + 1) * LANES]
        kc16 = kc_ref[0, :, sl].astype(BF16)
        vc16 = vc_ref[0, :, sl].astype(BF16)
        outs = []
        for hh in range(2):
            h = 2 * hp + hh
            keep = (lane < HEAD_DIM) if hh == 0 else (lane >= HEAD_DIM)
            qm = jnp.where(keep, q, jnp.zeros_like(q))
            cq = cq_ref[0, :, h:h + 1]
            s_new = jnp.where(visible, _dot_nt(qm, kn16) - ckn_ref[0, h:h + 1, :], NEG)
            m0 = jnp.full((t, 1), NEG, F32)
            z = jnp.zeros((t, 1), F32)
            m, l, acc = _softmax_step(s_new, cq, m0, z, jnp.zeros((t, LANES), F32), vn16)
            s_old = _dot_nt(qm, kc16) - ckc_ref[h:h + 1, :]
            m, l, acc = _softmax_step(s_old, cq, m, l, acc, vc16)
            outs.append(acc / l)
        o_ref[0, :, sl] = jnp.where(lane < HEAD_DIM, outs[0], outs[1]).astype(o_ref.dtype)


def _fox_sample(qkv16, cum, cum_t, kc, vc, lf_t, chunk=512):
    bsz, t, _ = qkv16.shape
    plen = kc.shape[1]
    r = jnp.arange(chunk)
    upper = (r[:, None] > r[None, :]).astype(BF16)
    per_b = lambda shape: pl.BlockSpec((1,) + shape, lambda b: (b, 0, 0))
    return pl.pallas_call(
        functools.partial(_fox_sample_kernel, t=t, plen=plen, chunk=chunk),
        grid=(bsz,),
        in_specs=[per_b((t, 3 * A_WIDTH)), per_b((t, LANES)), per_b((N_HEADS, t)),
                  per_b((plen, A_WIDTH)), per_b((plen, A_WIDTH)), per_b((N_HEADS, plen)),
                  _const_spec(upper.shape)],
        out_specs=per_b((t, A_WIDTH)),
        out_shape=jax.ShapeDtypeStruct((bsz, t, A_WIDTH), BF16),
        scratch_shapes=[pltpu.VMEM((N_HEADS, plen), F32)],
        compiler_params=_params(("arbitrary",)),
        name="fox_sample",
    )(qkv16, cum, cum_t, kc, vc, lf_t, upper)


def _even_tail_kernel(x_ref, attn_ref, xb_ref, gb_ref, cs_ref, h0_ref, cw_ref, cb_ref,
                      wa_ref, ba_ref, wx_ref, bx_ref, lam_ref, woa_ref, wob_ref, g_ref, b_ref,
                      y_ref, hl_ref, s_ref, ccarry_ref, hcarry_ref, *, tm, nseq, bps):
    i = pl.program_id(0)
    lb = tm // nseq
    xb = xb_ref[...]
    if bps > 1:
        first = (i % bps) == 0
        halos = [jnp.where(first, cs_ref[0], ccarry_ref[...])]
        hinit = jnp.where(first, h0_ref[0], hcarry_ref[...])
        ccarry_ref[...] = xb[tm - HALO:, :]
    else:
        halos = [cs_ref[s] for s in range(nseq)]
        hinit = jnp.concatenate(
            [jnp.broadcast_to(h0_ref[s], (lb, B_WIDTH)) for s in range(nseq)], axis=0)
    for s in range(nseq):
        base = s * (lb + HALO)
        s_ref[base:base + HALO, :] = halos[s]
        s_ref[base + HALO:base + HALO + lb, :] = xb[s * lb:(s + 1) * lb, :]
    pieces = []
    for s in range(nseq):
        base = s * (lb + HALO) + HALO
        xc = cb_ref[...] + cw_ref[RG_CONV - 1:RG_CONV, :] * xb[s * lb:(s + 1) * lb, :]
        for d in range(1, RG_CONV):
            xc = xc + cw_ref[RG_CONV - 1 - d:RG_CONV - d, :] * s_ref[base - d:base - d + lb, :]
        pieces.append(xc)
    xc = pieces[0] if nseq == 1 else jnp.concatenate(pieces, axis=0)
    xc16 = xc.astype(BF16)
    gate_r = jax.nn.sigmoid(_dot(xc16, wa_ref[...]) + ba_ref[...])
    gate_i = jax.nn.sigmoid(_dot(xc16, wx_ref[...]) + bx_ref[...])
    log_a = -RG_C * gate_r * _softplus(-lam_ref[...])
    a = jnp.exp(log_a)
    u = jnp.sqrt(jnp.tanh(-log_a) * (a * a + 1.0)) * (gate_i * xc)
    pos = lax.broadcasted_iota(jnp.int32, (tm, B_WIDTH), 0) % lb
    d = 1
    while d < lb:
        inside = pos >= d
        u = jnp.where(inside, a * pltpu.roll(u, d, axis=0) + u, u)
        a = jnp.where(inside, a * pltpu.roll(a, d, axis=0), a)
        d *= 2
    h = u + a * hinit
    if bps > 1:
        hcarry_ref[...] = h[tm - 1:tm, :]
    for s in range(nseq):
        hl_ref[0, s:s + 1, :] = h[(s + 1) * lb - 1:(s + 1) * lb, :]
    gated = (jax.nn.gelu(gb_ref[...]) * h).astype(BF16)
    y = _dot(attn_ref[...], woa_ref[...]) + _dot(gated, wob_ref[...])
    y_ref[...] = _layer_norm(ALPHA * x_ref[...] + y, g_ref[...], b_ref[...])


def _even_tail(x2d, attn16, xb, gb, conv_state, h0, seq_len, prm):
    n = x2d.shape[0]
    tm, nseq, bps = _plan(n, seq_len)
    nblk = n // tm
    cs = jnp.pad(conv_state, ((0, 0), (HALO - (RG_CONV - 1), 0), (0, 0)))
    h03 = h0.astype(F32)[:, None, :]
    row = lambda w: pl.BlockSpec((tm, w), lambda i: (i, 0))
    state = lambda r: pl.BlockSpec((nseq, r, B_WIDTH), lambda i: (i // bps, 0, 0))
    consts = [prm[k] for k in ("rg_cw", "rg_cb", "rg_wa", "rg_ba", "rg_wx", "rg_bx", "rg_lam",
                                "wo_a", "wo_b", "ln_g", "ln_b")]
    y, hl = pl.pallas_call(
        functools.partial(_even_tail_kernel, tm=tm, nseq=nseq, bps=bps),
        grid=(nblk,),
        in_specs=[row(D_MODEL), row(A_WIDTH), row(B_WIDTH), row(B_WIDTH), state(HALO), state(1)]
                 + [_const_spec(c.shape) for c in consts],
        out_specs=(row(D_MODEL), pl.BlockSpec((1, nseq, B_WIDTH), lambda i: (i, 0, 0))),
        out_shape=(jax.ShapeDtypeStruct((n, D_MODEL), F32),
                   jax.ShapeDtypeStruct((nblk, nseq, B_WIDTH), F32)),
        scratch_shapes=[pltpu.VMEM((nseq * (tm // nseq + HALO), B_WIDTH), F32),
                        pltpu.VMEM((HALO, B_WIDTH), F32),
                        pltpu.VMEM((1, B_WIDTH), F32)],
        compiler_params=_params(("arbitrary",)),
        name="even_tail",
    )(x2d, attn16, xb, gb, cs, h03, *consts)
    h_last = hl.reshape(n // seq_len, -1, B_WIDTH)[:, -1, :]
    return y, h_last


def _ffn_kernel(x_ref, st_ref, wg_ref, wu_ref, cwg_ref, cwu_ref, wd_ref, g_ref, b_ref,
                y_ref, so_ref, sg_ref, su_ref, a_ref, cg_ref, cu_ref, acc_ref,
                *, tm, nseq, bps, nchunk):
    i = pl.program_id(0)
    lb = tm // nseq
    x = x_ref[...]
    x16 = x.astype(BF16)
    acc_ref[...] = jnp.zeros_like(acc_ref)
    first = (i % bps) == 0

    def conv_branch(c, h, w_ref, carry_ref, s_ref, slot):
        taps = w_ref[c]
        for s in range(nseq):
            base = s * (lb + HALO)
            if bps > 1:
                halo = jnp.where(first, st_ref[0, slot], carry_ref[c])
            else:
                halo = st_ref[s, slot]
            s_ref[base:base + HALO, :] = halo
            s_ref[base + HALO:base + HALO + lb, :] = h[s * lb:(s + 1) * lb, :]
            so_ref[s, slot] = h[(s + 1) * lb - HALO:(s + 1) * lb, :]
        if bps > 1:
            carry_ref[c] = h[tm - HALO:, :]
        outs = []
        for s in range(nseq):
            base = s * (lb + HALO) + HALO
            y = taps[3:4, :] + taps[2:3, :] * h[s * lb:(s + 1) * lb, :]
            y = y + taps[1:2, :] * s_ref[base - 1:base - 1 + lb, :]
            y = y + taps[0:1, :] * s_ref[base - 2:base - 2 + lb, :]
            outs.append(y)
        return outs

    def chunk(c, carry):
        hg = _dot(x16, wg_ref[c])
        hu = _dot(x16, wu_ref[c])
        yg = conv_branch(c, hg, cwg_ref, cg_ref, sg_ref, c)
        yu = conv_branch(c, hu, cwu_ref, cu_ref, su_ref, nchunk + c)
        for s in range(nseq):
            a_ref[s * lb:(s + 1) * lb, :] = (jax.nn.gelu(yg[s]) * yu[s]).astype(BF16)
        acc_ref[...] += _dot(a_ref[...], wd_ref[c])
        return carry

    lax.fori_loop(0, nchunk, chunk, 0, unroll=True)
    y_ref[...] = _layer_norm(ALPHA * x + acc_ref[...], g_ref[...], b_ref[...])


def _ffn(x2d, state, seq_len, prm):
    n = x2d.shape[0]
    tm, nseq, bps = _plan(n, seq_len)
    nblk = n // tm
    tn = FFN_CHUNK
    nchunk = D_FF // tn
    bsz = state.shape[0]
    st = jnp.pad(state, ((0, 0), (HALO - (FFN_CONV - 1), 0), (0, 0)))
    st = st.reshape(bsz, HALO, 2 * nchunk, tn).transpose(0, 2, 1, 3)
    row = pl.BlockSpec((tm, D_MODEL), lambda i: (i, 0))
    consts = [prm[k] for k in ("w_g", "w_u", "cw_g", "cw_u", "w_d", "ln_g", "ln_b")]
    y, so = pl.pallas_call(
        functools.partial(_ffn_kernel, tm=tm, nseq=nseq, bps=bps, nchunk=nchunk),
        grid=(nblk,),
        in_specs=[row, pl.BlockSpec((nseq, 2 * nchunk, HALO, tn), lambda i: (i // bps, 0, 0, 0))]
                 + [_const_spec(c.shape) for c in consts],
        out_specs=(row, pl.BlockSpec((nseq, 2 * nchunk, HALO, tn), lambda i: (i, 0, 0, 0))),
        out_shape=(jax.ShapeDtypeStruct((n, D_MODEL), F32),
                   jax.ShapeDtypeStruct((nblk * nseq, 2 * nchunk, HALO, tn), F32)),
        scratch_shapes=[pltpu.VMEM((nseq * (tm // nseq + HALO), tn), F32),
                        pltpu.VMEM((nseq * (tm // nseq + HALO), tn), F32),
                        pltpu.VMEM((tm, tn), BF16),
                        pltpu.VMEM((nchunk, HALO, tn), F32),
                        pltpu.VMEM((nchunk, HALO, tn), F32),
                        pltpu.VMEM((tm, D_MODEL), F32)],
        compiler_params=_params(("arbitrary",)),
        name="conv_ffn",
    )(x2d, st, *consts)
    so = so.reshape(bsz, -1, 2 * nchunk, HALO, tn)[:, -1]
    new_state = so[:, :, HALO - (FFN_CONV - 1):, :].transpose(0, 2, 1, 3).reshape(bsz, FFN_CONV - 1, 2 * D_FF)
    return y, new_state


def _odd_kernel(x_ref, wu_ref, wv_ref, sg_ref, sb_ref, ws_ref, bias_ref, wo_ref, g_ref, b_ref,
                y_ref, *rest, tm, lc, emit_v):
    gate_ref = rest[-1]
    x = x_ref[...]
    x16 = x.astype(BF16)
    u = jax.nn.gelu(_dot(x16, wu_ref[...]))
    v = _layer_norm(jax.nn.gelu(_dot(x16, wv_ref[...])), sg_ref[...], sb_ref[...])
    if emit_v:
        rest[0][...] = v
    v16 = v.astype(BF16)
    gw = C_WIDTH // C_GROUPS
    for c in range(tm // lc):
        rows = slice(c * lc, (c + 1) * lc)
        for g in range(C_GROUPS):
            cols = slice(g * gw, (g + 1) * gw)
            mixed = _dot(ws_ref[g], v16[rows, cols]) + bias_ref[:, cols]
            gate_ref[rows, cols] = (u[rows, cols] * mixed).astype(BF16)
    y = _dot(gate_ref[...], wo_ref[...])
    y_ref[...] = _layer_norm(ALPHA * x + y, g_ref[...], b_ref[...])


def _odd(x2d, seq_len, prm, emit_v):
    n = x2d.shape[0]
    tm = min(ROW_BLOCK, n)
    lc = min(seq_len, C_LEN)
    gw = C_WIDTH // C_GROUPS
    ws16 = (prm["sgu_w"][:, :lc, :lc] * jnp.tril(jnp.ones((lc, lc), F32))).astype(BF16)
    bias = jnp.repeat(prm["sgu_bias"][:, :lc].T, gw, axis=1)
    consts = [prm["w_u"], prm["w_v"], prm["sgu_g"], prm["sgu_b"], ws16, bias,
              prm["w_o"], prm["ln_g"], prm["ln_b"]]
    row = pl.BlockSpec((tm, D_MODEL), lambda i: (i, 0))
    n_out = 2 if emit_v else 1
    return pl.pallas_call(
        functools.partial(_odd_kernel, tm=tm, lc=lc, emit_v=emit_v),
        grid=(n // tm,),
        in_specs=[row] + [_const_spec(c.shape) for c in consts],
        out_specs=(row, pl.BlockSpec((tm, C_WIDTH), lambda i: (i, 0)))[:n_out],
        out_shape=(jax.ShapeDtypeStruct((n, D_MODEL), F32),
                   jax.ShapeDtypeStruct((n, C_WIDTH), F32))[:n_out],
        scratch_shapes=[pltpu.VMEM((tm, C_WIDTH), BF16)],
        compiler_params=_params(("arbitrary",)),
        name="odd_mixer",
    )(x2d, *consts)


def _row(v):
    return v.astype(F32)[None, :]


def _block_diag(w):
    nb, bd, _ = w.shape
    eye = jnp.eye(nb, dtype=w.dtype)
    return jnp.einsum("ncd,nm->ncmd", w, eye).reshape(nb * bd, nb * bd)


def _even_params(w_in, b_f, rg_conv_w, rg_conv_b, rg_wa, rg_ba, rg_wx, rg_bx, rg_lam, w_out, ln_g, ln_b):
    nf = 3 * A_WIDTH
    wf = jnp.pad(w_in[:, nf:nf + N_HEADS], ((0, 0), (0, LANES - N_HEADS)))
    return {
        "wqkv": w_in[:, :nf].astype(BF16),
        "wf": wf.astype(BF16),
        "wxg": w_in[:, nf + N_HEADS:].astype(BF16),
        "bf": jnp.pad(b_f, (0, LANES - N_HEADS))[None, :],
        "rg_cw": rg_conv_w, "rg_cb": _row(rg_conv_b),
        "rg_wa": _block_diag(rg_wa).astype(BF16), "rg_ba": _row(rg_ba),
        "rg_wx": _block_diag(rg_wx).astype(BF16), "rg_bx": _row(rg_bx),
        "rg_lam": _row(rg_lam),
        "wo_a": w_out[:A_WIDTH].astype(BF16), "wo_b": w_out[A_WIDTH:].astype(BF16),
        "ln_g": _row(ln_g), "ln_b": _row(ln_b),
    }


def _ffn_params(w_up, conv_w, conv_b, w_down, ln_g, ln_b):
    tn = FFN_CHUNK
    nchunk = D_FF // tn
    chunked = lambda w: w.reshape(D_MODEL, nchunk, tn).transpose(1, 0, 2).astype(BF16)
    taps = jnp.concatenate([conv_w, conv_b[None, :],
                            jnp.zeros((HALO - FFN_CONV - 1, 2 * D_FF), F32)], axis=0)
    taps_c = lambda w: w.reshape(HALO, nchunk, tn).transpose(1, 0, 2)
    return {
        "w_g": chunked(w_up[:, :D_FF]), "w_u": chunked(w_up[:, D_FF:]),
        "cw_g": taps_c(taps[:, :D_FF]), "cw_u": taps_c(taps[:, D_FF:]),
        "w_d": w_down.reshape(nchunk, tn, D_MODEL).astype(BF16),
        "ln_g": _row(ln_g), "ln_b": _row(ln_b),
    }


def _odd_params(w_in, sgu_g, sgu_b, sgu_w, sgu_bias, w_out, ln_g, ln_b):
    return {
        "w_u": w_in[:, :C_WIDTH].astype(BF16), "w_v": w_in[:, C_WIDTH:].astype(BF16),
        "sgu_g": _row(sgu_g), "sgu_b": _row(sgu_b), "sgu_w": sgu_w, "sgu_bias": sgu_bias,
        "w_o": w_out.astype(BF16), "ln_g": _row(ln_g), "ln_b": _row(ln_b),
    }


def _even_layer(x3d, past, prm):
    bsz, t, _ = x3d.shape
    x2d = x3d.reshape(bsz * t, D_MODEL)
    qkv16, k, v, logf, cum, xb, gb = _even_in(x2d, t, prm["wqkv"], prm["wf"], prm["wxg"], prm["bf"])
    qkv16 = qkv16.reshape(bsz, t, 3 * A_WIDTH)
    cum3 = cum.reshape(bsz, t, LANES)
    cum_t = cum3[:, :, :N_HEADS].transpose(0, 2, 1)
    if past is None:
        attn = _fox_prompt(qkv16, cum3, cum_t, ROW_BLOCK)
        conv_s = jnp.zeros((bsz, RG_CONV - 1, B_WIDTH), F32)
        h0 = jnp.zeros((bsz, B_WIDTH), F32)
    else:
        k_c, v_c, lf_c, conv_s, h0 = past
        plen = k_c.shape[1]
        attn = _fox_sample(qkv16, cum3, cum_t, k_c.reshape(bsz, plen, A_WIDTH),
                           v_c.reshape(bsz, plen, A_WIDTH), lf_c.transpose(0, 2, 1))
    y, h_last = _even_tail(x2d, attn.reshape(bsz * t, A_WIDTH), xb, gb, conv_s, h0, t, prm)
    new_k = k.reshape(bsz, t, N_HEADS, HEAD_DIM)
    new_v = v.reshape(bsz, t, N_HEADS, HEAD_DIM)
    new_logf = logf.reshape(bsz, t, LANES)[:, :, :N_HEADS]
    xb3 = xb.reshape(bsz, t, B_WIDTH)
    conv_new = jnp.concatenate([conv_s, xb3], axis=1)[:, -(RG_CONV - 1):] if t < RG_CONV - 1 \
        else xb3[:, -(RG_CONV - 1):]
    return y.reshape(bsz, t, D_MODEL), (new_k, new_v, new_logf, conv_new, h_last)


def _ffn_layer(x3d, state, prm):
    bsz, t, _ = x3d.shape
    y, new_state = _ffn(x3d.reshape(bsz * t, D_MODEL), state, t, prm)
    return y.reshape(bsz, t, D_MODEL), new_state


def _odd_layer(x3d, prm, emit_v):
    bsz, t, _ = x3d.shape
    outs = _odd(x3d.reshape(bsz * t, D_MODEL), t, prm, emit_v)
    v = outs[1].reshape(bsz, t, C_WIDTH) if emit_v else None
    return outs[0].reshape(bsz, t, D_MODEL), v


def kernel(x_prompt, x_sample, cache_k, cache_v, cache_logf, state_rglru_conv, state_rglru_h, state_ffn_conv, w_in_e, b_f, rg_conv_w, rg_conv_b, rg_wa, rg_ba, rg_wx, rg_bx, rg_lam, w_out_e, w_in_o, sgu_g, sgu_b, sgu_w, sgu_bias, w_out_o, ln_mix_g, ln_mix_b, ln_ffn_g, ln_ffn_b, ffn_w_up, ffn_conv_w, ffn_conv_b, ffn_w_down):
    xp, xs = x_prompt, x_sample
    even_p, even_s, ffn_p, ffn_s, sgu_v = [], [], [], [], []
    for layer in range(DEPTH):
        if layer % 2 == 0:
            e = layer // 2
            prm = _even_params(w_in_e[e], b_f[e], rg_conv_w[e], rg_conv_b[e], rg_wa[e], rg_ba[e],
                               rg_wx[e], rg_bx[e], rg_lam[e], w_out_e[e],
                               ln_mix_g[layer], ln_mix_b[layer])
            xp, st_p = _even_layer(xp, None, prm)
            past = (cache_k[e], cache_v[e], cache_logf[e], state_rglru_conv[e], state_rglru_h[e])
            xs, st_s = _even_layer(xs, past, prm)
            even_p.append(st_p)
            even_s.append(st_s)
        else:
            o = layer // 2
            prm = _odd_params(w_in_o[o], sgu_g[o], sgu_b[o], sgu_w[o], sgu_bias[o], w_out_o[o],
                              ln_mix_g[layer], ln_mix_b[layer])
            xp, _ = _odd_layer(xp, prm, emit_v=False)
            xs, sv = _odd_layer(xs, prm, emit_v=True)
            sgu_v.append(sv)
        fprm = _ffn_params(ffn_w_up[layer], ffn_conv_w[layer], ffn_conv_b[layer], ffn_w_down[layer],
                           ln_ffn_g[layer], ln_ffn_b[layer])
        zero_fc = jnp.zeros((xp.shape[0], FFN_CONV - 1, 2 * D_FF), F32)
        xp, fcp = _ffn_layer(xp, zero_fc, fprm)
        xs, fcs = _ffn_layer(xs, state_ffn_conv[layer], fprm)
        ffn_p.append(fcp)
        ffn_s.append(fcs)
    stack = lambda lst, idx: jnp.stack([item[idx] for item in lst])
    return (xp, xs,
            stack(even_p, 0), stack(even_p, 1), stack(even_p, 2),
            stack(even_s, 0), stack(even_s, 1), stack(even_s, 2),
            stack(even_p, 3), stack(even_p, 4), stack(even_s, 3), stack(even_s, 4),
            jnp.stack(ffn_p), jnp.stack(ffn_s), jnp.stack(sgu_v))
```

```python
import functools

import jax
import jax.numpy as jnp
from jax import lax
from jax.experimental import pallas as pl
from jax.experimental.pallas import tpu as pltpu

F32 = jnp.float32
BF16 = jnp.bfloat16

D_MODEL = 1024
N_HEADS = 8
HEAD_DIM = 64
A_WIDTH = N_HEADS * HEAD_DIM
B_WIDTH = 512
B_BLOCKS = 8
RG_CONV = 4
RG_C = 8.0
C_WIDTH = 1024
C_GROUPS = 8
C_LEN = 128
D_FF = 2816
FFN_CONV = 3
DEPTH = 2
ALPHA = (2 * DEPTH) ** 0.25
LN_EPS = 1e-5

LANES = 128
SUBLANES = 8
HALO = SUBLANES
NEG = -1e30
ROW_BLOCK = 512
FFN_CHUNK = 256
VMEM_LIMIT = 48 * 1024 * 1024


def _layer_norm(x, g, b):
    mu = jnp.mean(x, axis=-1, keepdims=True)
    xc = x - mu
    var = jnp.mean(xc * xc, axis=-1, keepdims=True)
    return xc * lax.rsqrt(var + LN_EPS) * g + b


def _log_sigmoid(x):
    return jnp.minimum(x, 0.0) - jnp.log1p(jnp.exp(-jnp.abs(x)))


def _softplus(x):
    return jnp.maximum(x, 0.0) + jnp.log1p(jnp.exp(-jnp.abs(x)))


def _split3(x):
    hi = x.astype(BF16)
    r1 = x - hi.astype(F32)
    mid = r1.astype(BF16)
    lo = (r1 - mid.astype(F32)).astype(BF16)
    return hi, mid, lo


def _dot(a, b):
    return jnp.dot(a, b, preferred_element_type=F32)


def _dot_nt(a, b):
    return lax.dot_general(a, b, (((1,), (1,)), ((), ())), preferred_element_type=F32)


def _dot3(pieces, ones_matrix, left):
    out = None
    for p in pieces:
        t = _dot(ones_matrix, p) if left else _dot(p, ones_matrix)
        out = t if out is None else out + t
    return out


def _const_spec(shape):
    nd = len(shape)
    return pl.BlockSpec(shape, lambda *_: (0,) * nd, pipeline_mode=pl.Buffered(1))


def _params(sem):
    return pltpu.CompilerParams(dimension_semantics=sem, vmem_limit_bytes=VMEM_LIMIT)


def _plan(n_rows, seq_len):
    tm = min(ROW_BLOCK, n_rows)
    nseq = max(1, tm // seq_len)
    bps = max(1, seq_len // tm)
    assert n_rows % tm == 0 and (seq_len % tm == 0 or tm % seq_len == 0)
    return tm, nseq, bps


def _even_in_kernel(x_ref, wqkv_ref, wf_ref, wxg_ref, bf_ref, tri_ref,
                    qkv16_ref, k_ref, v_ref, logf_ref, cum_ref, xb_ref, gb_ref,
                    carry_ref, *, tm, bps):
    i = pl.program_id(0)
    x16 = x_ref[...].astype(BF16)
    qkv = _dot(x16, wqkv_ref[...])
    qkv16_ref[:, :A_WIDTH] = (qkv[:, :A_WIDTH] * (HEAD_DIM ** -0.5)).astype(BF16)
    qkv16_ref[:, A_WIDTH:] = qkv[:, A_WIDTH:].astype(BF16)
    k_ref[...] = qkv[:, A_WIDTH:2 * A_WIDTH]
    v_ref[...] = qkv[:, 2 * A_WIDTH:]
    logf = _log_sigmoid(_dot(x16, wf_ref[...]) + bf_ref[...])
    logf_ref[...] = logf
    cs = _dot3(_split3(logf), tri_ref[...], left=True)
    if bps > 1:
        @pl.when(i % bps == 0)
        def _():
            carry_ref[...] = jnp.zeros_like(carry_ref)
        cs = cs + carry_ref[...]
        carry_ref[...] = cs[tm - 1:tm, :]
    cum_ref[...] = cs
    xg = _dot(x16, wxg_ref[...])
    xb_ref[...] = xg[:, :B_WIDTH]
    gb_ref[...] = xg[:, B_WIDTH:]


def _even_in(x2d, seq_len, wqkv16, wf16, wxg16, bf_pad):
    n = x2d.shape[0]
    tm, nseq, bps = _plan(n, seq_len)
    r = jnp.arange(tm)
    tri = ((r[None, :] <= r[:, None]) & ((r[None, :] // seq_len) == (r[:, None] // seq_len))).astype(BF16)
    row = lambda w: pl.BlockSpec((tm, w), lambda i: (i, 0))
    out_shapes = (
        jax.ShapeDtypeStruct((n, 3 * A_WIDTH), BF16),
        jax.ShapeDtypeStruct((n, A_WIDTH), F32),
        jax.ShapeDtypeStruct((n, A_WIDTH), F32),
        jax.ShapeDtypeStruct((n, LANES), F32),
        jax.ShapeDtypeStruct((n, LANES), F32),
        jax.ShapeDtypeStruct((n, B_WIDTH), F32),
        jax.ShapeDtypeStruct((n, B_WIDTH), F32),
    )
    return pl.pallas_call(
        functools.partial(_even_in_kernel, tm=tm, bps=bps),
        grid=(n // tm,),
        in_specs=[row(D_MODEL), _const_spec(wqkv16.shape), _const_spec(wf16.shape),
                  _const_spec(wxg16.shape), _const_spec(bf_pad.shape), _const_spec(tri.shape)],
        out_specs=(row(3 * A_WIDTH), row(A_WIDTH), row(A_WIDTH), row(LANES), row(LANES),
                   row(B_WIDTH), row(B_WIDTH)),
        out_shape=out_shapes,
        scratch_shapes=[pltpu.VMEM((1, LANES), F32)],
        compiler_params=_params(("arbitrary",)),
        name="even_in_proj",
    )(x2d, wqkv16, wf16, wxg16, bf_pad, tri)


def _softmax_step(s, cq, m_old, l_old, acc_old, v16):
    m_new = jnp.maximum(m_old, jnp.max(s, axis=1, keepdims=True) + cq)
    p = jnp.exp(s - (m_new - cq))
    alpha = jnp.exp(m_old - m_new)
    l_new = alpha * l_old + jnp.sum(p, axis=1, keepdims=True)
    acc_new = alpha * acc_old + _dot(p.astype(BF16), v16)
    return m_new, l_new, acc_new


def _fox_prompt_kernel(q_ref, k_ref, v_ref, cq_ref, ck_ref, o_ref,
                       qm_ref, cqr_ref, m_ref, l_ref, acc_ref, *, blk):
    i = pl.program_id(1)
    lane = lax.broadcasted_iota(jnp.int32, (blk, LANES), 1)
    ngrp = blk // LANES

    for h in range(N_HEADS):
        hp, hh = divmod(h, 2)
        q = q_ref[0, :, hp * LANES:(hp + 1) * LANES]
        keep = (lane < HEAD_DIM) if hh == 0 else (lane >= HEAD_DIM)
        qm_ref[h] = jnp.where(keep, q, jnp.zeros_like(q))
        cqr_ref[h] = jnp.broadcast_to(cq_ref[0, :, h:h + 1], (blk, LANES))
    m_ref[...] = jnp.full(m_ref.shape, NEG, F32)
    l_ref[...] = jnp.zeros_like(l_ref)
    acc_ref[...] = jnp.zeros_like(acc_ref)

    def update(kg0, ngrp, diagonal):
        koff = pl.multiple_of(kg0 * LANES, LANES)
        if diagonal:
            rows = lax.broadcasted_iota(jnp.int32, (blk, LANES), 0)
            visible = [lane + g * LANES <= rows for g in range(ngrp)]
        for h in range(N_HEADS):
            hp = h // 2
            k16 = k_ref[0, pl.ds(koff, ngrp * LANES), hp * LANES:(hp + 1) * LANES]
            v16 = v_ref[0, pl.ds(koff, ngrp * LANES), hp * LANES:(hp + 1) * LANES]
            s = _dot_nt(qm_ref[h], k16)
            grp = [s[:, g * LANES:(g + 1) * LANES] - ck_ref[0, kg0 + g, h:h + 1, :] for g in range(ngrp)]
            if diagonal:
                grp = [jnp.where(vis, g, NEG) for vis, g in zip(visible, grp)]
            mc = grp[0]
            for g in grp[1:]:
                mc = jnp.maximum(mc, g)
            cqr = cqr_ref[h]
            m_old = m_ref[h]
            m_new = jnp.maximum(m_old, jnp.max(mc, axis=1, keepdims=True) + cqr)
            shift = m_new - cqr
            ps = [jnp.exp(g - shift) for g in grp]
            psum = ps[0]
            for p in ps[1:]:
                psum = psum + p
            alpha = jnp.exp(m_old - m_new)
            l_ref[h] = alpha * l_ref[h] + psum
            p16 = jnp.concatenate([p.astype(BF16) for p in ps], axis=1)
            acc_ref[h] = alpha * acc_ref[h] + _dot(p16, v16)
            m_ref[h] = m_new

    def body(j, carry):
        update(j * (2 * ngrp), 2 * ngrp, False)
        return carry

    lax.fori_loop(0, i // 2, body, 0)

    @pl.when(i % 2 == 1)
    def _():
        update((i - 1) * ngrp, ngrp, False)

    update(i * ngrp, ngrp, True)
    for hp in range(N_HEADS // 2):
        l0 = jnp.sum(l_ref[2 * hp], axis=1, keepdims=True)
        l1 = jnp.sum(l_ref[2 * hp + 1], axis=1, keepdims=True)
        o = jnp.where(lane < HEAD_DIM, acc_ref[2 * hp] / l0, acc_ref[2 * hp + 1] / l1)
        o_ref[0, :, hp * LANES:(hp + 1) * LANES] = o.astype(o_ref.dtype)


def _fox_prompt(qkv16, cum, cum_t, blk):
    bsz, t, _ = qkv16.shape
    nb = t // blk
    ng = t // LANES
    ck = cum_t.reshape(bsz, N_HEADS, ng, LANES).transpose(0, 2, 1, 3)
    return pl.pallas_call(
        functools.partial(_fox_prompt_kernel, blk=blk),
        grid=(bsz, nb),
        in_specs=[pl.BlockSpec((1, blk, A_WIDTH), lambda b, i: (b, i, 0)),
                  pl.BlockSpec((1, t, A_WIDTH), lambda b, i: (b, 0, 1)),
                  pl.BlockSpec((1, t, A_WIDTH), lambda b, i: (b, 0, 2)),
                  pl.BlockSpec((1, blk, LANES), lambda b, i: (b, i, 0)),
                  pl.BlockSpec((1, ng, N_HEADS, LANES), lambda b, i: (b, 0, 0, 0))],
        out_specs=pl.BlockSpec((1, blk, A_WIDTH), lambda b, i: (b, i, 0)),
        out_shape=jax.ShapeDtypeStruct((bsz, t, A_WIDTH), BF16),
        scratch_shapes=[pltpu.VMEM((N_HEADS, blk, LANES), BF16),
                        pltpu.VMEM((N_HEADS, blk, LANES), F32),
                        pltpu.VMEM((N_HEADS, blk, LANES), F32),
                        pltpu.VMEM((N_HEADS, blk, LANES), F32),
                        pltpu.VMEM((N_HEADS, blk, LANES), F32)],
        compiler_params=_params(("arbitrary", "arbitrary")),
        name="fox_prompt",
    )(qkv16, qkv16, qkv16, cum, ck)


def _fox_sample_kernel(qkv_ref, cq_ref, ckn_ref, kc_ref, vc_ref, lf_ref, upper_ref, o_ref,
                       ckc_ref, *, t, plen, chunk):
    lane = lax.broadcasted_iota(jnp.int32, (t, LANES), 1)
    carry = jnp.zeros((N_HEADS, 1), F32)
    for c in reversed(range(plen // chunk)):
        lf = lf_ref[0, :, c * chunk:(c + 1) * chunk]
        after = _dot3(_split3(lf), upper_ref[...], left=False)
        ckc_ref[:, c * chunk:(c + 1) * chunk] = -(after + carry)
        carry = carry + jnp.sum(lf, axis=1, keepdims=True)
    rows = lax.broadcasted_iota(jnp.int32, (t, t), 0)
    cols = lax.broadcasted_iota(jnp.int32, (t, t), 1)
    visible = cols <= rows
    for hp in range(N_HEADS // 2):
        sl = slice(hp * LANES, (hp + 1) * LANES)
        q = qkv_ref[0, :, hp * LANES:(hp + 1) * LANES]
        kn16 = qkv_ref[0, :, A_WIDTH + hp * LANES:A_WIDTH + (hp + 1) * LANES]
        vn16 = qkv_ref[0, :, 2 * A_WIDTH + hp * LANES:2 * A_WIDTH + (hp + 1) * LANES]
        kc16 = kc_ref[0, :, sl].astype(BF16)
        vc16 = vc_ref[0, :, sl].astype(BF16)
        outs = []
        for hh in range(2):
            h = 2 * hp + hh
            keep = (lane < HEAD_DIM) if hh == 0 else (lane >= HEAD_DIM)
            qm = jnp.where(keep, q, jnp.zeros_like(q))
            cq = cq_ref[0, :, h:h + 1]
            s_new = jnp.where(visible, _dot_nt(qm, kn16) - ckn_ref[0, h:h + 1, :], NEG)
            m0 = jnp.full((t, 1), NEG, F32)
            z = jnp.zeros((t, 1), F32)
            m, l, acc = _softmax_step(s_new, cq, m0, z, jnp.zeros((t, LANES), F32), vn16)
            s_old = _dot_nt(qm, kc16) - ckc_ref[h:h + 1, :]
            m, l, acc = _softmax_step(s_old, cq, m, l, acc, vc16)
            outs.append(acc / l)
        o_ref[0, :, sl] = jnp.where(lane < HEAD_DIM, outs[0], outs[1]).astype(o_ref.dtype)


def _fox_sample(qkv16, cum, cum_t, kc, vc, lf_t, chunk=512):
    bsz, t, _ = qkv16.shape
    plen = kc.shape[1]
    r = jnp.arange(chunk)
    upper = (r[:, None] > r[None, :]).astype(BF16)
    per_b = lambda shape: pl.BlockSpec((1,) + shape, lambda b: (b, 0, 0))
    return pl.pallas_call(
        functools.partial(_fox_sample_kernel, t=t, plen=plen, chunk=chunk),
        grid=(bsz,),
        in_specs=[per_b((t, 3 * A_WIDTH)), per_b((t, LANES)), per_b((N_HEADS, t)),
                  per_b((plen, A_WIDTH)), per_b((plen, A_WIDTH)), per_b((N_HEADS, plen)),
                  _const_spec(upper.shape)],
        out_specs=per_b((t, A_WIDTH)),
        out_shape=jax.ShapeDtypeStruct((bsz, t, A_WIDTH), BF16),
        scratch_shapes=[pltpu.VMEM((N_HEADS, plen), F32)],
        compiler_params=_params(("arbitrary",)),
        name="fox_sample",
    )(qkv16, cum, cum_t, kc, vc, lf_t, upper)


def _even_tail_kernel(x_ref, attn_ref, xb_ref, gb_ref, cs_ref, h0_ref, cw_ref, cb_ref,
                      wa_ref, ba_ref, wx_ref, bx_ref, lam_ref, woa_ref, wob_ref, g_ref, b_ref,
                      y_ref, hl_ref, s_ref, ccarry_ref, hcarry_ref, *, tm, nseq, bps):
    i = pl.program_id(0)
    lb = tm // nseq
    xb = xb_ref[...]
    if bps > 1:
        first = (i % bps) == 0
        halos = [jnp.where(first, cs_ref[0], ccarry_ref[...])]
        hinit = jnp.where(first, h0_ref[0], hcarry_ref[...])
        ccarry_ref[...] = xb[tm - HALO:, :]
    else:
        halos = [cs_ref[s] for s in range(nseq)]
        hinit = jnp.concatenate(
            [jnp.broadcast_to(h0_ref[s], (lb, B_WIDTH)) for s in range(nseq)], axis=0)
    for s in range(nseq):
        base = s * (lb + HALO)
        s_ref[base:base + HALO, :] = halos[s]
        s_ref[base + HALO:base + HALO + lb, :] = xb[s * lb:(s + 1) * lb, :]
    pieces = []
    for s in range(nseq):
        base = s * (lb + HALO) + HALO
        xc = cb_ref[...] + cw_ref[RG_CONV - 1:RG_CONV, :] * xb[s * lb:(s + 1) * lb, :]
        for d in range(1, RG_CONV):
            xc = xc + cw_ref[RG_CONV - 1 - d:RG_CONV - d, :] * s_ref[base - d:base - d + lb, :]
        pieces.append(xc)
    xc = pieces[0] if nseq == 1 else jnp.concatenate(pieces, axis=0)
    xc16 = xc.astype(BF16)
    gate_r = jax.nn.sigmoid(_dot(xc16, wa_ref[...]) + ba_ref[...])
    gate_i = jax.nn.sigmoid(_dot(xc16, wx_ref[...]) + bx_ref[...])
    log_a = -RG_C * gate_r * _softplus(-lam_ref[...])
    a = jnp.exp(log_a)
    u = jnp.sqrt(jnp.tanh(-log_a) * (a * a + 1.0)) * (gate_i * xc)
    pos = lax.broadcasted_iota(jnp.int32, (tm, B_WIDTH), 0) % lb
    d = 1
    while d < lb:
        inside = pos >= d
        u = jnp.where(inside, a * pltpu.roll(u, d, axis=0) + u, u)
        a = jnp.where(inside, a * pltpu.roll(a, d, axis=0), a)
        d *= 2
    h = u + a * hinit
    if bps > 1:
        hcarry_ref[...] = h[tm - 1:tm, :]
    for s in range(nseq):
        hl_ref[0, s:s + 1, :] = h[(s + 1) * lb - 1:(s + 1) * lb, :]
    gated = (jax.nn.gelu(gb_ref[...]) * h).astype(BF16)
    y = _dot(attn_ref[...], woa_ref[...]) + _dot(gated, wob_ref[...])
    y_ref[...] = _layer_norm(ALPHA * x_ref[...] + y, g_ref[...], b_ref[...])


def _even_tail(x2d, attn16, xb, gb, conv_state, h0, seq_len, prm):
    n = x2d.shape[0]
    tm, nseq, bps = _plan(n, seq_len)
    nblk = n // tm
    cs = jnp.pad(conv_state, ((0, 0), (HALO - (RG_CONV - 1), 0), (0, 0)))
    h03 = h0.astype(F32)[:, None, :]
    row = lambda w: pl.BlockSpec((tm, w), lambda i: (i, 0))
    state = lambda r: pl.BlockSpec((nseq, r, B_WIDTH), lambda i: (i // bps, 0, 0))
    consts = [prm[k] for k in ("rg_cw", "rg_cb", "rg_wa", "rg_ba", "rg_wx", "rg_bx", "rg_lam",
                                "wo_a", "wo_b", "ln_g", "ln_b")]
    y, hl = pl.pallas_call(
        functools.partial(_even_tail_kernel, tm=tm, nseq=nseq, bps=bps),
        grid=(nblk,),
        in_specs=[row(D_MODEL), row(A_WIDTH), row(B_WIDTH), row(B_WIDTH), state(HALO), state(1)]
                 + [_const_spec(c.shape) for c in consts],
        out_specs=(row(D_MODEL), pl.BlockSpec((1, nseq, B_WIDTH), lambda i: (i, 0, 0))),
        out_shape=(jax.ShapeDtypeStruct((n, D_MODEL), F32),
                   jax.ShapeDtypeStruct((nblk, nseq, B_WIDTH), F32)),
        scratch_shapes=[pltpu.VMEM((nseq * (tm // nseq + HALO), B_WIDTH), F32),
                        pltpu.VMEM((HALO, B_WIDTH), F32),
                        pltpu.VMEM((1, B_WIDTH), F32)],
        compiler_params=_params(("arbitrary",)),
        name="even_tail",
    )(x2d, attn16, xb, gb, cs, h03, *consts)
    h_last = hl.reshape(n // seq_len, -1, B_WIDTH)[:, -1, :]
    return y, h_last


def _shift_rows(h, halo, d):
    rolled = pltpu.roll(h, d, axis=0)
    rowid = lax.broadcasted_iota(jnp.int32, halo.shape, 0)
    head = jnp.where(rowid < d, pltpu.roll(halo, d, axis=0), rolled[:HALO, :])
    return jnp.concatenate([head, rolled[HALO:, :]], axis=0)


def _ffn_kernel(x_ref, st_ref, wg_ref, wu_ref, cwg_ref, cwu_ref, wd_ref, g_ref, b_ref,
                y_ref, so_ref, a_ref, cg_ref, cu_ref, *, tm, nseq, bps, nchunk, tn):
    i = pl.program_id(0)
    lb = tm // nseq
    x = x_ref[...]
    x16 = x.astype(BF16)
    first = (i % bps) == 0

    def conv_branch(c, h, w_ref, carry_ref, slot):
        taps = w_ref[c]
        outs = []
        for s in range(nseq):
            hs = h[s * lb:(s + 1) * lb, :]
            if bps > 1:
                halo = jnp.where(first, st_ref[0, slot], carry_ref[c])
            else:
                halo = st_ref[s, slot]
            so_ref[s, slot] = hs[lb - HALO:, :]
            y = taps[3:4, :] + taps[2:3, :] * hs
            y = y + taps[1:2, :] * _shift_rows(hs, halo, 1)
            y = y + taps[0:1, :] * _shift_rows(hs, halo, 2)
            outs.append(y)
        if bps > 1:
            carry_ref[c] = h[tm - HALO:, :]
        return outs

    for c in range(nchunk):
        hg = _dot(x16, wg_ref[c])
        hu = _dot(x16, wu_ref[c])
        yg = conv_branch(c, hg, cwg_ref, cg_ref, c)
        yu = conv_branch(c, hu, cwu_ref, cu_ref, nchunk + c)
        for s in range(nseq):
            a_ref[s * lb:(s + 1) * lb, c * tn:(c + 1) * tn] = (jax.nn.gelu(yg[s]) * yu[s]).astype(BF16)
    y = _dot(a_ref[...], wd_ref[...])
    y_ref[...] = _layer_norm(ALPHA * x + y, g_ref[...], b_ref[...])


def _ffn(x2d, state, seq_len, prm):
    n = x2d.shape[0]
    tm, nseq, bps = _plan(n, seq_len)
    nblk = n // tm
    tn = FFN_CHUNK
    nchunk = D_FF // tn
    bsz = state.shape[0]
    st = jnp.pad(state, ((0, 0), (HALO - (FFN_CONV - 1), 0), (0, 0)))
    st = st.reshape(bsz, HALO, 2 * nchunk, tn).transpose(0, 2, 1, 3)
    row = pl.BlockSpec((tm, D_MODEL), lambda i: (i, 0))
    consts = [prm[k] for k in ("w_g", "w_u", "cw_g", "cw_u", "w_d", "ln_g", "ln_b")]
    y, so = pl.pallas_call(
        functools.partial(_ffn_kernel, tm=tm, nseq=nseq, bps=bps, nchunk=nchunk, tn=tn),
        grid=(nblk,),
        in_specs=[row, pl.BlockSpec((nseq, 2 * nchunk, HALO, tn), lambda i: (i // bps, 0, 0, 0))]
                 + [_const_spec(c.shape) for c in consts],
        out_specs=(row, pl.BlockSpec((nseq, 2 * nchunk, HALO, tn), lambda i: (i, 0, 0, 0))),
        out_shape=(jax.ShapeDtypeStruct((n, D_MODEL), F32),
                   jax.ShapeDtypeStruct((nblk * nseq, 2 * nchunk, HALO, tn), F32)),
        scratch_shapes=[pltpu.VMEM((tm, D_FF), BF16),
                        pltpu.VMEM((nchunk, HALO, tn), F32),
                        pltpu.VMEM((nchunk, HALO, tn), F32)],
        compiler_params=_params(("arbitrary",)),
        name="conv_ffn",
    )(x2d, st, *consts)
    so = so.reshape(bsz, -1, 2 * nchunk, HALO, tn)[:, -1]
    new_state = so[:, :, HALO - (FFN_CONV - 1):, :].transpose(0, 2, 1, 3).reshape(bsz, FFN_CONV - 1, 2 * D_FF)
    return y, new_state


def _odd_kernel(x_ref, wu_ref, wv_ref, sg_ref, sb_ref, ws_ref, bias_ref, wo_ref, g_ref, b_ref,
                y_ref, *rest, tm, lc, emit_v):
    gate_ref = rest[-1]
    x = x_ref[...]
    x16 = x.astype(BF16)
    u = jax.nn.gelu(_dot(x16, wu_ref[...]))
    v = _layer_norm(jax.nn.gelu(_dot(x16, wv_ref[...])), sg_ref[...], sb_ref[...])
    if emit_v:
        rest[0][...] = v
    v16 = v.astype(BF16)
    gw = C_WIDTH // C_GROUPS
    for c in range(tm // lc):
        rows = slice(c * lc, (c + 1) * lc)
        for g in range(C_GROUPS):
            cols = slice(g * gw, (g + 1) * gw)
            mixed = _dot(ws_ref[g], v16[rows, cols]) + bias_ref[:, cols]
            gate_ref[rows, cols] = (u[rows, cols] * mixed).astype(BF16)
    y = _dot(gate_ref[...], wo_ref[...])
    y_ref[...] = _layer_norm(ALPHA * x + y, g_ref[...], b_ref[...])


def _odd(x2d, seq_len, prm, emit_v):
    n = x2d.shape[0]
    tm = min(ROW_BLOCK, n)
    lc = min(seq_len, C_LEN)
    gw = C_WIDTH // C_GROUPS
    ws16 = (prm["sgu_w"][:, :lc, :lc] * jnp.tril(jnp.ones((lc, lc), F32))).astype(BF16)
    bias = jnp.repeat(prm["sgu_bias"][:, :lc].T, gw, axis=1)
    consts = [prm["w_u"], prm["w_v"], prm["sgu_g"], prm["sgu_b"], ws16, bias,
              prm["w_o"], prm["ln_g"], prm["ln_b"]]
    row = pl.BlockSpec((tm, D_MODEL), lambda i: (i, 0))
    n_out = 2 if emit_v else 1
    return pl.pallas_call(
        functools.partial(_odd_kernel, tm=tm, lc=lc, emit_v=emit_v),
        grid=(n // tm,),
        in_specs=[row] + [_const_spec(c.shape) for c in consts],
        out_specs=(row, pl.BlockSpec((tm, C_WIDTH), lambda i: (i, 0)))[:n_out],
        out_shape=(jax.ShapeDtypeStruct((n, D_MODEL), F32),
                   jax.ShapeDtypeStruct((n, C_WIDTH), F32))[:n_out],
        scratch_shapes=[pltpu.VMEM((tm, C_WIDTH), BF16)],
        compiler_params=_params(("arbitrary",)),
        name="odd_mixer",
    )(x2d, *consts)


def _row(v):
    return v.astype(F32)[None, :]


def _block_diag(w):
    nb, bd, _ = w.shape
    eye = jnp.eye(nb, dtype=w.dtype)
    return jnp.einsum("ncd,nm->ncmd", w, eye).reshape(nb * bd, nb * bd)


def _even_params(w_in, b_f, rg_conv_w, rg_conv_b, rg_wa, rg_ba, rg_wx, rg_bx, rg_lam, w_out, ln_g, ln_b):
    nf = 3 * A_WIDTH
    wf = jnp.pad(w_in[:, nf:nf + N_HEADS], ((0, 0), (0, LANES - N_HEADS)))
    return {
        "wqkv": w_in[:, :nf].astype(BF16),
        "wf": wf.astype(BF16),
        "wxg": w_in[:, nf + N_HEADS:].astype(BF16),
        "bf": jnp.pad(b_f, (0, LANES - N_HEADS))[None, :],
        "rg_cw": rg_conv_w, "rg_cb": _row(rg_conv_b),
        "rg_wa": _block_diag(rg_wa).astype(BF16), "rg_ba": _row(rg_ba),
        "rg_wx": _block_diag(rg_wx).astype(BF16), "rg_bx": _row(rg_bx),
        "rg_lam": _row(rg_lam),
        "wo_a": w_out[:A_WIDTH].astype(BF16), "wo_b": w_out[A_WIDTH:].astype(BF16),
        "ln_g": _row(ln_g), "ln_b": _row(ln_b),
    }


def _ffn_params(w_up, conv_w, conv_b, w_down, ln_g, ln_b):
    tn = FFN_CHUNK
    nchunk = D_FF // tn
    chunked = lambda w: w.reshape(D_MODEL, nchunk, tn).transpose(1, 0, 2).astype(BF16)
    taps = jnp.concatenate([conv_w, conv_b[None, :],
                            jnp.zeros((HALO - FFN_CONV - 1, 2 * D_FF), F32)], axis=0)
    taps_c = lambda w: w.reshape(HALO, nchunk, tn).transpose(1, 0, 2)
    return {
        "w_g": chunked(w_up[:, :D_FF]), "w_u": chunked(w_up[:, D_FF:]),
        "cw_g": taps_c(taps[:, :D_FF]), "cw_u": taps_c(taps[:, D_FF:]),
        "w_d": w_down.astype(BF16),
        "ln_g": _row(ln_g), "ln_b": _row(ln_b),
    }


def _odd_params(w_in, sgu_g, sgu_b, sgu_w, sgu_bias, w_out, ln_g, ln_b):
    return {
        "w_u": w_in[:, :C_WIDTH].astype(BF16), "w_v": w_in[:, C_WIDTH:].astype(BF16),
        "sgu_g": _row(sgu_g), "sgu_b": _row(sgu_b), "sgu_w": sgu_w, "sgu_bias": sgu_bias,
        "w_o": w_out.astype(BF16), "ln_g": _row(ln_g), "ln_b": _row(ln_b),
    }


def _even_layer(x3d, past, prm):
    bsz, t, _ = x3d.shape
    x2d = x3d.reshape(bsz * t, D_MODEL)
    qkv16, k, v, logf, cum, xb, gb = _even_in(x2d, t, prm["wqkv"], prm["wf"], prm["wxg"], prm["bf"])
    qkv16 = qkv16.reshape(bsz, t, 3 * A_WIDTH)
    cum3 = cum.reshape(bsz, t, LANES)
    cum_t = cum3[:, :, :N_HEADS].transpose(0, 2, 1)
    if past is None:
        attn = _fox_prompt(qkv16, cum3, cum_t, ROW_BLOCK)
        conv_s = jnp.zeros((bsz, RG_CONV - 1, B_WIDTH), F32)
        h0 = jnp.zeros((bsz, B_WIDTH), F32)
    else:
        k_c, v_c, lf_c, conv_s, h0 = past
        plen = k_c.shape[1]
        attn = _fox_sample(qkv16, cum3, cum_t, k_c.reshape(bsz, plen, A_WIDTH),
                           v_c.reshape(bsz, plen, A_WIDTH), lf_c.transpose(0, 2, 1))
    y, h_last = _even_tail(x2d, attn.reshape(bsz * t, A_WIDTH), xb, gb, conv_s, h0, t, prm)
    new_k = k.reshape(bsz, t, N_HEADS, HEAD_DIM)
    new_v = v.reshape(bsz, t, N_HEADS, HEAD_DIM)
    new_logf = logf.reshape(bsz, t, LANES)[:, :, :N_HEADS]
    xb3 = xb.reshape(bsz, t, B_WIDTH)
    conv_new = jnp.concatenate([conv_s, xb3], axis=1)[:, -(RG_CONV - 1):] if t < RG_CONV - 1 \
        else xb3[:, -(RG_CONV - 1):]
    return y.reshape(bsz, t, D_MODEL), (new_k, new_v, new_logf, conv_new, h_last)


def _ffn_layer(x3d, state, prm):
    bsz, t, _ = x3d.shape
    y, new_state = _ffn(x3d.reshape(bsz * t, D_MODEL), state, t, prm)
    return y.reshape(bsz, t, D_MODEL), new_state


def _odd_layer(x3d, prm, emit_v):
    bsz, t, _ = x3d.shape
    outs = _odd(x3d.reshape(bsz * t, D_MODEL), t, prm, emit_v)
    v = outs[1].reshape(bsz, t, C_WIDTH) if emit_v else None
    return outs[0].reshape(bsz, t, D_MODEL), v


def kernel(x_prompt, x_sample, cache_k, cache_v, cache_logf, state_rglru_conv, state_rglru_h, state_ffn_conv, w_in_e, b_f, rg_conv_w, rg_conv_b, rg_wa, rg_ba, rg_wx, rg_bx, rg_lam, w_out_e, w_in_o, sgu_g, sgu_b, sgu_w, sgu_bias, w_out_o, ln_mix_g, ln_mix_b, ln_ffn_g, ln_ffn_b, ffn_w_up, ffn_conv_w, ffn_conv_b, ffn_w_down):
    xp, xs = x_prompt, x_sample
    even_p, even_s, ffn_p, ffn_s, sgu_v = [], [], [], [], []
    for layer in range(DEPTH):
        if layer % 2 == 0:
            e = layer // 2
            prm = _even_params(w_in_e[e], b_f[e], rg_conv_w[e], rg_conv_b[e], rg_wa[e], rg_ba[e],
                               rg_wx[e], rg_bx[e], rg_lam[e], w_out_e[e],
                               ln_mix_g[layer], ln_mix_b[layer])
            xp, st_p = _even_layer(xp, None, prm)
            past = (cache_k[e], cache_v[e], cache_logf[e], state_rglru_conv[e], state_rglru_h[e])
            xs, st_s = _even_layer(xs, past, prm)
            even_p.append(st_p)
            even_s.append(st_s)
        else:
            o = layer // 2
            prm = _odd_params(w_in_o[o], sgu_g[o], sgu_b[o], sgu_w[o], sgu_bias[o], w_out_o[o],
                              ln_mix_g[layer], ln_mix_b[layer])
            xp, _ = _odd_layer(xp, prm, emit_v=False)
            xs, sv = _odd_layer(xs, prm, emit_v=True)
            sgu_v.append(sv)
        fprm = _ffn_params(ffn_w_up[layer], ffn_conv_w[layer], ffn_conv_b[layer], ffn_w_down[layer],
                           ln_ffn_g[layer], ln_ffn_b[layer])
        zero_fc = jnp.zeros((xp.shape[0], FFN_CONV - 1, 2 * D_FF), F32)
        xp, fcp = _ffn_layer(xp, zero_fc, fprm)
        xs, fcs = _ffn_layer(xs, state_ffn_conv[layer], fprm)
        ffn_p.append(fcp)
        ffn_s.append(fcs)
    stack = lambda lst, idx: jnp.stack([item[idx] for item in lst])
    return (xp, xs,
            stack(even_p, 0), stack(even_p, 1), stack(even_p, 2),
            stack(even_s, 0), stack(even_s, 1), stack(even_s, 2),
            stack(even_p, 3), stack(even_p, 4), stack(even_s, 3), stack(even_s, 4),
            jnp.stack(ffn_p), jnp.stack(ffn_s), jnp.stack(sgu_v))
```

```python
import functools

import jax
import jax.numpy as jnp
from jax import lax
from jax.experimental import pallas as pl
from jax.experimental.pallas import tpu as pltpu

F32 = jnp.float32
BF16 = jnp.bfloat16

D_MODEL = 1024
N_HEADS = 8
HEAD_DIM = 64
A_WIDTH = N_HEADS * HEAD_DIM
B_WIDTH = 512
B_BLOCKS = 8
RG_CONV = 4
RG_C = 8.0
C_WIDTH = 1024
C_GROUPS = 8
C_LEN = 128
D_FF = 2816
FFN_CONV = 3
DEPTH = 2
ALPHA = (2 * DEPTH) ** 0.25
LN_EPS = 1e-5

LANES = 128
SUBLANES = 8
HALO = SUBLANES
NEG = -1e30
ROW_BLOCK = 512
FFN_CHUNK = 256
VMEM_LIMIT = 48 * 1024 * 1024


def _layer_norm(x, g, b):
    mu = jnp.mean(x, axis=-1, keepdims=True)
    xc = x - mu
    var = jnp.mean(xc * xc, axis=-1, keepdims=True)
    return xc * lax.rsqrt(var + LN_EPS) * g + b


def _log_sigmoid(x):
    return jnp.minimum(x, 0.0) - jnp.log1p(jnp.exp(-jnp.abs(x)))


def _softplus(x):
    return jnp.maximum(x, 0.0) + jnp.log1p(jnp.exp(-jnp.abs(x)))


def _split3(x):
    hi = x.astype(BF16)
    r1 = x - hi.astype(F32)
    mid = r1.astype(BF16)
    lo = (r1 - mid.astype(F32)).astype(BF16)
    return hi, mid, lo


def _dot(a, b):
    return jnp.dot(a, b, preferred_element_type=F32)


def _dot_nt(a, b):
    return lax.dot_general(a, b, (((1,), (1,)), ((), ())), preferred_element_type=F32)


def _dot3(pieces, ones_matrix, left):
    out = None
    for p in pieces:
        t = _dot(ones_matrix, p) if left else _dot(p, ones_matrix)
        out = t if out is None else out + t
    return out


def _const_spec(shape):
    nd = len(shape)
    return pl.BlockSpec(shape, lambda *_: (0,) * nd, pipeline_mode=pl.Buffered(1))


def _params(sem):
    return pltpu.CompilerParams(dimension_semantics=sem, vmem_limit_bytes=VMEM_LIMIT)


def _plan(n_rows, seq_len):
    tm = min(ROW_BLOCK, n_rows)
    nseq = max(1, tm // seq_len)
    bps = max(1, seq_len // tm)
    assert n_rows % tm == 0 and (seq_len % tm == 0 or tm % seq_len == 0)
    return tm, nseq, bps


def _even_in_kernel(x_ref, wqkv_ref, wf_ref, wxg_ref, bf_ref, tri_ref,
                    qkv16_ref, k_ref, v_ref, logf_ref, cum_ref, xb_ref, gb_ref,
                    carry_ref, *, tm, bps):
    i = pl.program_id(0)
    x16 = x_ref[...].astype(BF16)
    qkv = _dot(x16, wqkv_ref[...])
    qkv16_ref[:, :A_WIDTH] = (qkv[:, :A_WIDTH] * (HEAD_DIM ** -0.5)).astype(BF16)
    qkv16_ref[:, A_WIDTH:] = qkv[:, A_WIDTH:].astype(BF16)
    k_ref[...] = qkv[:, A_WIDTH:2 * A_WIDTH]
    v_ref[...] = qkv[:, 2 * A_WIDTH:]
    logf = _log_sigmoid(_dot(x16, wf_ref[...]) + bf_ref[...])
    logf_ref[...] = logf
    cs = _dot3(_split3(logf), tri_ref[...], left=True)
    if bps > 1:
        @pl.when(i % bps == 0)
        def _():
            carry_ref[...] = jnp.zeros_like(carry_ref)
        cs = cs + carry_ref[...]
        carry_ref[...] = cs[tm - 1:tm, :]
    cum_ref[...] = cs
    xg = _dot(x16, wxg_ref[...])
    xb_ref[...] = xg[:, :B_WIDTH]
    gb_ref[...] = xg[:, B_WIDTH:]


def _even_in(x2d, seq_len, wqkv16, wf16, wxg16, bf_pad):
    n = x2d.shape[0]
    tm, nseq, bps = _plan(n, seq_len)
    r = jnp.arange(tm)
    tri = ((r[None, :] <= r[:, None]) & ((r[None, :] // seq_len) == (r[:, None] // seq_len))).astype(BF16)
    row = lambda w: pl.BlockSpec((tm, w), lambda i: (i, 0))
    out_shapes = (
        jax.ShapeDtypeStruct((n, 3 * A_WIDTH), BF16),
        jax.ShapeDtypeStruct((n, A_WIDTH), F32),
        jax.ShapeDtypeStruct((n, A_WIDTH), F32),
        jax.ShapeDtypeStruct((n, LANES), F32),
        jax.ShapeDtypeStruct((n, LANES), F32),
        jax.ShapeDtypeStruct((n, B_WIDTH), F32),
        jax.ShapeDtypeStruct((n, B_WIDTH), F32),
    )
    return pl.pallas_call(
        functools.partial(_even_in_kernel, tm=tm, bps=bps),
        grid=(n // tm,),
        in_specs=[row(D_MODEL), _const_spec(wqkv16.shape), _const_spec(wf16.shape),
                  _const_spec(wxg16.shape), _const_spec(bf_pad.shape), _const_spec(tri.shape)],
        out_specs=(row(3 * A_WIDTH), row(A_WIDTH), row(A_WIDTH), row(LANES), row(LANES),
                   row(B_WIDTH), row(B_WIDTH)),
        out_shape=out_shapes,
        scratch_shapes=[pltpu.VMEM((1, LANES), F32)],
        compiler_params=_params(("arbitrary",)),
        name="even_in_proj",
    )(x2d, wqkv16, wf16, wxg16, bf_pad, tri)


def _softmax_step(s, cq, m_old, l_old, acc_old, v16):
    m_new = jnp.maximum(m_old, jnp.max(s, axis=1, keepdims=True) + cq)
    p = jnp.exp(s - (m_new - cq))
    alpha = jnp.exp(m_old - m_new)
    l_new = alpha * l_old + jnp.sum(p, axis=1, keepdims=True)
    acc_new = alpha * acc_old + _dot(p.astype(BF16), v16)
    return m_new, l_new, acc_new


def _fox_prompt_kernel(q_ref, k_ref, v_ref, cq_ref, ck_ref, o_ref,
                       qm_ref, cqr_ref, m_ref, l_ref, acc_ref, *, blk):
    i = pl.program_id(1)
    lane = lax.broadcasted_iota(jnp.int32, (blk, LANES), 1)
    ngrp = blk // LANES

    for h in range(N_HEADS):
        hp, hh = divmod(h, 2)
        q = q_ref[0, :, hp * LANES:(hp + 1) * LANES]
        keep = (lane < HEAD_DIM) if hh == 0 else (lane >= HEAD_DIM)
        qm_ref[h] = jnp.where(keep, q, jnp.zeros_like(q))
        cqr_ref[h] = jnp.broadcast_to(cq_ref[0, :, h:h + 1], (blk, LANES))
    m_ref[...] = jnp.full(m_ref.shape, NEG, F32)
    l_ref[...] = jnp.zeros_like(l_ref)
    acc_ref[...] = jnp.zeros_like(acc_ref)

    def update(kg0, ngrp, diagonal):
        koff = pl.multiple_of(kg0 * LANES, LANES)
        if diagonal:
            rows = lax.broadcasted_iota(jnp.int32, (blk, LANES), 0)
            visible = [lane + g * LANES <= rows for g in range(ngrp)]
        for h in range(N_HEADS):
            hp = h // 2
            k16 = k_ref[0, pl.ds(koff, ngrp * LANES), hp * LANES:(hp + 1) * LANES]
            v16 = v_ref[0, pl.ds(koff, ngrp * LANES), hp * LANES:(hp + 1) * LANES]
            s = _dot_nt(qm_ref[h], k16)
            grp = [s[:, g * LANES:(g + 1) * LANES] - ck_ref[0, kg0 + g, h:h + 1, :] for g in range(ngrp)]
            if diagonal:
                grp = [jnp.where(vis, g, NEG) for vis, g in zip(visible, grp)]
            mc = grp[0]
            for g in grp[1:]:
                mc = jnp.maximum(mc, g)
            cqr = cqr_ref[h]
            m_old = m_ref[h]
            m_new = jnp.maximum(m_old, jnp.max(mc, axis=1, keepdims=True) + cqr)
            shift = m_new - cqr
            ps = [jnp.exp(g - shift) for g in grp]
            psum = ps[0]
            for p in ps[1:]:
                psum = psum + p
            alpha = jnp.exp(m_old - m_new)
            l_ref[h] = alpha * l_ref[h] + psum
            p16 = jnp.concatenate([p.astype(BF16) for p in ps], axis=1)
            acc_ref[h] = alpha * acc_ref[h] + _dot(p16, v16)
            m_ref[h] = m_new

    def body(j, carry):
        update(j * (2 * ngrp), 2 * ngrp, False)
        return carry

    lax.fori_loop(0, i // 2, body, 0)

    @pl.when(i % 2 == 1)
    def _():
        update((i - 1) * ngrp, ngrp, False)

    update(i * ngrp, ngrp, True)
    for hp in range(N_HEADS // 2):
        l0 = jnp.sum(l_ref[2 * hp], axis=1, keepdims=True)
        l1 = jnp.sum(l_ref[2 * hp + 1], axis=1, keepdims=True)
        o = jnp.where(lane < HEAD_DIM, acc_ref[2 * hp] / l0, acc_ref[2 * hp + 1] / l1)
        o_ref[0, :, hp * LANES:(hp + 1) * LANES] = o.astype(o_ref.dtype)


def _fox_prompt(qkv16, cum, cum_t, blk):
    bsz, t, _ = qkv16.shape
    nb = t // blk
    ng = t // LANES
    ck = cum_t.reshape(bsz, N_HEADS, ng, LANES).transpose(0, 2, 1, 3)
    return pl.pallas_call(
        functools.partial(_fox_prompt_kernel, blk=blk),
        grid=(bsz, nb),
        in_specs=[pl.BlockSpec((1, blk, A_WIDTH), lambda b, i: (b, i, 0)),
                  pl.BlockSpec((1, t, A_WIDTH), lambda b, i: (b, 0, 1)),
                  pl.BlockSpec((1, t, A_WIDTH), lambda b, i: (b, 0, 2)),
                  pl.BlockSpec((1, blk, LANES), lambda b, i: (b, i, 0)),
                  pl.BlockSpec((1, ng, N_HEADS, LANES), lambda b, i: (b, 0, 0, 0))],
        out_specs=pl.BlockSpec((1, blk, A_WIDTH), lambda b, i: (b, i, 0)),
        out_shape=jax.ShapeDtypeStruct((bsz, t, A_WIDTH), BF16),
        scratch_shapes=[pltpu.VMEM((N_HEADS, blk, LANES), BF16),
                        pltpu.VMEM((N_HEADS, blk, LANES), F32),
                        pltpu.VMEM((N_HEADS, blk, LANES), F32),
                        pltpu.VMEM((N_HEADS, blk, LANES), F32),
                        pltpu.VMEM((N_HEADS, blk, LANES), F32)],
        compiler_params=_params(("arbitrary", "arbitrary")),
        name="fox_prompt",
    )(qkv16, qkv16, qkv16, cum, ck)


def _fox_sample_kernel(qkv_ref, cq_ref, ckn_ref, lf_ref, upper_ref, kc_hbm, vc_hbm, o_ref,
                       ckc_ref, kbuf, vbuf, sem, *, t, plen, chunk):
    b = pl.program_id(0)
    nb = pl.num_programs(0)
    slot = b % 2

    def copies(bb, sl):
        out = []
        for h in range(N_HEADS):
            out.append(pltpu.make_async_copy(kc_hbm.at[bb, :, h, :], kbuf.at[sl, h], sem.at[0, sl, h]))
            out.append(pltpu.make_async_copy(vc_hbm.at[bb, :, h, :], vbuf.at[sl, h], sem.at[1, sl, h]))
        return out

    @pl.when(b == 0)
    def _():
        for cp in copies(0, 0):
            cp.start()

    @pl.when(b + 1 < nb)
    def _():
        for cp in copies(b + 1, 1 - slot):
            cp.start()

    carry = jnp.zeros((N_HEADS, 1), F32)
    for c in reversed(range(plen // chunk)):
        lf = lf_ref[0, :, c * chunk:(c + 1) * chunk]
        after = _dot3(_split3(lf), upper_ref[...], left=False)
        ckc_ref[:, c * chunk:(c + 1) * chunk] = -(after + carry)
        carry = carry + jnp.sum(lf, axis=1, keepdims=True)
    rows = lax.broadcasted_iota(jnp.int32, (t, t), 0)
    cols = lax.broadcasted_iota(jnp.int32, (t, t), 1)
    visible = cols <= rows
    for cp in copies(b, slot):
        cp.wait()
    outs = []
    for h in range(N_HEADS):
        q = qkv_ref[0, :, h * HEAD_DIM:(h + 1) * HEAD_DIM]
        kn16 = qkv_ref[0, :, A_WIDTH + h * HEAD_DIM:A_WIDTH + (h + 1) * HEAD_DIM]
        vn16 = qkv_ref[0, :, 2 * A_WIDTH + h * HEAD_DIM:2 * A_WIDTH + (h + 1) * HEAD_DIM]
        cq = cq_ref[0, :, h:h + 1]
        s_new = jnp.where(visible, _dot_nt(q, kn16) - ckn_ref[0, h:h + 1, :], NEG)
        m0 = jnp.full((t, 1), NEG, F32)
        z = jnp.zeros((t, 1), F32)
        m, l, acc = _softmax_step(s_new, cq, m0, z, jnp.zeros((t, HEAD_DIM), F32), vn16)
        s_old = _dot_nt(q, kbuf[slot, h].astype(BF16)) - ckc_ref[h:h + 1, :]
        m, l, acc = _softmax_step(s_old, cq, m, l, acc, vbuf[slot, h].astype(BF16))
        outs.append(acc / l)
    o_ref[0] = jnp.concatenate(outs, axis=1).astype(o_ref.dtype)


def _fox_sample(qkv16, cum, cum_t, kc, vc, lf_t, chunk=512):
    bsz, t, _ = qkv16.shape
    plen = kc.shape[1]
    r = jnp.arange(chunk)
    upper = (r[:, None] > r[None, :]).astype(BF16)
    per_b = lambda shape: pl.BlockSpec((1,) + shape, lambda b: (b, 0, 0))
    hbm = pl.BlockSpec(memory_space=pl.ANY)
    return pl.pallas_call(
        functools.partial(_fox_sample_kernel, t=t, plen=plen, chunk=chunk),
        grid=(bsz,),
        in_specs=[per_b((t, 3 * A_WIDTH)), per_b((t, LANES)), per_b((N_HEADS, t)),
                  per_b((N_HEADS, plen)), _const_spec(upper.shape), hbm, hbm],
        out_specs=per_b((t, A_WIDTH)),
        out_shape=jax.ShapeDtypeStruct((bsz, t, A_WIDTH), BF16),
        scratch_shapes=[pltpu.VMEM((N_HEADS, plen), F32),
                        pltpu.VMEM((2, N_HEADS, plen, HEAD_DIM), F32),
                        pltpu.VMEM((2, N_HEADS, plen, HEAD_DIM), F32),
                        pltpu.SemaphoreType.DMA((2, 2, N_HEADS))],
        compiler_params=_params(("arbitrary",)),
        name="fox_sample",
    )(qkv16, cum, cum_t, lf_t, upper, kc, vc)


def _even_tail_kernel(x_ref, attn_ref, xb_ref, gb_ref, cs_ref, h0_ref, cw_ref, cb_ref,
                      wa_ref, ba_ref, wx_ref, bx_ref, lam_ref, woa_ref, wob_ref, g_ref, b_ref,
                      y_ref, hl_ref, s_ref, ccarry_ref, hcarry_ref, *, tm, nseq, bps):
    i = pl.program_id(0)
    lb = tm // nseq
    xb = xb_ref[...]
    if bps > 1:
        first = (i % bps) == 0
        halos = [jnp.where(first, cs_ref[0], ccarry_ref[...])]
        hinit = jnp.where(first, h0_ref[0], hcarry_ref[...])
        ccarry_ref[...] = xb[tm - HALO:, :]
    else:
        halos = [cs_ref[s] for s in range(nseq)]
        hinit = jnp.concatenate(
            [jnp.broadcast_to(h0_ref[s], (lb, B_WIDTH)) for s in range(nseq)], axis=0)
    for s in range(nseq):
        base = s * (lb + HALO)
        s_ref[base:base + HALO, :] = halos[s]
        s_ref[base + HALO:base + HALO + lb, :] = xb[s * lb:(s + 1) * lb, :]
    pieces = []
    for s in range(nseq):
        base = s * (lb + HALO) + HALO
        xc = cb_ref[...] + cw_ref[RG_CONV - 1:RG_CONV, :] * xb[s * lb:(s + 1) * lb, :]
        for d in range(1, RG_CONV):
            xc = xc + cw_ref[RG_CONV - 1 - d:RG_CONV - d, :] * s_ref[base - d:base - d + lb, :]
        pieces.append(xc)
    xc = pieces[0] if nseq == 1 else jnp.concatenate(pieces, axis=0)
    xc16 = xc.astype(BF16)
    gate_r = jax.nn.sigmoid(_dot(xc16, wa_ref[...]) + ba_ref[...])
    gate_i = jax.nn.sigmoid(_dot(xc16, wx_ref[...]) + bx_ref[...])
    log_a = -RG_C * gate_r * _softplus(-lam_ref[...])
    a = jnp.exp(log_a)
    u = jnp.sqrt(jnp.tanh(-log_a) * (a * a + 1.0)) * (gate_i * xc)
    pos = lax.broadcasted_iota(jnp.int32, (tm, B_WIDTH), 0) % lb
    d = 1
    while d < lb:
        inside = pos >= d
        u = jnp.where(inside, a * pltpu.roll(u, d, axis=0) + u, u)
        a = jnp.where(inside, a * pltpu.roll(a, d, axis=0), a)
        d *= 2
    h = u + a * hinit
    if bps > 1:
        hcarry_ref[...] = h[tm - 1:tm, :]
    for s in range(nseq):
        hl_ref[0, s:s + 1, :] = h[(s + 1) * lb - 1:(s + 1) * lb, :]
    gated = (jax.nn.gelu(gb_ref[...]) * h).astype(BF16)
    y = _dot(attn_ref[...], woa_ref[...]) + _dot(gated, wob_ref[...])
    y_ref[...] = _layer_norm(ALPHA * x_ref[...] + y, g_ref[...], b_ref[...])


def _even_tail(x2d, attn16, xb, gb, conv_state, h0, seq_len, prm):
    n = x2d.shape[0]
    tm, nseq, bps = _plan(n, seq_len)
    nblk = n // tm
    cs = jnp.pad(conv_state, ((0, 0), (HALO - (RG_CONV - 1), 0), (0, 0)))
    h03 = h0.astype(F32)[:, None, :]
    row = lambda w: pl.BlockSpec((tm, w), lambda i: (i, 0))
    state = lambda r: pl.BlockSpec((nseq, r, B_WIDTH), lambda i: (i // bps, 0, 0))
    consts = [prm[k] for k in ("rg_cw", "rg_cb", "rg_wa", "rg_ba", "rg_wx", "rg_bx", "rg_lam",
                                "wo_a", "wo_b", "ln_g", "ln_b")]
    y, hl = pl.pallas_call(
        functools.partial(_even_tail_kernel, tm=tm, nseq=nseq, bps=bps),
        grid=(nblk,),
        in_specs=[row(D_MODEL), row(A_WIDTH), row(B_WIDTH), row(B_WIDTH), state(HALO), state(1)]
                 + [_const_spec(c.shape) for c in consts],
        out_specs=(row(D_MODEL), pl.BlockSpec((1, nseq, B_WIDTH), lambda i: (i, 0, 0))),
        out_shape=(jax.ShapeDtypeStruct((n, D_MODEL), F32),
                   jax.ShapeDtypeStruct((nblk, nseq, B_WIDTH), F32)),
        scratch_shapes=[pltpu.VMEM((nseq * (tm // nseq + HALO), B_WIDTH), F32),
                        pltpu.VMEM((HALO, B_WIDTH), F32),
                        pltpu.VMEM((1, B_WIDTH), F32)],
        compiler_params=_params(("arbitrary",)),
        name="even_tail",
    )(x2d, attn16, xb, gb, cs, h03, *consts)
    h_last = hl.reshape(n // seq_len, -1, B_WIDTH)[:, -1, :]
    return y, h_last


def _shift_rows(h, halo, d):
    rolled = pltpu.roll(h, d, axis=0)
    rowid = lax.broadcasted_iota(jnp.int32, halo.shape, 0)
    head = jnp.where(rowid < d, pltpu.roll(halo, d, axis=0), rolled[:HALO, :])
    return jnp.concatenate([head, rolled[HALO:, :]], axis=0)


def _ffn_kernel(x_ref, st_ref, wup_ref, taps_ref, wd_ref, g_ref, b_ref,
                y_ref, so_ref, a_ref, carry_ref, *, tm, nseq, bps, tn):
    i = pl.program_id(0)
    lb = tm // nseq
    x = x_ref[...]
    x16 = x.astype(BF16)
    first = (i % bps) == 0

    def conv_branch(h, cols):
        taps = taps_ref[:, cols]
        outs = []
        for s in range(nseq):
            hs = h[s * lb:(s + 1) * lb, :]
            if bps > 1:
                halo = jnp.where(first, st_ref[0, :, cols], carry_ref[:, cols])
            else:
                halo = st_ref[s, :, cols]
            so_ref[s, :, cols] = hs[lb - HALO:, :]
            y = taps[3:4, :] + taps[2:3, :] * hs
            y = y + taps[1:2, :] * _shift_rows(hs, halo, 1)
            y = y + taps[0:1, :] * _shift_rows(hs, halo, 2)
            outs.append(y)
        if bps > 1:
            carry_ref[:, cols] = h[tm - HALO:, :]
        return outs

    for c in range(D_FF // tn):
        cg = slice(c * tn, (c + 1) * tn)
        cu = slice(D_FF + c * tn, D_FF + (c + 1) * tn)
        yg = conv_branch(_dot(x16, wup_ref[:, cg]), cg)
        yu = conv_branch(_dot(x16, wup_ref[:, cu]), cu)
        for s in range(nseq):
            a_ref[s * lb:(s + 1) * lb, cg] = (jax.nn.gelu(yg[s]) * yu[s]).astype(BF16)
    y = _dot(a_ref[...], wd_ref[...])
    y_ref[...] = _layer_norm(ALPHA * x + y, g_ref[...], b_ref[...])


def _ffn(x2d, state, seq_len, prm):
    n = x2d.shape[0]
    tm, nseq, bps = _plan(n, seq_len)
    nblk = n // tm
    bsz = state.shape[0]
    st = jnp.pad(state, ((0, 0), (HALO - (FFN_CONV - 1), 0), (0, 0)))
    row = pl.BlockSpec((tm, D_MODEL), lambda i: (i, 0))
    consts = [prm[k] for k in ("w_up", "taps", "w_d", "ln_g", "ln_b")]
    y, so = pl.pallas_call(
        functools.partial(_ffn_kernel, tm=tm, nseq=nseq, bps=bps, tn=FFN_CHUNK),
        grid=(nblk,),
        in_specs=[row, pl.BlockSpec((nseq, HALO, 2 * D_FF), lambda i: (i // bps, 0, 0))]
                 + [_const_spec(c.shape) for c in consts],
        out_specs=(row, pl.BlockSpec((nseq, HALO, 2 * D_FF), lambda i: (i, 0, 0))),
        out_shape=(jax.ShapeDtypeStruct((n, D_MODEL), F32),
                   jax.ShapeDtypeStruct((nblk * nseq, HALO, 2 * D_FF), F32)),
        scratch_shapes=[pltpu.VMEM((tm, D_FF), BF16),
                        pltpu.VMEM((HALO, 2 * D_FF), F32)],
        compiler_params=_params(("arbitrary",)),
        name="conv_ffn",
    )(x2d, st, *consts)
    so = so.reshape(bsz, -1, HALO, 2 * D_FF)[:, -1]
    return y, so[:, HALO - (FFN_CONV - 1):, :]


def _odd_kernel(x_ref, wu_ref, wv_ref, sg_ref, sb_ref, ws_ref, bias_ref, wo_ref, g_ref, b_ref,
                y_ref, *rest, tm, lc, emit_v):
    gate_ref = rest[-1]
    x = x_ref[...]
    x16 = x.astype(BF16)
    u = jax.nn.gelu(_dot(x16, wu_ref[...]))
    v = _layer_norm(jax.nn.gelu(_dot(x16, wv_ref[...])), sg_ref[...], sb_ref[...])
    if emit_v:
        rest[0][...] = v
    v16 = v.astype(BF16)
    gw = C_WIDTH // C_GROUPS
    for c in range(tm // lc):
        rows = slice(c * lc, (c + 1) * lc)
        for g in range(C_GROUPS):
            cols = slice(g * gw, (g + 1) * gw)
            mixed = _dot(ws_ref[g], v16[rows, cols]) + bias_ref[:, cols]
            gate_ref[rows, cols] = (u[rows, cols] * mixed).astype(BF16)
    y = _dot(gate_ref[...], wo_ref[...])
    y_ref[...] = _layer_norm(ALPHA * x + y, g_ref[...], b_ref[...])


def _odd(x2d, seq_len, prm, emit_v):
    n = x2d.shape[0]
    tm = min(ROW_BLOCK, n)
    lc = min(seq_len, C_LEN)
    gw = C_WIDTH // C_GROUPS
    ws16 = (prm["sgu_w"][:, :lc, :lc] * jnp.tril(jnp.ones((lc, lc), F32))).astype(BF16)
    bias = jnp.repeat(prm["sgu_bias"][:, :lc].T, gw, axis=1)
    consts = [prm["w_u"], prm["w_v"], prm["sgu_g"], prm["sgu_b"], ws16, bias,
              prm["w_o"], prm["ln_g"], prm["ln_b"]]
    row = pl.BlockSpec((tm, D_MODEL), lambda i: (i, 0))
    n_out = 2 if emit_v else 1
    return pl.pallas_call(
        functools.partial(_odd_kernel, tm=tm, lc=lc, emit_v=emit_v),
        grid=(n // tm,),
        in_specs=[row] + [_const_spec(c.shape) for c in consts],
        out_specs=(row, pl.BlockSpec((tm, C_WIDTH), lambda i: (i, 0)))[:n_out],
        out_shape=(jax.ShapeDtypeStruct((n, D_MODEL), F32),
                   jax.ShapeDtypeStruct((n, C_WIDTH), F32))[:n_out],
        scratch_shapes=[pltpu.VMEM((tm, C_WIDTH), BF16)],
        compiler_params=_params(("arbitrary",)),
        name="odd_mixer",
    )(x2d, *consts)


def _row(v):
    return v.astype(F32)[None, :]


def _block_diag(w):
    nb, bd, _ = w.shape
    eye = jnp.eye(nb, dtype=w.dtype)
    return jnp.einsum("ncd,nm->ncmd", w, eye).reshape(nb * bd, nb * bd)


def _even_params(w_in, b_f, rg_conv_w, rg_conv_b, rg_wa, rg_ba, rg_wx, rg_bx, rg_lam, w_out, ln_g, ln_b):
    nf = 3 * A_WIDTH
    wf = jnp.pad(w_in[:, nf:nf + N_HEADS], ((0, 0), (0, LANES - N_HEADS)))
    return {
        "wqkv": w_in[:, :nf].astype(BF16),
        "wf": wf.astype(BF16),
        "wxg": w_in[:, nf + N_HEADS:].astype(BF16),
        "bf": jnp.pad(b_f, (0, LANES - N_HEADS))[None, :],
        "rg_cw": rg_conv_w, "rg_cb": _row(rg_conv_b),
        "rg_wa": _block_diag(rg_wa).astype(BF16), "rg_ba": _row(rg_ba),
        "rg_wx": _block_diag(rg_wx).astype(BF16), "rg_bx": _row(rg_bx),
        "rg_lam": _row(rg_lam),
        "wo_a": w_out[:A_WIDTH].astype(BF16), "wo_b": w_out[A_WIDTH:].astype(BF16),
        "ln_g": _row(ln_g), "ln_b": _row(ln_b),
    }


def _ffn_params(w_up, conv_w, conv_b, w_down, ln_g, ln_b):
    taps = jnp.concatenate([conv_w, conv_b[None, :],
                            jnp.zeros((HALO - FFN_CONV - 1, 2 * D_FF), F32)], axis=0)
    return {"w_up": w_up.astype(BF16), "taps": taps, "w_d": w_down.astype(BF16),
            "ln_g": _row(ln_g), "ln_b": _row(ln_b)}


def _odd_params(w_in, sgu_g, sgu_b, sgu_w, sgu_bias, w_out, ln_g, ln_b):
    return {
        "w_u": w_in[:, :C_WIDTH].astype(BF16), "w_v": w_in[:, C_WIDTH:].astype(BF16),
        "sgu_g": _row(sgu_g), "sgu_b": _row(sgu_b), "sgu_w": sgu_w, "sgu_bias": sgu_bias,
        "w_o": w_out.astype(BF16), "ln_g": _row(ln_g), "ln_b": _row(ln_b),
    }


def _even_layer(x3d, past, prm):
    bsz, t, _ = x3d.shape
    x2d = x3d.reshape(bsz * t, D_MODEL)
    qkv16, k, v, logf, cum, xb, gb = _even_in(x2d, t, prm["wqkv"], prm["wf"], prm["wxg"], prm["bf"])
    qkv16 = qkv16.reshape(bsz, t, 3 * A_WIDTH)
    cum3 = cum.reshape(bsz, t, LANES)
    cum_t = cum3[:, :, :N_HEADS].transpose(0, 2, 1)
    if past is None:
        attn = _fox_prompt(qkv16, cum3, cum_t, ROW_BLOCK)
        conv_s = jnp.zeros((bsz, RG_CONV - 1, B_WIDTH), F32)
        h0 = jnp.zeros((bsz, B_WIDTH), F32)
    else:
        k_c, v_c, lf_c, conv_s, h0 = past
        plen = k_c.shape[1]
        attn = _fox_sample(qkv16, cum3, cum_t, k_c, v_c, lf_c.transpose(0, 2, 1))
    y, h_last = _even_tail(x2d, attn.reshape(bsz * t, A_WIDTH), xb, gb, conv_s, h0, t, prm)
    new_k = k.reshape(bsz, t, N_HEADS, HEAD_DIM)
    new_v = v.reshape(bsz, t, N_HEADS, HEAD_DIM)
    new_logf = logf.reshape(bsz, t, LANES)[:, :, :N_HEADS]
    xb3 = xb.reshape(bsz, t, B_WIDTH)
    conv_new = jnp.concatenate([conv_s, xb3], axis=1)[:, -(RG_CONV - 1):] if t < RG_CONV - 1 \
        else xb3[:, -(RG_CONV - 1):]
    return y.reshape(bsz, t, D_MODEL), (new_k, new_v, new_logf, conv_new, h_last)


def _ffn_layer(x3d, state, prm):
    bsz, t, _ = x3d.shape
    y, new_state = _ffn(x3d.reshape(bsz * t, D_MODEL), state, t, prm)
    return y.reshape(bsz, t, D_MODEL), new_state


def _odd_layer(x3d, prm, emit_v):
    bsz, t, _ = x3d.shape
    outs = _odd(x3d.reshape(bsz * t, D_MODEL), t, prm, emit_v)
    v = outs[1].reshape(bsz, t, C_WIDTH) if emit_v else None
    return outs[0].reshape(bsz, t, D_MODEL), v


def kernel(x_prompt, x_sample, cache_k, cache_v, cache_logf, state_rglru_conv, state_rglru_h, state_ffn_conv, w_in_e, b_f, rg_conv_w, rg_conv_b, rg_wa, rg_ba, rg_wx, rg_bx, rg_lam, w_out_e, w_in_o, sgu_g, sgu_b, sgu_w, sgu_bias, w_out_o, ln_mix_g, ln_mix_b, ln_ffn_g, ln_ffn_b, ffn_w_up, ffn_conv_w, ffn_conv_b, ffn_w_down):
    xp, xs = x_prompt, x_sample
    even_p, even_s, ffn_p, ffn_s, sgu_v = [], [], [], [], []
    for layer in range(DEPTH):
        if layer % 2 == 0:
            e = layer // 2
            prm = _even_params(w_in_e[e], b_f[e], rg_conv_w[e], rg_conv_b[e], rg_wa[e], rg_ba[e],
                               rg_wx[e], rg_bx[e], rg_lam[e], w_out_e[e],
                               ln_mix_g[layer], ln_mix_b[layer])
            xp, st_p = _even_layer(xp, None, prm)
            past = (cache_k[e], cache_v[e], cache_logf[e], state_rglru_conv[e], state_rglru_h[e])
            xs, st_s = _even_layer(xs, past, prm)
            even_p.append(st_p)
            even_s.append(st_s)
        else:
            o = layer // 2
            prm = _odd_params(w_in_o[o], sgu_g[o], sgu_b[o], sgu_w[o], sgu_bias[o], w_out_o[o],
                              ln_mix_g[layer], ln_mix_b[layer])
            xp, _ = _odd_layer(xp, prm, emit_v=False)
            xs, sv = _odd_layer(xs, prm, emit_v=True)
            sgu_v.append(sv)
        fprm = _ffn_params(ffn_w_up[layer], ffn_conv_w[layer], ffn_conv_b[layer], ffn_w_down[layer],
                           ln_ffn_g[layer], ln_ffn_b[layer])
        zero_fc = jnp.zeros((xp.shape[0], FFN_CONV - 1, 2 * D_FF), F32)
        xp, fcp = _ffn_layer(xp, zero_fc, fprm)
        xs, fcs = _ffn_layer(xs, state_ffn_conv[layer], fprm)
        ffn_p.append(fcp)
        ffn_s.append(fcs)
    stack = lambda lst, idx: jnp.stack([item[idx] for item in lst])
    return (xp, xs,
            stack(even_p, 0), stack(even_p, 1), stack(even_p, 2),
            stack(even_s, 0), stack(even_s, 1), stack(even_s, 2),
            stack(even_p, 3), stack(even_p, 4), stack(even_s, 3), stack(even_s, 4),
            jnp.stack(ffn_p), jnp.stack(ffn_s), jnp.stack(sgu_v))
```

```python
import functools

import jax
import jax.numpy as jnp
from jax import lax
from jax.experimental import pallas as pl
from jax.experimental.pallas import tpu as pltpu

F32 = jnp.float32
BF16 = jnp.bfloat16

D_MODEL = 1024
N_HEADS = 8
HEAD_DIM = 64
A_WIDTH = N_HEADS * HEAD_DIM
B_WIDTH = 512
B_BLOCKS = 8
RG_CONV = 4
RG_C = 8.0
C_WIDTH = 1024
C_GROUPS = 8
C_LEN = 128
D_FF = 2816
FFN_CONV = 3
DEPTH = 2
ALPHA = (2 * DEPTH) ** 0.25
LN_EPS = 1e-5

LANES = 128
SUBLANES = 8
HALO = SUBLANES
NEG = -1e30
ROW_BLOCK = 512
FFN_CHUNK = 256
VMEM_LIMIT = 48 * 1024 * 1024


def _layer_norm(x, g, b):
    mu = jnp.mean(x, axis=-1, keepdims=True)
    xc = x - mu
    var = jnp.mean(xc * xc, axis=-1, keepdims=True)
    return xc * lax.rsqrt(var + LN_EPS) * g + b


def _log_sigmoid(x):
    return jnp.minimum(x, 0.0) - jnp.log1p(jnp.exp(-jnp.abs(x)))


def _softplus(x):
    return jnp.maximum(x, 0.0) + jnp.log1p(jnp.exp(-jnp.abs(x)))


def _split3(x):
    hi = x.astype(BF16)
    r1 = x - hi.astype(F32)
    mid = r1.astype(BF16)
    lo = (r1 - mid.astype(F32)).astype(BF16)
    return hi, mid, lo


def _dot(a, b):
    return jnp.dot(a, b, preferred_element_type=F32)


def _dot_nt(a, b):
    return lax.dot_general(a, b, (((1,), (1,)), ((), ())), preferred_element_type=F32)


def _dot3(pieces, ones_matrix, left):
    out = None
    for p in pieces:
        t = _dot(ones_matrix, p) if left else _dot(p, ones_matrix)
        out = t if out is None else out + t
    return out


def _const_spec(shape):
    nd = len(shape)
    return pl.BlockSpec(shape, lambda *_: (0,) * nd, pipeline_mode=pl.Buffered(1))


def _params(sem):
    return pltpu.CompilerParams(dimension_semantics=sem, vmem_limit_bytes=VMEM_LIMIT)


def _plan(n_rows, seq_len):
    tm = min(ROW_BLOCK, n_rows)
    nseq = max(1, tm // seq_len)
    bps = max(1, seq_len // tm)
    assert n_rows % tm == 0 and (seq_len % tm == 0 or tm % seq_len == 0)
    return tm, nseq, bps


def _even_in_kernel(x_ref, wqkv_ref, wf_ref, wxg_ref, bf_ref, tri_ref,
                    qkv16_ref, k_ref, v_ref, logf_ref, cum_ref, xb_ref, gb_ref,
                    carry_ref, *, tm, bps, kv_t):
    i = pl.program_id(0)
    x16 = x_ref[...].astype(BF16)
    qkv = _dot(x16, wqkv_ref[...])
    qkv16_ref[:, :A_WIDTH] = (qkv[:, :A_WIDTH] * (HEAD_DIM ** -0.5)).astype(BF16)
    qkv16_ref[:, A_WIDTH:] = qkv[:, A_WIDTH:].astype(BF16)
    if kv_t:
        k_ref[0] = qkv[:, A_WIDTH:2 * A_WIDTH].T
        v_ref[0] = qkv[:, 2 * A_WIDTH:].T
    else:
        k_ref[...] = qkv[:, A_WIDTH:2 * A_WIDTH]
        v_ref[...] = qkv[:, 2 * A_WIDTH:]
    logf = _log_sigmoid(_dot(x16, wf_ref[...]) + bf_ref[...])
    logf_ref[...] = logf
    cs = _dot3(_split3(logf), tri_ref[...], left=True)
    if bps > 1:
        @pl.when(i % bps == 0)
        def _():
            carry_ref[...] = jnp.zeros_like(carry_ref)
        cs = cs + carry_ref[...]
        carry_ref[...] = cs[tm - 1:tm, :]
    cum_ref[...] = cs
    xg = _dot(x16, wxg_ref[...])
    xb_ref[...] = xg[:, :B_WIDTH]
    gb_ref[...] = xg[:, B_WIDTH:]


def _even_in(x2d, seq_len, wqkv16, wf16, wxg16, bf_pad, kv_t):
    n = x2d.shape[0]
    tm, nseq, bps = _plan(n, seq_len)
    if kv_t:
        assert nseq == 1
        kv_shape = jax.ShapeDtypeStruct((n // seq_len, A_WIDTH, seq_len), F32)
        kv_spec = pl.BlockSpec((1, A_WIDTH, tm), lambda i: (i // bps, 0, i % bps))
    else:
        kv_shape = jax.ShapeDtypeStruct((n, A_WIDTH), F32)
        kv_spec = pl.BlockSpec((tm, A_WIDTH), lambda i: (i, 0))
    r = jnp.arange(tm)
    tri = ((r[None, :] <= r[:, None]) & ((r[None, :] // seq_len) == (r[:, None] // seq_len))).astype(BF16)
    row = lambda w: pl.BlockSpec((tm, w), lambda i: (i, 0))
    out_shapes = (
        jax.ShapeDtypeStruct((n, 3 * A_WIDTH), BF16),
        kv_shape,
        kv_shape,
        jax.ShapeDtypeStruct((n, LANES), F32),
        jax.ShapeDtypeStruct((n, LANES), F32),
        jax.ShapeDtypeStruct((n, B_WIDTH), F32),
        jax.ShapeDtypeStruct((n, B_WIDTH), F32),
    )
    return pl.pallas_call(
        functools.partial(_even_in_kernel, tm=tm, bps=bps, kv_t=kv_t),
        grid=(n // tm,),
        in_specs=[row(D_MODEL), _const_spec(wqkv16.shape), _const_spec(wf16.shape),
                  _const_spec(wxg16.shape), _const_spec(bf_pad.shape), _const_spec(tri.shape)],
        out_specs=(row(3 * A_WIDTH), kv_spec, kv_spec, row(LANES), row(LANES),
                   row(B_WIDTH), row(B_WIDTH)),
        out_shape=out_shapes,
        scratch_shapes=[pltpu.VMEM((1, LANES), F32)],
        compiler_params=_params(("arbitrary",)),
        name="even_in_proj",
    )(x2d, wqkv16, wf16, wxg16, bf_pad, tri)


def _fox_prompt_kernel(q_ref, k_ref, v_ref, cq_ref, ck_ref, o_ref,
                       qm_ref, cqr_ref, m_ref, l_ref, acc_ref, *, blk):
    i = pl.program_id(1)
    lane = lax.broadcasted_iota(jnp.int32, (blk, LANES), 1)
    ngrp = blk // LANES

    for h in range(N_HEADS):
        hp, hh = divmod(h, 2)
        q = q_ref[0, :, hp * LANES:(hp + 1) * LANES]
        keep = (lane < HEAD_DIM) if hh == 0 else (lane >= HEAD_DIM)
        qm_ref[h] = jnp.where(keep, q, jnp.zeros_like(q))
        cqr_ref[h] = jnp.broadcast_to(cq_ref[0, :, h:h + 1], (blk, LANES))
    m_ref[...] = jnp.full(m_ref.shape, NEG, F32)
    l_ref[...] = jnp.zeros_like(l_ref)
    acc_ref[...] = jnp.zeros_like(acc_ref)

    def update(kg0, ngrp, diagonal):
        koff = pl.multiple_of(kg0 * LANES, LANES)
        if diagonal:
            rows = lax.broadcasted_iota(jnp.int32, (blk, LANES), 0)
            visible = [lane + g * LANES <= rows for g in range(ngrp)]
        for h in range(N_HEADS):
            hp = h // 2
            k16 = k_ref[0, pl.ds(koff, ngrp * LANES), hp * LANES:(hp + 1) * LANES]
            v16 = v_ref[0, pl.ds(koff, ngrp * LANES), hp * LANES:(hp + 1) * LANES]
            s = _dot_nt(qm_ref[h], k16)
            grp = [s[:, g * LANES:(g + 1) * LANES] - ck_ref[0, kg0 + g, h:h + 1, :] for g in range(ngrp)]
            if diagonal:
                grp = [jnp.where(vis, g, NEG) for vis, g in zip(visible, grp)]
            mc = grp[0]
            for g in grp[1:]:
                mc = jnp.maximum(mc, g)
            cqr = cqr_ref[h]
            m_old = m_ref[h]
            m_new = jnp.maximum(m_old, jnp.max(mc, axis=1, keepdims=True) + cqr)
            shift = m_new - cqr
            ps = [jnp.exp(g - shift) for g in grp]
            psum = ps[0]
            for p in ps[1:]:
                psum = psum + p
            alpha = jnp.exp(m_old - m_new)
            l_ref[h] = alpha * l_ref[h] + psum
            p16 = jnp.concatenate([p.astype(BF16) for p in ps], axis=1)
            acc_ref[h] = alpha * acc_ref[h] + _dot(p16, v16)
            m_ref[h] = m_new

    def body(j, carry):
        update(j * (2 * ngrp), 2 * ngrp, False)
        return carry

    lax.fori_loop(0, i // 2, body, 0)

    @pl.when(i % 2 == 1)
    def _():
        update((i - 1) * ngrp, ngrp, False)

    update(i * ngrp, ngrp, True)
    for hp in range(N_HEADS // 2):
        l0 = jnp.sum(l_ref[2 * hp], axis=1, keepdims=True)
        l1 = jnp.sum(l_ref[2 * hp + 1], axis=1, keepdims=True)
        o = jnp.where(lane < HEAD_DIM, acc_ref[2 * hp] / l0, acc_ref[2 * hp + 1] / l1)
        o_ref[0, :, hp * LANES:(hp + 1) * LANES] = o.astype(o_ref.dtype)


def _fox_prompt(qkv16, cum, cum_t, blk):
    bsz, t, _ = qkv16.shape
    nb = t // blk
    ng = t // LANES
    ck = cum_t.reshape(bsz, N_HEADS, ng, LANES).transpose(0, 2, 1, 3)
    return pl.pallas_call(
        functools.partial(_fox_prompt_kernel, blk=blk),
        grid=(bsz, nb),
        in_specs=[pl.BlockSpec((1, blk, A_WIDTH), lambda b, i: (b, i, 0)),
                  pl.BlockSpec((1, t, A_WIDTH), lambda b, i: (b, 0, 1)),
                  pl.BlockSpec((1, t, A_WIDTH), lambda b, i: (b, 0, 2)),
                  pl.BlockSpec((1, blk, LANES), lambda b, i: (b, i, 0)),
                  pl.BlockSpec((1, ng, N_HEADS, LANES), lambda b, i: (b, 0, 0, 0))],
        out_specs=pl.BlockSpec((1, blk, A_WIDTH), lambda b, i: (b, i, 0)),
        out_shape=jax.ShapeDtypeStruct((bsz, t, A_WIDTH), BF16),
        scratch_shapes=[pltpu.VMEM((N_HEADS, blk, LANES), BF16),
                        pltpu.VMEM((N_HEADS, blk, LANES), F32),
                        pltpu.VMEM((N_HEADS, blk, LANES), F32),
                        pltpu.VMEM((N_HEADS, blk, LANES), F32),
                        pltpu.VMEM((N_HEADS, blk, LANES), F32)],
        compiler_params=_params(("arbitrary", "arbitrary")),
        name="fox_prompt",
    )(qkv16, qkv16, qkv16, cum, ck)


def _fox_sample_kernel(qkv_ref, cq_ref, ckn_ref, lf_ref, upper_ref, kt_ref, vt_ref, o_ref,
                       ckc_ref, *, t, plen, chunk):
    lane = lax.broadcasted_iota(jnp.int32, (t, LANES), 1)
    carry = jnp.zeros((N_HEADS, 1), F32)
    for c in reversed(range(plen // chunk)):
        lf = lf_ref[0, :, c * chunk:(c + 1) * chunk]
        after = _dot3(_split3(lf), upper_ref[...], left=False)
        ckc_ref[:, c * chunk:(c + 1) * chunk] = -(after + carry)
        carry = carry + jnp.sum(lf, axis=1, keepdims=True)
    rows = lax.broadcasted_iota(jnp.int32, (t, t), 0)
    cols = lax.broadcasted_iota(jnp.int32, (t, t), 1)
    visible = cols <= rows

    def per_head(s, bias_ref_rows, mask):
        halves = []
        for hh in range(2):
            part = s[hh * t:(hh + 1) * t, :] - bias_ref_rows[hh]
            halves.append(jnp.where(visible, part, NEG) if mask else part)
        return jnp.concatenate(halves, axis=0)

    for hp in range(N_HEADS // 2):
        h0 = 2 * hp
        sl = slice(hp * LANES, (hp + 1) * LANES)
        q = qkv_ref[0, :, hp * LANES:(hp + 1) * LANES]
        zero = jnp.zeros_like(q)
        qs = jnp.concatenate([jnp.where(lane < HEAD_DIM, q, zero),
                              jnp.where(lane >= HEAD_DIM, q, zero)], axis=0)
        kn16 = qkv_ref[0, :, A_WIDTH + hp * LANES:A_WIDTH + (hp + 1) * LANES]
        vn16 = qkv_ref[0, :, 2 * A_WIDTH + hp * LANES:2 * A_WIDTH + (hp + 1) * LANES]
        cq = jnp.concatenate([cq_ref[0, :, h0:h0 + 1], cq_ref[0, :, h0 + 1:h0 + 2]], axis=0)
        kt16 = kt_ref[0, h0:h0 + 2].reshape(LANES, plen).astype(BF16)
        vt16 = vt_ref[0, h0:h0 + 2].reshape(LANES, plen).astype(BF16)
        s_new = per_head(_dot_nt(qs, kn16), [ckn_ref[0, h0:h0 + 1, :], ckn_ref[0, h0 + 1:h0 + 2, :]], True)
        s_old = per_head(_dot(qs, kt16), [ckc_ref[h0:h0 + 1, :], ckc_ref[h0 + 1:h0 + 2, :]], False)
        m = jnp.maximum(jnp.max(s_new, axis=1, keepdims=True), jnp.max(s_old, axis=1, keepdims=True)) + cq
        shift = m - cq
        p_new = jnp.exp(s_new - shift)
        p_old = jnp.exp(s_old - shift)
        l = jnp.sum(p_new, axis=1, keepdims=True) + jnp.sum(p_old, axis=1, keepdims=True)
        acc = _dot(p_new.astype(BF16), vn16) + _dot_nt(p_old.astype(BF16), vt16)
        o = acc / l
        o_ref[0, :, sl] = jnp.where(lane < HEAD_DIM, o[:t, :], o[t:, :]).astype(o_ref.dtype)


def _fox_sample(qkv16, cum, cum_t, kt, vt, lf_t, chunk=512):
    bsz, t, _ = qkv16.shape
    plen = kt.shape[-1]
    r = jnp.arange(chunk)
    upper = (r[:, None] > r[None, :]).astype(BF16)
    per_b = lambda shape: pl.BlockSpec((1,) + shape, lambda b: (b,) + (0,) * len(shape))
    return pl.pallas_call(
        functools.partial(_fox_sample_kernel, t=t, plen=plen, chunk=chunk),
        grid=(bsz,),
        in_specs=[per_b((t, 3 * A_WIDTH)), per_b((t, LANES)), per_b((N_HEADS, t)),
                  per_b((N_HEADS, plen)), _const_spec(upper.shape),
                  per_b((N_HEADS, HEAD_DIM, plen)), per_b((N_HEADS, HEAD_DIM, plen))],
        out_specs=per_b((t, A_WIDTH)),
        out_shape=jax.ShapeDtypeStruct((bsz, t, A_WIDTH), BF16),
        scratch_shapes=[pltpu.VMEM((N_HEADS, plen), F32)],
        compiler_params=_params(("arbitrary",)),
        name="fox_sample",
    )(qkv16, cum, cum_t, lf_t, upper, kt, vt)


def _even_tail_kernel(x_ref, attn_ref, xb_ref, gb_ref, cs_ref, h0_ref, cw_ref, cb_ref,
                      wa_ref, ba_ref, wx_ref, bx_ref, lam_ref, woa_ref, wob_ref, g_ref, b_ref,
                      y_ref, hl_ref, s_ref, ccarry_ref, hcarry_ref, *, tm, nseq, bps):
    i = pl.program_id(0)
    lb = tm // nseq
    xb = xb_ref[...]
    if bps > 1:
        first = (i % bps) == 0
        halos = [jnp.where(first, cs_ref[0], ccarry_ref[...])]
        hinit = jnp.where(first, h0_ref[0], hcarry_ref[...])
        ccarry_ref[...] = xb[tm - HALO:, :]
    else:
        halos = [cs_ref[s] for s in range(nseq)]
        hinit = jnp.concatenate(
            [jnp.broadcast_to(h0_ref[s], (lb, B_WIDTH)) for s in range(nseq)], axis=0)
    for s in range(nseq):
        base = s * (lb + HALO)
        s_ref[base:base + HALO, :] = halos[s]
        s_ref[base + HALO:base + HALO + lb, :] = xb[s * lb:(s + 1) * lb, :]
    pieces = []
    for s in range(nseq):
        base = s * (lb + HALO) + HALO
        xc = cb_ref[...] + cw_ref[RG_CONV - 1:RG_CONV, :] * xb[s * lb:(s + 1) * lb, :]
        for d in range(1, RG_CONV):
            xc = xc + cw_ref[RG_CONV - 1 - d:RG_CONV - d, :] * s_ref[base - d:base - d + lb, :]
        pieces.append(xc)
    xc = pieces[0] if nseq == 1 else jnp.concatenate(pieces, axis=0)
    xc16 = xc.astype(BF16)
    gate_r = jax.nn.sigmoid(_dot(xc16, wa_ref[...]) + ba_ref[...])
    gate_i = jax.nn.sigmoid(_dot(xc16, wx_ref[...]) + bx_ref[...])
    log_a = -RG_C * gate_r * _softplus(-lam_ref[...])
    a = jnp.exp(log_a)
    u = jnp.sqrt(jnp.tanh(-log_a) * (a * a + 1.0)) * (gate_i * xc)
    pos = lax.broadcasted_iota(jnp.int32, (tm, B_WIDTH), 0) % lb
    d = 1
    while d < lb:
        inside = pos >= d
        u = jnp.where(inside, a * pltpu.roll(u, d, axis=0) + u, u)
        a = jnp.where(inside, a * pltpu.roll(a, d, axis=0), a)
        d *= 2
    h = u + a * hinit
    if bps > 1:
        hcarry_ref[...] = h[tm - 1:tm, :]
    for s in range(nseq):
        hl_ref[0, s:s + 1, :] = h[(s + 1) * lb - 1:(s + 1) * lb, :]
    gated = (jax.nn.gelu(gb_ref[...]) * h).astype(BF16)
    y = _dot(attn_ref[...], woa_ref[...]) + _dot(gated, wob_ref[...])
    y_ref[...] = _layer_norm(ALPHA * x_ref[...] + y, g_ref[...], b_ref[...])


def _even_tail(x2d, attn16, xb, gb, conv_state, h0, seq_len, prm):
    n = x2d.shape[0]
    tm, nseq, bps = _plan(n, seq_len)
    nblk = n // tm
    cs = jnp.pad(conv_state, ((0, 0), (HALO - (RG_CONV - 1), 0), (0, 0)))
    h03 = h0.astype(F32)[:, None, :]
    row = lambda w: pl.BlockSpec((tm, w), lambda i: (i, 0))
    state = lambda r: pl.BlockSpec((nseq, r, B_WIDTH), lambda i: (i // bps, 0, 0))
    consts = [prm[k] for k in ("rg_cw", "rg_cb", "rg_wa", "rg_ba", "rg_wx", "rg_bx", "rg_lam",
                                "wo_a", "wo_b", "ln_g", "ln_b")]
    y, hl = pl.pallas_call(
        functools.partial(_even_tail_kernel, tm=tm, nseq=nseq, bps=bps),
        grid=(nblk,),
        in_specs=[row(D_MODEL), row(A_WIDTH), row(B_WIDTH), row(B_WIDTH), state(HALO), state(1)]
                 + [_const_spec(c.shape) for c in consts],
        out_specs=(row(D_MODEL), pl.BlockSpec((1, nseq, B_WIDTH), lambda i: (i, 0, 0))),
        out_shape=(jax.ShapeDtypeStruct((n, D_MODEL), F32),
                   jax.ShapeDtypeStruct((nblk, nseq, B_WIDTH), F32)),
        scratch_shapes=[pltpu.VMEM((nseq * (tm // nseq + HALO), B_WIDTH), F32),
                        pltpu.VMEM((HALO, B_WIDTH), F32),
                        pltpu.VMEM((1, B_WIDTH), F32)],
        compiler_params=_params(("arbitrary",)),
        name="even_tail",
    )(x2d, attn16, xb, gb, cs, h03, *consts)
    h_last = hl.reshape(n // seq_len, -1, B_WIDTH)[:, -1, :]
    return y, h_last


def _shift_rows(h, halo, d):
    rolled = pltpu.roll(h, d, axis=0)
    rowid = lax.broadcasted_iota(jnp.int32, halo.shape, 0)
    head = jnp.where(rowid < d, pltpu.roll(halo, d, axis=0), rolled[:HALO, :])
    return jnp.concatenate([head, rolled[HALO:, :]], axis=0)


def _ffn_kernel(x_ref, st_ref, wup_ref, taps_ref, wd_ref, g_ref, b_ref,
                y_ref, so_ref, a_ref, carry_ref, *, tm, nseq, bps, tn):
    i = pl.program_id(0)
    lb = tm // nseq
    x = x_ref[...]
    x16 = x.astype(BF16)
    first = (i % bps) == 0

    def conv_branch(h, cols):
        taps = taps_ref[:, cols]
        outs = []
        for s in range(nseq):
            hs = h[s * lb:(s + 1) * lb, :]
            if bps > 1:
                halo = jnp.where(first, st_ref[0, :, cols], carry_ref[:, cols])
            else:
                halo = st_ref[s, :, cols]
            so_ref[s, :, cols] = hs[lb - HALO:, :]
            y = taps[3:4, :] + taps[2:3, :] * hs
            y = y + taps[1:2, :] * _shift_rows(hs, halo, 1)
            y = y + taps[0:1, :] * _shift_rows(hs, halo, 2)
            outs.append(y)
        if bps > 1:
            carry_ref[:, cols] = h[tm - HALO:, :]
        return outs

    for c in range(D_FF // tn):
        cg = slice(c * tn, (c + 1) * tn)
        cu = slice(D_FF + c * tn, D_FF + (c + 1) * tn)
        yg = conv_branch(_dot(x16, wup_ref[:, cg]), cg)
        yu = conv_branch(_dot(x16, wup_ref[:, cu]), cu)
        for s in range(nseq):
            a_ref[s * lb:(s + 1) * lb, cg] = (jax.nn.gelu(yg[s]) * yu[s]).astype(BF16)
    y = _dot(a_ref[...], wd_ref[...])
    y_ref[...] = _layer_norm(ALPHA * x + y, g_ref[...], b_ref[...])


def _ffn(x2d, state, seq_len, prm):
    n = x2d.shape[0]
    tm, nseq, bps = _plan(n, seq_len)
    nblk = n // tm
    bsz = state.shape[0]
    st = jnp.pad(state, ((0, 0), (HALO - (FFN_CONV - 1), 0), (0, 0)))
    row = pl.BlockSpec((tm, D_MODEL), lambda i: (i, 0))
    consts = [prm[k] for k in ("w_up", "taps", "w_d", "ln_g", "ln_b")]
    y, so = pl.pallas_call(
        functools.partial(_ffn_kernel, tm=tm, nseq=nseq, bps=bps, tn=FFN_CHUNK),
        grid=(nblk,),
        in_specs=[row, pl.BlockSpec((nseq, HALO, 2 * D_FF), lambda i: (i // bps, 0, 0))]
                 + [_const_spec(c.shape) for c in consts],
        out_specs=(row, pl.BlockSpec((nseq, HALO, 2 * D_FF), lambda i: (i, 0, 0))),
        out_shape=(jax.ShapeDtypeStruct((n, D_MODEL), F32),
                   jax.ShapeDtypeStruct((nblk * nseq, HALO, 2 * D_FF), F32)),
        scratch_shapes=[pltpu.VMEM((tm, D_FF), BF16),
                        pltpu.VMEM((HALO, 2 * D_FF), F32)],
        compiler_params=_params(("arbitrary",)),
        name="conv_ffn",
    )(x2d, st, *consts)
    so = so.reshape(bsz, -1, HALO, 2 * D_FF)[:, -1]
    return y, so[:, HALO - (FFN_CONV - 1):, :]


def _odd_kernel(x_ref, wu_ref, wv_ref, sg_ref, sb_ref, ws_ref, bias_ref, wo_ref, g_ref, b_ref,
                y_ref, *rest, tm, lc, emit_v):
    gate_ref = rest[-1]
    x = x_ref[...]
    x16 = x.astype(BF16)
    u = jax.nn.gelu(_dot(x16, wu_ref[...]))
    v = _layer_norm(jax.nn.gelu(_dot(x16, wv_ref[...])), sg_ref[...], sb_ref[...])
    if emit_v:
        rest[0][...] = v
    v16 = v.astype(BF16)
    gw = C_WIDTH // C_GROUPS
    for c in range(tm // lc):
        rows = slice(c * lc, (c + 1) * lc)
        for g in range(C_GROUPS):
            cols = slice(g * gw, (g + 1) * gw)
            mixed = _dot(ws_ref[g], v16[rows, cols]) + bias_ref[:, cols]
            gate_ref[rows, cols] = (u[rows, cols] * mixed).astype(BF16)
    y = _dot(gate_ref[...], wo_ref[...])
    y_ref[...] = _layer_norm(ALPHA * x + y, g_ref[...], b_ref[...])


def _odd(x2d, seq_len, prm, emit_v):
    n = x2d.shape[0]
    tm = min(ROW_BLOCK, n)
    lc = min(seq_len, C_LEN)
    gw = C_WIDTH // C_GROUPS
    ws16 = (prm["sgu_w"][:, :lc, :lc] * jnp.tril(jnp.ones((lc, lc), F32))).astype(BF16)
    bias = jnp.repeat(prm["sgu_bias"][:, :lc].T, gw, axis=1)
    consts = [prm["w_u"], prm["w_v"], prm["sgu_g"], prm["sgu_b"], ws16, bias,
              prm["w_o"], prm["ln_g"], prm["ln_b"]]
    row = pl.BlockSpec((tm, D_MODEL), lambda i: (i, 0))
    n_out = 2 if emit_v else 1
    return pl.pallas_call(
        functools.partial(_odd_kernel, tm=tm, lc=lc, emit_v=emit_v),
        grid=(n // tm,),
        in_specs=[row] + [_const_spec(c.shape) for c in consts],
        out_specs=(row, pl.BlockSpec((tm, C_WIDTH), lambda i: (i, 0)))[:n_out],
        out_shape=(jax.ShapeDtypeStruct((n, D_MODEL), F32),
                   jax.ShapeDtypeStruct((n, C_WIDTH), F32))[:n_out],
        scratch_shapes=[pltpu.VMEM((tm, C_WIDTH), BF16)],
        compiler_params=_params(("arbitrary",)),
        name="odd_mixer",
    )(x2d, *consts)


def _row(v):
    return v.astype(F32)[None, :]


def _block_diag(w):
    nb, bd, _ = w.shape
    eye = jnp.eye(nb, dtype=w.dtype)
    return jnp.einsum("ncd,nm->ncmd", w, eye).reshape(nb * bd, nb * bd)


def _even_params(w_in, b_f, rg_conv_w, rg_conv_b, rg_wa, rg_ba, rg_wx, rg_bx, rg_lam, w_out, ln_g, ln_b):
    nf = 3 * A_WIDTH
    wf = jnp.pad(w_in[:, nf:nf + N_HEADS], ((0, 0), (0, LANES - N_HEADS)))
    return {
        "wqkv": w_in[:, :nf].astype(BF16),
        "wf": wf.astype(BF16),
        "wxg": w_in[:, nf + N_HEADS:].astype(BF16),
        "bf": jnp.pad(b_f, (0, LANES - N_HEADS))[None, :],
        "rg_cw": rg_conv_w, "rg_cb": _row(rg_conv_b),
        "rg_wa": _block_diag(rg_wa).astype(BF16), "rg_ba": _row(rg_ba),
        "rg_wx": _block_diag(rg_wx).astype(BF16), "rg_bx": _row(rg_bx),
        "rg_lam": _row(rg_lam),
        "wo_a": w_out[:A_WIDTH].astype(BF16), "wo_b": w_out[A_WIDTH:].astype(BF16),
        "ln_g": _row(ln_g), "ln_b": _row(ln_b),
    }


def _ffn_params(w_up, conv_w, conv_b, w_down, ln_g, ln_b):
    taps = jnp.concatenate([conv_w, conv_b[None, :],
                            jnp.zeros((HALO - FFN_CONV - 1, 2 * D_FF), F32)], axis=0)
    return {"w_up": w_up.astype(BF16), "taps": taps, "w_d": w_down.astype(BF16),
            "ln_g": _row(ln_g), "ln_b": _row(ln_b)}


def _odd_params(w_in, sgu_g, sgu_b, sgu_w, sgu_bias, w_out, ln_g, ln_b):
    return {
        "w_u": w_in[:, :C_WIDTH].astype(BF16), "w_v": w_in[:, C_WIDTH:].astype(BF16),
        "sgu_g": _row(sgu_g), "sgu_b": _row(sgu_b), "sgu_w": sgu_w, "sgu_bias": sgu_bias,
        "w_o": w_out.astype(BF16), "ln_g": _row(ln_g), "ln_b": _row(ln_b),
    }


def _even_layer(x3d, past, prm):
    bsz, t, _ = x3d.shape
    x2d = x3d.reshape(bsz * t, D_MODEL)
    kv_t = t >= ROW_BLOCK
    qkv16, k, v, logf, cum, xb, gb = _even_in(x2d, t, prm["wqkv"], prm["wf"], prm["wxg"], prm["bf"], kv_t)
    qkv16 = qkv16.reshape(bsz, t, 3 * A_WIDTH)
    cum3 = cum.reshape(bsz, t, LANES)
    cum_t = cum3[:, :, :N_HEADS].transpose(0, 2, 1)
    if past is None:
        attn = _fox_prompt(qkv16, cum3, cum_t, ROW_BLOCK)
        conv_s = jnp.zeros((bsz, RG_CONV - 1, B_WIDTH), F32)
        h0 = jnp.zeros((bsz, B_WIDTH), F32)
    else:
        k_c, v_c, lf_c, conv_s, h0 = past
        attn = _fox_sample(qkv16, cum3, cum_t, k_c.transpose(0, 2, 3, 1), v_c.transpose(0, 2, 3, 1),
                           lf_c.transpose(0, 2, 1))
    y, h_last = _even_tail(x2d, attn.reshape(bsz * t, A_WIDTH), xb, gb, conv_s, h0, t, prm)
    if kv_t:
        new_k = k.reshape(bsz, N_HEADS, HEAD_DIM, t).transpose(0, 3, 1, 2)
        new_v = v.reshape(bsz, N_HEADS, HEAD_DIM, t).transpose(0, 3, 1, 2)
    else:
        new_k = k.reshape(bsz, t, N_HEADS, HEAD_DIM)
        new_v = v.reshape(bsz, t, N_HEADS, HEAD_DIM)
    new_logf = logf.reshape(bsz, t, LANES)[:, :, :N_HEADS]
    xb3 = xb.reshape(bsz, t, B_WIDTH)
    conv_new = jnp.concatenate([conv_s, xb3], axis=1)[:, -(RG_CONV - 1):] if t < RG_CONV - 1 \
        else xb3[:, -(RG_CONV - 1):]
    return y.reshape(bsz, t, D_MODEL), (new_k, new_v, new_logf, conv_new, h_last)


def _ffn_layer(x3d, state, prm):
    bsz, t, _ = x3d.shape
    y, new_state = _ffn(x3d.reshape(bsz * t, D_MODEL), state, t, prm)
    return y.reshape(bsz, t, D_MODEL), new_state


def _odd_layer(x3d, prm, emit_v):
    bsz, t, _ = x3d.shape
    outs = _odd(x3d.reshape(bsz * t, D_MODEL), t, prm, emit_v)
    v = outs[1].reshape(bsz, t, C_WIDTH) if emit_v else None
    return outs[0].reshape(bsz, t, D_MODEL), v


def kernel(x_prompt, x_sample, cache_k, cache_v, cache_logf, state_rglru_conv, state_rglru_h, state_ffn_conv, w_in_e, b_f, rg_conv_w, rg_conv_b, rg_wa, rg_ba, rg_wx, rg_bx, rg_lam, w_out_e, w_in_o, sgu_g, sgu_b, sgu_w, sgu_bias, w_out_o, ln_mix_g, ln_mix_b, ln_ffn_g, ln_ffn_b, ffn_w_up, ffn_conv_w, ffn_conv_b, ffn_w_down):
    xp, xs = x_prompt, x_sample
    even_p, even_s, ffn_p, ffn_s, sgu_v = [], [], [], [], []
    for layer in range(DEPTH):
        if layer % 2 == 0:
            e = layer // 2
            prm = _even_params(w_in_e[e], b_f[e], rg_conv_w[e], rg_conv_b[e], rg_wa[e], rg_ba[e],
                               rg_wx[e], rg_bx[e], rg_lam[e], w_out_e[e],
                               ln_mix_g[layer], ln_mix_b[layer])
            xp, st_p = _even_layer(xp, None, prm)
            past = (cache_k[e], cache_v[e], cache_logf[e], state_rglru_conv[e], state_rglru_h[e])
            xs, st_s = _even_layer(xs, past, prm)
            even_p.append(st_p)
            even_s.append(st_s)
        else:
            o = layer // 2
            prm = _odd_params(w_in_o[o], sgu_g[o], sgu_b[o], sgu_w[o], sgu_bias[o], w_out_o[o],
                              ln_mix_g[layer], ln_mix_b[layer])
            xp, _ = _odd_layer(xp, prm, emit_v=False)
            xs, sv = _odd_layer(xs, prm, emit_v=True)
            sgu_v.append(sv)
        fprm = _ffn_params(ffn_w_up[layer], ffn_conv_w[layer], ffn_conv_b[layer], ffn_w_down[layer],
                           ln_ffn_g[layer], ln_ffn_b[layer])
        zero_fc = jnp.zeros((xp.shape[0], FFN_CONV - 1, 2 * D_FF), F32)
        xp, fcp = _ffn_layer(xp, zero_fc, fprm)
        xs, fcs = _ffn_layer(xs, state_ffn_conv[layer], fprm)
        ffn_p.append(fcp)
        ffn_s.append(fcs)
    stack = lambda lst, idx: jnp.stack([item[idx] for item in lst])
    return (xp, xs,
            stack(even_p, 0), stack(even_p, 1), stack(even_p, 2),
            stack(even_s, 0), stack(even_s, 1), stack(even_s, 2),
            stack(even_p, 3), stack(even_p, 4), stack(even_s, 3), stack(even_s, 4),
            jnp.stack(ffn_p), jnp.stack(ffn_s), jnp.stack(sgu_v))
```

```python
import functools

import jax
import jax.numpy as jnp
from jax import lax
from jax.experimental import pallas as pl
from jax.experimental.pallas import tpu as pltpu

F32 = jnp.float32
BF16 = jnp.bfloat16

D_MODEL = 1024
N_HEADS = 8
HEAD_DIM = 64
A_WIDTH = N_HEADS * HEAD_DIM
B_WIDTH = 512
B_BLOCKS = 8
RG_CONV = 4
RG_C = 8.0
C_WIDTH = 1024
C_GROUPS = 8
C_LEN = 128
D_FF = 2816
FFN_CONV = 3
DEPTH = 2
ALPHA = (2 * DEPTH) ** 0.25
LN_EPS = 1e-5

LANES = 128
SUBLANES = 8
HALO = SUBLANES
NEG = -1e30
LOG2E = 1.4426950408889634
ROW_BLOCK = 512
FFN_CHUNK = 256
VMEM_LIMIT = 48 * 1024 * 1024


def _layer_norm(x, g, b):
    mu = jnp.mean(x, axis=-1, keepdims=True)
    xc = x - mu
    var = jnp.mean(xc * xc, axis=-1, keepdims=True)
    return xc * lax.rsqrt(var + LN_EPS) * g + b


def _log_sigmoid(x):
    return jnp.minimum(x, 0.0) - jnp.log1p(jnp.exp(-jnp.abs(x)))


def _softplus(x):
    return jnp.maximum(x, 0.0) + jnp.log1p(jnp.exp(-jnp.abs(x)))


def _split3(x):
    hi = x.astype(BF16)
    r1 = x - hi.astype(F32)
    mid = r1.astype(BF16)
    lo = (r1 - mid.astype(F32)).astype(BF16)
    return hi, mid, lo


def _dot(a, b):
    return jnp.dot(a, b, preferred_element_type=F32)


def _dot_nt(a, b):
    return lax.dot_general(a, b, (((1,), (1,)), ((), ())), preferred_element_type=F32)


def _dot3(pieces, ones_matrix, left):
    out = None
    for p in pieces:
        t = _dot(ones_matrix, p) if left else _dot(p, ones_matrix)
        out = t if out is None else out + t
    return out


def _const_spec(shape):
    nd = len(shape)
    return pl.BlockSpec(shape, lambda *_: (0,) * nd, pipeline_mode=pl.Buffered(1))


def _params(sem):
    return pltpu.CompilerParams(dimension_semantics=sem, vmem_limit_bytes=VMEM_LIMIT)


def _plan(n_rows, seq_len):
    tm = min(ROW_BLOCK, n_rows)
    nseq = max(1, tm // seq_len)
    bps = max(1, seq_len // tm)
    assert n_rows % tm == 0 and (seq_len % tm == 0 or tm % seq_len == 0)
    return tm, nseq, bps


def _even_in_kernel(x_ref, wqkv_ref, wf_ref, wxg_ref, bf_ref, tri_ref,
                    qkv16_ref, k_ref, v_ref, logf_ref, cum_ref, xb_ref, gb_ref,
                    carry_ref, *, tm, bps, kv_t):
    i = pl.program_id(0)
    x16 = x_ref[...].astype(BF16)
    qkv = _dot(x16, wqkv_ref[...])
    qkv16_ref[:, :A_WIDTH] = (qkv[:, :A_WIDTH] * (HEAD_DIM ** -0.5 * LOG2E)).astype(BF16)
    qkv16_ref[:, A_WIDTH:] = qkv[:, A_WIDTH:].astype(BF16)
    if kv_t:
        k_ref[0] = qkv[:, A_WIDTH:2 * A_WIDTH].T
        v_ref[0] = qkv[:, 2 * A_WIDTH:].T
    else:
        k_ref[...] = qkv[:, A_WIDTH:2 * A_WIDTH]
        v_ref[...] = qkv[:, 2 * A_WIDTH:]
    logf = _log_sigmoid(_dot(x16, wf_ref[...]) + bf_ref[...])
    logf_ref[...] = logf
    cs = _dot3(_split3(logf), tri_ref[...], left=True)
    if bps > 1:
        @pl.when(i % bps == 0)
        def _():
            carry_ref[...] = jnp.zeros_like(carry_ref)
        cs = cs + carry_ref[...]
        carry_ref[...] = cs[tm - 1:tm, :]
    cum_ref[...] = cs
    xg = _dot(x16, wxg_ref[...])
    xb_ref[...] = xg[:, :B_WIDTH]
    gb_ref[...] = xg[:, B_WIDTH:]


def _even_in(x2d, seq_len, wqkv16, wf16, wxg16, bf_pad, kv_t):
    n = x2d.shape[0]
    tm, nseq, bps = _plan(n, seq_len)
    if kv_t:
        assert nseq == 1
        kv_shape = jax.ShapeDtypeStruct((n // seq_len, A_WIDTH, seq_len), F32)
        kv_spec = pl.BlockSpec((1, A_WIDTH, tm), lambda i: (i // bps, 0, i % bps))
    else:
        kv_shape = jax.ShapeDtypeStruct((n, A_WIDTH), F32)
        kv_spec = pl.BlockSpec((tm, A_WIDTH), lambda i: (i, 0))
    r = jnp.arange(tm)
    tri = ((r[None, :] <= r[:, None]) & ((r[None, :] // seq_len) == (r[:, None] // seq_len))).astype(BF16)
    row = lambda w: pl.BlockSpec((tm, w), lambda i: (i, 0))
    out_shapes = (
        jax.ShapeDtypeStruct((n, 3 * A_WIDTH), BF16),
        kv_shape,
        kv_shape,
        jax.ShapeDtypeStruct((n, LANES), F32),
        jax.ShapeDtypeStruct((n, LANES), F32),
        jax.ShapeDtypeStruct((n, B_WIDTH), F32),
        jax.ShapeDtypeStruct((n, B_WIDTH), F32),
    )
    return pl.pallas_call(
        functools.partial(_even_in_kernel, tm=tm, bps=bps, kv_t=kv_t),
        grid=(n // tm,),
        in_specs=[row(D_MODEL), _const_spec(wqkv16.shape), _const_spec(wf16.shape),
                  _const_spec(wxg16.shape), _const_spec(bf_pad.shape), _const_spec(tri.shape)],
        out_specs=(row(3 * A_WIDTH), kv_spec, kv_spec, row(LANES), row(LANES),
                   row(B_WIDTH), row(B_WIDTH)),
        out_shape=out_shapes,
        scratch_shapes=[pltpu.VMEM((1, LANES), F32)],
        compiler_params=_params(("arbitrary",)),
        name="even_in_proj",
    )(x2d, wqkv16, wf16, wxg16, bf_pad, tri)


def _fox_prompt_kernel(q_ref, k_ref, v_ref, cq_ref, ck_ref, o_ref,
                       qm_ref, cqr_ref, m_ref, acc_ref, *, blk):
    i = pl.program_id(1)
    lane = lax.broadcasted_iota(jnp.int32, (blk, LANES), 1)
    ngrp = blk // LANES

    for h in range(N_HEADS):
        hp, hh = divmod(h, 2)
        q = q_ref[0, :, hp * LANES:(hp + 1) * LANES]
        keep = (lane < HEAD_DIM) if hh == 0 else (lane >= HEAD_DIM)
        qm_ref[h] = jnp.where(keep, q, jnp.zeros_like(q))
        cqr_ref[h] = jnp.broadcast_to(cq_ref[0, :, h:h + 1] * LOG2E, (blk, LANES))
    m_ref[...] = jnp.full(m_ref.shape, NEG, F32)
    acc_ref[...] = jnp.zeros_like(acc_ref)

    def update(kg0, ngrp, diagonal):
        koff = pl.multiple_of(kg0 * LANES, LANES)
        vlane = lax.broadcasted_iota(jnp.int32, (ngrp * LANES, LANES), 1)
        if diagonal:
            rows = lax.broadcasted_iota(jnp.int32, (blk, LANES), 0)
            visible = [lane + g * LANES <= rows for g in range(ngrp)]
        for h in range(N_HEADS):
            hp, hh = divmod(h, 2)
            k16 = k_ref[0, pl.ds(koff, ngrp * LANES), hp * LANES:(hp + 1) * LANES]
            v16 = v_ref[0, pl.ds(koff, ngrp * LANES), hp * LANES:(hp + 1) * LANES]
            vkeep = (vlane < HEAD_DIM) if hh == 0 else (vlane >= HEAD_DIM)
            v16 = jnp.where(vkeep, v16, jnp.ones_like(v16))
            s = _dot_nt(qm_ref[h], k16)
            grp = [s[:, g * LANES:(g + 1) * LANES] - ck_ref[0, kg0 + g, h:h + 1, :] * LOG2E
                   for g in range(ngrp)]
            if diagonal:
                grp = [jnp.where(vis, g, NEG) for vis, g in zip(visible, grp)]
            mc = grp[0]
            for g in grp[1:]:
                mc = jnp.maximum(mc, g)
            cqr = cqr_ref[h]
            m_old = m_ref[h]
            m_new = jnp.maximum(m_old, jnp.max(mc, axis=1, keepdims=True) + cqr)
            shift = m_new - cqr
            p16 = jnp.concatenate([jnp.exp2(g - shift).astype(BF16) for g in grp], axis=1)
            acc_ref[h] = jnp.exp2(m_old - m_new) * acc_ref[h] + _dot(p16, v16)
            m_ref[h] = m_new

    def body(j, carry):
        update(j * (2 * ngrp), 2 * ngrp, False)
        return carry

    lax.fori_loop(0, i // 2, body, 0)

    @pl.when(i % 2 == 1)
    def _():
        update((i - 1) * ngrp, ngrp, False)

    update(i * ngrp, ngrp, True)
    for hp in range(N_HEADS // 2):
        a0 = acc_ref[2 * hp]
        a1 = acc_ref[2 * hp + 1]
        o = jnp.where(lane < HEAD_DIM, a0 / a0[:, HEAD_DIM:HEAD_DIM + 1], a1 / a1[:, 0:1])
        o_ref[0, :, hp * LANES:(hp + 1) * LANES] = o.astype(o_ref.dtype)


def _fox_prompt(qkv16, cum, cum_t, blk):
    bsz, t, _ = qkv16.shape
    nb = t // blk
    ng = t // LANES
    ck = cum_t.reshape(bsz, N_HEADS, ng, LANES).transpose(0, 2, 1, 3)
    return pl.pallas_call(
        functools.partial(_fox_prompt_kernel, blk=blk),
        grid=(bsz, nb),
        in_specs=[pl.BlockSpec((1, blk, A_WIDTH), lambda b, i: (b, i, 0)),
                  pl.BlockSpec((1, t, A_WIDTH), lambda b, i: (b, 0, 1)),
                  pl.BlockSpec((1, t, A_WIDTH), lambda b, i: (b, 0, 2)),
                  pl.BlockSpec((1, blk, LANES), lambda b, i: (b, i, 0)),
                  pl.BlockSpec((1, ng, N_HEADS, LANES), lambda b, i: (b, 0, 0, 0))],
        out_specs=pl.BlockSpec((1, blk, A_WIDTH), lambda b, i: (b, i, 0)),
        out_shape=jax.ShapeDtypeStruct((bsz, t, A_WIDTH), BF16),
        scratch_shapes=[pltpu.VMEM((N_HEADS, blk, LANES), BF16),
                        pltpu.VMEM((N_HEADS, blk, LANES), F32),
                        pltpu.VMEM((N_HEADS, blk, LANES), F32),
                        pltpu.VMEM((N_HEADS, blk, LANES), F32)],
        compiler_params=_params(("arbitrary", "arbitrary")),
        name="fox_prompt",
    )(qkv16, qkv16, qkv16, cum, ck)


def _fox_sample_kernel(qkv_ref, cq_ref, ckn_ref, lf_ref, upper_ref, kt_ref, vt_ref, o_ref,
                       ckc_ref, *, t, plen, chunk):
    lane = lax.broadcasted_iota(jnp.int32, (t, LANES), 1)
    carry = jnp.zeros((N_HEADS, 1), F32)
    for c in reversed(range(plen // chunk)):
        lf = lf_ref[0, :, c * chunk:(c + 1) * chunk]
        after = _dot3(_split3(lf), upper_ref[...], left=False)
        ckc_ref[:, c * chunk:(c + 1) * chunk] = -(after + carry)
        carry = carry + jnp.sum(lf, axis=1, keepdims=True)
    rows = lax.broadcasted_iota(jnp.int32, (t, t), 0)
    cols = lax.broadcasted_iota(jnp.int32, (t, t), 1)
    visible = cols <= rows

    def per_head(s, bias_ref_rows, mask):
        halves = []
        for hh in range(2):
            part = s[hh * t:(hh + 1) * t, :] - bias_ref_rows[hh] * LOG2E
            halves.append(jnp.where(visible, part, NEG) if mask else part)
        return jnp.concatenate(halves, axis=0)

    for hp in range(N_HEADS // 2):
        h0 = 2 * hp
        sl = slice(hp * LANES, (hp + 1) * LANES)
        q = qkv_ref[0, :, hp * LANES:(hp + 1) * LANES]
        zero = jnp.zeros_like(q)
        qs = jnp.concatenate([jnp.where(lane < HEAD_DIM, q, zero),
                              jnp.where(lane >= HEAD_DIM, q, zero)], axis=0)
        kn16 = qkv_ref[0, :, A_WIDTH + hp * LANES:A_WIDTH + (hp + 1) * LANES]
        vn16 = qkv_ref[0, :, 2 * A_WIDTH + hp * LANES:2 * A_WIDTH + (hp + 1) * LANES]
        cq = jnp.concatenate([cq_ref[0, :, h0:h0 + 1], cq_ref[0, :, h0 + 1:h0 + 2]], axis=0) * LOG2E
        kt16 = kt_ref[0, h0:h0 + 2].reshape(LANES, plen).astype(BF16)
        vt16 = vt_ref[0, h0:h0 + 2].reshape(LANES, plen).astype(BF16)
        s_new = per_head(_dot_nt(qs, kn16), [ckn_ref[0, h0:h0 + 1, :], ckn_ref[0, h0 + 1:h0 + 2, :]], True)
        s_old = per_head(_dot(qs, kt16), [ckc_ref[h0:h0 + 1, :], ckc_ref[h0 + 1:h0 + 2, :]], False)
        m = jnp.maximum(jnp.max(s_new, axis=1, keepdims=True), jnp.max(s_old, axis=1, keepdims=True)) + cq
        shift = m - cq
        p_new = jnp.exp2(s_new - shift)
        p_old = jnp.exp2(s_old - shift)
        l = jnp.sum(p_new, axis=1, keepdims=True) + jnp.sum(p_old, axis=1, keepdims=True)
        acc = _dot(p_new.astype(BF16), vn16) + _dot_nt(p_old.astype(BF16), vt16)
        o = acc / l
        o_ref[0, :, sl] = jnp.where(lane < HEAD_DIM, o[:t, :], o[t:, :]).astype(o_ref.dtype)


def _fox_sample(qkv16, cum, cum_t, kt, vt, lf_t, chunk=512):
    bsz, t, _ = qkv16.shape
    plen = kt.shape[-1]
    r = jnp.arange(chunk)
    upper = (r[:, None] > r[None, :]).astype(BF16)
    per_b = lambda shape: pl.BlockSpec((1,) + shape, lambda b: (b,) + (0,) * len(shape))
    return pl.pallas_call(
        functools.partial(_fox_sample_kernel, t=t, plen=plen, chunk=chunk),
        grid=(bsz,),
        in_specs=[per_b((t, 3 * A_WIDTH)), per_b((t, LANES)), per_b((N_HEADS, t)),
                  per_b((N_HEADS, plen)), _const_spec(upper.shape),
                  per_b((N_HEADS, HEAD_DIM, plen)), per_b((N_HEADS, HEAD_DIM, plen))],
        out_specs=per_b((t, A_WIDTH)),
        out_shape=jax.ShapeDtypeStruct((bsz, t, A_WIDTH), BF16),
        scratch_shapes=[pltpu.VMEM((N_HEADS, plen), F32)],
        compiler_params=_params(("arbitrary",)),
        name="fox_sample",
    )(qkv16, cum, cum_t, lf_t, upper, kt, vt)


def _shift_rows(h, halo, d):
    rolled = pltpu.roll(h, d, axis=0)
    rowid = lax.broadcasted_iota(jnp.int32, halo.shape, 0)
    head = jnp.where(rowid < d, pltpu.roll(halo, d, axis=0), rolled[:HALO, :])
    return jnp.concatenate([head, rolled[HALO:, :]], axis=0)


def _linear_scan(a, u, inits, lb):
    rows, width = a.shape
    ngroups = rows // SUBLANES
    a3 = a.reshape(ngroups, SUBLANES, width)
    u3 = u.reshape(ngroups, SUBLANES, width)
    sub = lax.broadcasted_iota(jnp.int32, a3.shape, 1)
    d = 1
    while d < SUBLANES:
        inside = sub >= d
        u3 = jnp.where(inside, a3 * pltpu.roll(u3, d, axis=1) + u3, u3)
        a3 = jnp.where(inside, a3 * pltpu.roll(a3, d, axis=1), a3)
        d *= 2
    out = []
    carry = None
    for g in range(ngroups):
        if g % (lb // SUBLANES) == 0:
            carry = inits[g // (lb // SUBLANES)]
        hg = u3[g] + a3[g] * carry
        out.append(hg)
        carry = hg[SUBLANES - 1:, :]
    return jnp.concatenate(out, axis=0)


def _even_tail_kernel(x_ref, attn_ref, xb_ref, gb_ref, cs_ref, h0_ref, cw_ref, cb_ref,
                      wa_ref, ba_ref, wx_ref, bx_ref, lam_ref, woa_ref, wob_ref, g_ref, b_ref,
                      y_ref, hl_ref, ccarry_ref, hcarry_ref, *, tm, nseq, bps):
    i = pl.program_id(0)
    lb = tm // nseq
    xb = xb_ref[...]
    if bps > 1:
        first = (i % bps) == 0
        halos = [jnp.where(first, cs_ref[0], ccarry_ref[...])]
        inits = [jnp.where(first, h0_ref[0], hcarry_ref[...])]
        ccarry_ref[...] = xb[tm - HALO:, :]
    else:
        halos = [cs_ref[s] for s in range(nseq)]
        inits = [h0_ref[s] for s in range(nseq)]
    pieces = []
    for s in range(nseq):
        xs = xb[s * lb:(s + 1) * lb, :]
        xc = cb_ref[...] + cw_ref[RG_CONV - 1:RG_CONV, :] * xs
        for d in range(1, RG_CONV):
            xc = xc + cw_ref[RG_CONV - 1 - d:RG_CONV - d, :] * _shift_rows(xs, halos[s], d)
        pieces.append(xc)
    xc = pieces[0] if nseq == 1 else jnp.concatenate(pieces, axis=0)
    xc16 = xc.astype(BF16)
    gate_r = jax.nn.sigmoid(_dot(xc16, wa_ref[...]) + ba_ref[...])
    gate_i = jax.nn.sigmoid(_dot(xc16, wx_ref[...]) + bx_ref[...])
    log_a = -RG_C * gate_r * _softplus(-lam_ref[...])
    a = jnp.exp(log_a)
    u = jnp.sqrt(jnp.tanh(-log_a) * (a * a + 1.0)) * (gate_i * xc)
    h = _linear_scan(a, u, inits, lb)
    if bps > 1:
        hcarry_ref[...] = h[tm - 1:tm, :]
    for s in range(nseq):
        hl_ref[0, s:s + 1, :] = h[(s + 1) * lb - 1:(s + 1) * lb, :]
    gated = (jax.nn.gelu(gb_ref[...]) * h).astype(BF16)
    y = _dot(attn_ref[...], woa_ref[...]) + _dot(gated, wob_ref[...])
    y_ref[...] = _layer_norm(ALPHA * x_ref[...] + y, g_ref[...], b_ref[...])


def _even_tail(x2d, attn16, xb, gb, conv_state, h0, seq_len, prm):
    n = x2d.shape[0]
    tm, nseq, bps = _plan(n, seq_len)
    nblk = n // tm
    cs = jnp.pad(conv_state, ((0, 0), (HALO - (RG_CONV - 1), 0), (0, 0)))
    h03 = h0.astype(F32)[:, None, :]
    row = lambda w: pl.BlockSpec((tm, w), lambda i: (i, 0))
    state = lambda r: pl.BlockSpec((nseq, r, B_WIDTH), lambda i: (i // bps, 0, 0))
    consts = [prm[k] for k in ("rg_cw", "rg_cb", "rg_wa", "rg_ba", "rg_wx", "rg_bx", "rg_lam",
                                "wo_a", "wo_b", "ln_g", "ln_b")]
    y, hl = pl.pallas_call(
        functools.partial(_even_tail_kernel, tm=tm, nseq=nseq, bps=bps),
        grid=(nblk,),
        in_specs=[row(D_MODEL), row(A_WIDTH), row(B_WIDTH), row(B_WIDTH), state(HALO), state(1)]
                 + [_const_spec(c.shape) for c in consts],
        out_specs=(row(D_MODEL), pl.BlockSpec((1, nseq, B_WIDTH), lambda i: (i, 0, 0))),
        out_shape=(jax.ShapeDtypeStruct((n, D_MODEL), F32),
                   jax.ShapeDtypeStruct((nblk, nseq, B_WIDTH), F32)),
        scratch_shapes=[pltpu.VMEM((HALO, B_WIDTH), F32),
                        pltpu.VMEM((1, B_WIDTH), F32)],
        compiler_params=_params(("arbitrary",)),
        name="even_tail",
    )(x2d, attn16, xb, gb, cs, h03, *consts)
    h_last = hl.reshape(n // seq_len, -1, B_WIDTH)[:, -1, :]
    return y, h_last


def _ffn_kernel(x_ref, st_ref, wup_ref, taps_ref, wd_ref, g_ref, b_ref,
                y_ref, so_ref, a_ref, carry_ref, *, tm, nseq, bps, tn):
    i = pl.program_id(0)
    lb = tm // nseq
    x = x_ref[...]
    x16 = x.astype(BF16)
    first = (i % bps) == 0

    def conv_branch(h, cols):
        taps = taps_ref[:, cols]
        outs = []
        for s in range(nseq):
            hs = h[s * lb:(s + 1) * lb, :]
            if bps > 1:
                halo = jnp.where(first, st_ref[0, :, cols], carry_ref[:, cols])
            else:
                halo = st_ref[s, :, cols]
            so_ref[s, :, cols] = hs[lb - HALO:, :]
            y = taps[3:4, :] + taps[2:3, :] * hs
            y = y + taps[1:2, :] * _shift_rows(hs, halo, 1)
            y = y + taps[0:1, :] * _shift_rows(hs, halo, 2)
            outs.append(y)
        if bps > 1:
            carry_ref[:, cols] = h[tm - HALO:, :]
        return outs

    for c in range(D_FF // tn):
        cg = slice(c * tn, (c + 1) * tn)
        cu = slice(D_FF + c * tn, D_FF + (c + 1) * tn)
        yg = conv_branch(_dot(x16, wup_ref[:, cg]), cg)
        yu = conv_branch(_dot(x16, wup_ref[:, cu]), cu)
        for s in range(nseq):
            a_ref[s * lb:(s + 1) * lb, cg] = (jax.nn.gelu(yg[s]) * yu[s]).astype(BF16)
    y = _dot(a_ref[...], wd_ref[...])
    y_ref[...] = _layer_norm(ALPHA * x + y, g_ref[...], b_ref[...])


def _ffn(x2d, state, seq_len, prm):
    n = x2d.shape[0]
    tm, nseq, bps = _plan(n, seq_len)
    nblk = n // tm
    bsz = state.shape[0]
    st = jnp.pad(state, ((0, 0), (HALO - (FFN_CONV - 1), 0), (0, 0)))
    row = pl.BlockSpec((tm, D_MODEL), lambda i: (i, 0))
    consts = [prm[k] for k in ("w_up", "taps", "w_d", "ln_g", "ln_b")]
    y, so = pl.pallas_call(
        functools.partial(_ffn_kernel, tm=tm, nseq=nseq, bps=bps, tn=FFN_CHUNK),
        grid=(nblk,),
        in_specs=[row, pl.BlockSpec((nseq, HALO, 2 * D_FF), lambda i: (i // bps, 0, 0))]
                 + [_const_spec(c.shape) for c in consts],
        out_specs=(row, pl.BlockSpec((nseq, HALO, 2 * D_FF), lambda i: (i, 0, 0))),
        out_shape=(jax.ShapeDtypeStruct((n, D_MODEL), F32),
                   jax.ShapeDtypeStruct((nblk * nseq, HALO, 2 * D_FF), F32)),
        scratch_shapes=[pltpu.VMEM((tm, D_FF), BF16),
                        pltpu.VMEM((HALO, 2 * D_FF), F32)],
        compiler_params=_params(("arbitrary",)),
        name="conv_ffn",
    )(x2d, st, *consts)
    so = so.reshape(bsz, -1, HALO, 2 * D_FF)[:, -1]
    return y, so[:, HALO - (FFN_CONV - 1):, :]


def _odd_kernel(x_ref, wu_ref, wv_ref, sg_ref, sb_ref, ws_ref, bias_ref, wo_ref, g_ref, b_ref,
                y_ref, *rest, tm, lc, emit_v):
    gate_ref = rest[-1]
    x = x_ref[...]
    x16 = x.astype(BF16)
    u = jax.nn.gelu(_dot(x16, wu_ref[...]))
    v = _layer_norm(jax.nn.gelu(_dot(x16, wv_ref[...])), sg_ref[...], sb_ref[...])
    if emit_v:
        rest[0][...] = v
    v16 = v.astype(BF16)
    gw = C_WIDTH // C_GROUPS
    for c in range(tm // lc):
        rows = slice(c * lc, (c + 1) * lc)
        for g in range(C_GROUPS):
            cols = slice(g * gw, (g + 1) * gw)
            mixed = _dot(ws_ref[g], v16[rows, cols]) + bias_ref[:, cols]
            gate_ref[rows, cols] = (u[rows, cols] * mixed).astype(BF16)
    y = _dot(gate_ref[...], wo_ref[...])
    y_ref[...] = _layer_norm(ALPHA * x + y, g_ref[...], b_ref[...])


def _odd(x2d, seq_len, prm, emit_v):
    n = x2d.shape[0]
    tm = min(ROW_BLOCK, n)
    lc = min(seq_len, C_LEN)
    gw = C_WIDTH // C_GROUPS
    ws16 = (prm["sgu_w"][:, :lc, :lc] * jnp.tril(jnp.ones((lc, lc), F32))).astype(BF16)
    bias = jnp.repeat(prm["sgu_bias"][:, :lc].T, gw, axis=1)
    consts = [prm["w_u"], prm["w_v"], prm["sgu_g"], prm["sgu_b"], ws16, bias,
              prm["w_o"], prm["ln_g"], prm["ln_b"]]
    row = pl.BlockSpec((tm, D_MODEL), lambda i: (i, 0))
    n_out = 2 if emit_v else 1
    return pl.pallas_call(
        functools.partial(_odd_kernel, tm=tm, lc=lc, emit_v=emit_v),
        grid=(n // tm,),
        in_specs=[row] + [_const_spec(c.shape) for c in consts],
        out_specs=(row, pl.BlockSpec((tm, C_WIDTH), lambda i: (i, 0)))[:n_out],
        out_shape=(jax.ShapeDtypeStruct((n, D_MODEL), F32),
                   jax.ShapeDtypeStruct((n, C_WIDTH), F32))[:n_out],
        scratch_shapes=[pltpu.VMEM((tm, C_WIDTH), BF16)],
        compiler_params=_params(("arbitrary",)),
        name="odd_mixer",
    )(x2d, *consts)


def _row(v):
    return v.astype(F32)[None, :]


def _block_diag(w):
    nb, bd, _ = w.shape
    eye = jnp.eye(nb, dtype=w.dtype)
    return jnp.einsum("ncd,nm->ncmd", w, eye).reshape(nb * bd, nb * bd)


def _even_params(w_in, b_f, rg_conv_w, rg_conv_b, rg_wa, rg_ba, rg_wx, rg_bx, rg_lam, w_out, ln_g, ln_b):
    nf = 3 * A_WIDTH
    wf = jnp.pad(w_in[:, nf:nf + N_HEADS], ((0, 0), (0, LANES - N_HEADS)))
    return {
        "wqkv": w_in[:, :nf].astype(BF16),
        "wf": wf.astype(BF16),
        "wxg": w_in[:, nf + N_HEADS:].astype(BF16),
        "bf": jnp.pad(b_f, (0, LANES - N_HEADS))[None, :],
        "rg_cw": rg_conv_w, "rg_cb": _row(rg_conv_b),
        "rg_wa": _block_diag(rg_wa).astype(BF16), "rg_ba": _row(rg_ba),
        "rg_wx": _block_diag(rg_wx).astype(BF16), "rg_bx": _row(rg_bx),
        "rg_lam": _row(rg_lam),
        "wo_a": w_out[:A_WIDTH].astype(BF16), "wo_b": w_out[A_WIDTH:].astype(BF16),
        "ln_g": _row(ln_g), "ln_b": _row(ln_b),
    }


def _ffn_params(w_up, conv_w, conv_b, w_down, ln_g, ln_b):
    taps = jnp.concatenate([conv_w, conv_b[None, :],
                            jnp.zeros((HALO - FFN_CONV - 1, 2 * D_FF), F32)], axis=0)
    return {"w_up": w_up.astype(BF16), "taps": taps, "w_d": w_down.astype(BF16),
            "ln_g": _row(ln_g), "ln_b": _row(ln_b)}


def _odd_params(w_in, sgu_g, sgu_b, sgu_w, sgu_bias, w_out, ln_g, ln_b):
    return {
        "w_u": w_in[:, :C_WIDTH].astype(BF16), "w_v": w_in[:, C_WIDTH:].astype(BF16),
        "sgu_g": _row(sgu_g), "sgu_b": _row(sgu_b), "sgu_w": sgu_w, "sgu_bias": sgu_bias,
        "w_o": w_out.astype(BF16), "ln_g": _row(ln_g), "ln_b": _row(ln_b),
    }


def _even_layer(x3d, past, prm):
    bsz, t, _ = x3d.shape
    x2d = x3d.reshape(bsz * t, D_MODEL)
    kv_t = t >= ROW_BLOCK
    qkv16, k, v, logf, cum, xb, gb = _even_in(x2d, t, prm["wqkv"], prm["wf"], prm["wxg"], prm["bf"], kv_t)
    qkv16 = qkv16.reshape(bsz, t, 3 * A_WIDTH)
    cum3 = cum.reshape(bsz, t, LANES)
    cum_t = cum3[:, :, :N_HEADS].transpose(0, 2, 1)
    if past is None:
        attn = _fox_prompt(qkv16, cum3, cum_t, ROW_BLOCK)
        conv_s = jnp.zeros((bsz, RG_CONV - 1, B_WIDTH), F32)
        h0 = jnp.zeros((bsz, B_WIDTH), F32)
    else:
        k_c, v_c, lf_c, conv_s, h0 = past
        attn = _fox_sample(qkv16, cum3, cum_t, k_c.transpose(0, 2, 3, 1), v_c.transpose(0, 2, 3, 1),
                           lf_c.transpose(0, 2, 1))
    y, h_last = _even_tail(x2d, attn.reshape(bsz * t, A_WIDTH), xb, gb, conv_s, h0, t, prm)
    if kv_t:
        new_k = k.reshape(bsz, N_HEADS, HEAD_DIM, t).transpose(0, 3, 1, 2)
        new_v = v.reshape(bsz, N_HEADS, HEAD_DIM, t).transpose(0, 3, 1, 2)
    else:
        new_k = k.reshape(bsz, t, N_HEADS, HEAD_DIM)
        new_v = v.reshape(bsz, t, N_HEADS, HEAD_DIM)
    new_logf = logf.reshape(bsz, t, LANES)[:, :, :N_HEADS]
    xb3 = xb.reshape(bsz, t, B_WIDTH)
    conv_new = jnp.concatenate([conv_s, xb3], axis=1)[:, -(RG_CONV - 1):] if t < RG_CONV - 1 \
        else xb3[:, -(RG_CONV - 1):]
    return y.reshape(bsz, t, D_MODEL), (new_k, new_v, new_logf, conv_new, h_last)


def _ffn_layer(x3d, state, prm):
    bsz, t, _ = x3d.shape
    y, new_state = _ffn(x3d.reshape(bsz * t, D_MODEL), state, t, prm)
    return y.reshape(bsz, t, D_MODEL), new_state


def _odd_layer(x3d, prm, emit_v):
    bsz, t, _ = x3d.shape
    outs = _odd(x3d.reshape(bsz * t, D_MODEL), t, prm, emit_v)
    v = outs[1].reshape(bsz, t, C_WIDTH) if emit_v else None
    return outs[0].reshape(bsz, t, D_MODEL), v


def kernel(x_prompt, x_sample, cache_k, cache_v, cache_logf, state_rglru_conv, state_rglru_h, state_ffn_conv, w_in_e, b_f, rg_conv_w, rg_conv_b, rg_wa, rg_ba, rg_wx, rg_bx, rg_lam, w_out_e, w_in_o, sgu_g, sgu_b, sgu_w, sgu_bias, w_out_o, ln_mix_g, ln_mix_b, ln_ffn_g, ln_ffn_b, ffn_w_up, ffn_conv_w, ffn_conv_b, ffn_w_down):
    xp, xs = x_prompt, x_sample
    even_p, even_s, ffn_p, ffn_s, sgu_v = [], [], [], [], []
    for layer in range(DEPTH):
        if layer % 2 == 0:
            e = layer // 2
            prm = _even_params(w_in_e[e], b_f[e], rg_conv_w[e], rg_conv_b[e], rg_wa[e], rg_ba[e],
                               rg_wx[e], rg_bx[e], rg_lam[e], w_out_e[e],
                               ln_mix_g[layer], ln_mix_b[layer])
            xp, st_p = _even_layer(xp, None, prm)
            past = (cache_k[e], cache_v[e], cache_logf[e], state_rglru_conv[e], state_rglru_h[e])
            xs, st_s = _even_layer(xs, past, prm)
            even_p.append(st_p)
            even_s.append(st_s)
        else:
            o = layer // 2
            prm = _odd_params(w_in_o[o], sgu_g[o], sgu_b[o], sgu_w[o], sgu_bias[o], w_out_o[o],
                              ln_mix_g[layer], ln_mix_b[layer])
            xp, _ = _odd_layer(xp, prm, emit_v=False)
            xs, sv = _odd_layer(xs, prm, emit_v=True)
            sgu_v.append(sv)
        fprm = _ffn_params(ffn_w_up[layer], ffn_conv_w[layer], ffn_conv_b[layer], ffn_w_down[layer],
                           ln_ffn_g[layer], ln_ffn_b[layer])
        zero_fc = jnp.zeros((xp.shape[0], FFN_CONV - 1, 2 * D_FF), F32)
        xp, fcp = _ffn_layer(xp, zero_fc, fprm)
        xs, fcs = _ffn_layer(xs, state_ffn_conv[layer], fprm)
        ffn_p.append(fcp)
        ffn_s.append(fcs)
    stack = lambda lst, idx: jnp.stack([item[idx] for item in lst])
    return (xp, xs,
            stack(even_p, 0), stack(even_p, 1), stack(even_p, 2),
            stack(even_s, 0), stack(even_s, 1), stack(even_s, 2),
            stack(even_p, 3), stack(even_p, 4), stack(even_s, 3), stack(even_s, 4),
            jnp.stack(ffn_p), jnp.stack(ffn_s), jnp.stack(sgu_v))
```

```python
import functools

import jax
import jax.numpy as jnp
from jax import lax
from jax.experimental import pallas as pl
from jax.experimental.pallas import tpu as pltpu

F32 = jnp.float32
BF16 = jnp.bfloat16

D_MODEL = 1024
N_HEADS = 8
HEAD_DIM = 64
A_WIDTH = N_HEADS * HEAD_DIM
B_WIDTH = 512
B_BLOCKS = 8
RG_CONV = 4
RG_C = 8.0
C_WIDTH = 1024
C_GROUPS = 8
C_LEN = 128
D_FF = 2816
FFN_CONV = 3
DEPTH = 2
ALPHA = (2 * DEPTH) ** 0.25
LN_EPS = 1e-5

LANES = 128
SUBLANES = 8
HALO = SUBLANES
NEG = -1e30
LOG2E = 1.4426950408889634
ROW_BLOCK = 512
FFN_CHUNK = 256
VMEM_LIMIT = 48 * 1024 * 1024


def _layer_norm(x, g, b):
    mu = jnp.mean(x, axis=-1, keepdims=True)
    xc = x - mu
    var = jnp.mean(xc * xc, axis=-1, keepdims=True)
    return xc * lax.rsqrt(var + LN_EPS) * g + b


def _log_sigmoid(x):
    return jnp.minimum(x, 0.0) - jnp.log1p(jnp.exp(-jnp.abs(x)))


def _softplus(x):
    return jnp.maximum(x, 0.0) + jnp.log1p(jnp.exp(-jnp.abs(x)))


def _split3(x):
    hi = x.astype(BF16)
    r1 = x - hi.astype(F32)
    mid = r1.astype(BF16)
    lo = (r1 - mid.astype(F32)).astype(BF16)
    return hi, mid, lo


def _dot(a, b):
    return jnp.dot(a, b, preferred_element_type=F32)


def _dot_nt(a, b):
    return lax.dot_general(a, b, (((1,), (1,)), ((), ())), preferred_element_type=F32)


def _dot3(pieces, ones_matrix, left):
    out = None
    for p in pieces:
        t = _dot(ones_matrix, p) if left else _dot(p, ones_matrix)
        out = t if out is None else out + t
    return out


def _const_spec(shape):
    nd = len(shape)
    return pl.BlockSpec(shape, lambda *_: (0,) * nd, pipeline_mode=pl.Buffered(1))


def _params(sem, vmem_limit=VMEM_LIMIT):
    return pltpu.CompilerParams(dimension_semantics=sem, vmem_limit_bytes=vmem_limit)


def _plan(n_rows, seq_len, row_block=ROW_BLOCK):
    tm = min(row_block, n_rows)
    nseq = max(1, tm // seq_len)
    bps = max(1, seq_len // tm)
    assert n_rows % tm == 0 and (seq_len % tm == 0 or tm % seq_len == 0)
    return tm, nseq, bps


def _even_in_kernel(x_ref, wqkv_ref, wf_ref, wxg_ref, bf_ref, tri_ref, cs_ref, h0_ref,
                    cw_ref, cb_ref, wa_ref, ba_ref, wx_ref, bx_ref, lam_ref,
                    qkv16_ref, k_ref, v_ref, logf_ref, cum_ref, gated_ref, xtail_ref, hl_ref,
                    carry_ref, ccarry_ref, hcarry_ref, *, tm, nseq, bps, kv_t):
    i = pl.program_id(0)
    lb = tm // nseq
    first = (i % bps) == 0
    x16 = x_ref[...].astype(BF16)
    xg = _dot(x16, wxg_ref[...])
    half = B_WIDTH // 2
    for cols in (slice(0, half), slice(half, B_WIDTH)):
        xb = xg[:, cols]
        if bps > 1:
            halos = [jnp.where(first, cs_ref[0, :, cols], ccarry_ref[:, cols])]
            inits = [jnp.where(first, h0_ref[0, :, cols], hcarry_ref[:, cols])]
            ccarry_ref[:, cols] = xb[tm - HALO:, :]
        else:
            halos = [cs_ref[s, :, cols] for s in range(nseq)]
            inits = [h0_ref[s, :, cols] for s in range(nseq)]
        pieces = []
        for s in range(nseq):
            xs = xb[s * lb:(s + 1) * lb, :]
            xtail_ref[s, :, cols] = xs[lb - HALO:, :]
            xc = cb_ref[:, cols] + cw_ref[RG_CONV - 1:RG_CONV, cols] * xs
            for d in range(1, RG_CONV):
                xc = xc + cw_ref[RG_CONV - 1 - d:RG_CONV - d, cols] * _shift_rows(xs, halos[s], d)
            pieces.append(xc)
        xc = pieces[0] if nseq == 1 else jnp.concatenate(pieces, axis=0)
        xc16 = xc.astype(BF16)
        gate_r = jax.nn.sigmoid(_dot(xc16, wa_ref[cols, cols]) + ba_ref[:, cols])
        gate_i = jax.nn.sigmoid(_dot(xc16, wx_ref[cols, cols]) + bx_ref[:, cols])
        log_a = -RG_C * gate_r * _softplus(-lam_ref[:, cols])
        a = jnp.exp(log_a)
        u = jnp.sqrt(jnp.tanh(-log_a) * (a * a + 1.0)) * (gate_i * xc)
        h = _linear_scan(a, u, inits, lb)
        if bps > 1:
            hcarry_ref[:, cols] = h[tm - 1:tm, :]
        for s in range(nseq):
            hl_ref[0, s:s + 1, cols] = h[(s + 1) * lb - 1:(s + 1) * lb, :]
        gb = xg[:, B_WIDTH + cols.start:B_WIDTH + cols.stop]
        gated_ref[:, cols] = (jax.nn.gelu(gb) * h).astype(BF16)

    qkv = _dot(x16, wqkv_ref[...])
    qkv16_ref[:, :A_WIDTH] = (qkv[:, :A_WIDTH] * (HEAD_DIM ** -0.5 * LOG2E)).astype(BF16)
    qkv16_ref[:, A_WIDTH:] = qkv[:, A_WIDTH:].astype(BF16)
    if kv_t:
        k_ref[0] = qkv[:, A_WIDTH:2 * A_WIDTH].T
        v_ref[0] = qkv[:, 2 * A_WIDTH:].T
    else:
        k_ref[...] = qkv[:, A_WIDTH:2 * A_WIDTH]
        v_ref[...] = qkv[:, 2 * A_WIDTH:]
    logf = _log_sigmoid(_dot(x16, wf_ref[...]) + bf_ref[...])
    logf_ref[...] = logf
    cs = _dot3(_split3(logf), tri_ref[...], left=True)
    if bps > 1:
        cs = cs + jnp.where(first, 0.0, carry_ref[...])
        carry_ref[...] = cs[tm - 1:tm, :]
    cum_ref[...] = cs


def _even_in(x2d, seq_len, conv_state, h0, prm, kv_t):
    n = x2d.shape[0]
    tm, nseq, bps = _plan(n, seq_len)
    nblk = n // tm
    cs = jnp.pad(conv_state, ((0, 0), (HALO - (RG_CONV - 1), 0), (0, 0)))
    h03 = h0.astype(F32)[:, None, :]
    state = lambda r: pl.BlockSpec((nseq, r, B_WIDTH), lambda i: (i // bps, 0, 0))
    consts = [prm[k] for k in ("wqkv", "wf", "wxg", "bf")]
    rg = [prm[k] for k in ("rg_cw", "rg_cb", "rg_wa", "rg_ba", "rg_wx", "rg_bx", "rg_lam")]
    if kv_t:
        assert nseq == 1
        kv_shape = jax.ShapeDtypeStruct((n // seq_len, A_WIDTH, seq_len), F32)
        kv_spec = pl.BlockSpec((1, A_WIDTH, tm), lambda i: (i // bps, 0, i % bps))
    else:
        kv_shape = jax.ShapeDtypeStruct((n, A_WIDTH), F32)
        kv_spec = pl.BlockSpec((tm, A_WIDTH), lambda i: (i, 0))
    r = jnp.arange(tm)
    tri = ((r[None, :] <= r[:, None]) & ((r[None, :] // seq_len) == (r[:, None] // seq_len))).astype(BF16)
    row = lambda w: pl.BlockSpec((tm, w), lambda i: (i, 0))
    out_shapes = (
        jax.ShapeDtypeStruct((n, 3 * A_WIDTH), BF16),
        kv_shape,
        kv_shape,
        jax.ShapeDtypeStruct((n, LANES), F32),
        jax.ShapeDtypeStruct((n, LANES), F32),
        jax.ShapeDtypeStruct((n, B_WIDTH), BF16),
        jax.ShapeDtypeStruct((nblk * nseq, HALO, B_WIDTH), F32),
        jax.ShapeDtypeStruct((nblk, nseq, B_WIDTH), F32),
    )
    return pl.pallas_call(
        functools.partial(_even_in_kernel, tm=tm, nseq=nseq, bps=bps, kv_t=kv_t),
        grid=(nblk,),
        in_specs=[row(D_MODEL)] + [_const_spec(c.shape) for c in consts] + [_const_spec(tri.shape)]
                 + [state(HALO), state(1)] + [_const_spec(c.shape) for c in rg],
        out_specs=(row(3 * A_WIDTH), kv_spec, kv_spec, row(LANES), row(LANES), row(B_WIDTH),
                   pl.BlockSpec((nseq, HALO, B_WIDTH), lambda i: (i, 0, 0)),
                   pl.BlockSpec((1, nseq, B_WIDTH), lambda i: (i, 0, 0))),
        out_shape=out_shapes,
        scratch_shapes=[pltpu.VMEM((1, LANES), F32),
                        pltpu.VMEM((HALO, B_WIDTH), F32),
                        pltpu.VMEM((1, B_WIDTH), F32)],
        compiler_params=_params(("arbitrary",)),
        name="even_in_proj",
    )(x2d, *consts, tri, cs, h03, *rg)


def _fox_prompt_kernel(q_ref, k_ref, v_ref, cq_ref, ck_ref, o_ref,
                       qm_ref, cqr_ref, m_ref, acc_ref, *, blk):
    i = pl.program_id(1)
    lane = lax.broadcasted_iota(jnp.int32, (blk, LANES), 1)
    ngrp = blk // LANES

    for h in range(N_HEADS):
        hp, hh = divmod(h, 2)
        q = q_ref[0, :, hp * LANES:(hp + 1) * LANES]
        keep = (lane < HEAD_DIM) if hh == 0 else (lane >= HEAD_DIM)
        qm_ref[h] = jnp.where(keep, q, jnp.zeros_like(q))
        cqr_ref[h] = jnp.broadcast_to(cq_ref[0, :, h:h + 1] * LOG2E, (blk, LANES))
    m_ref[...] = jnp.full(m_ref.shape, NEG, F32)
    acc_ref[...] = jnp.zeros_like(acc_ref)

    def update(kg0, ngrp, diagonal):
        koff = pl.multiple_of(kg0 * LANES, LANES)
        vlane = lax.broadcasted_iota(jnp.int32, (ngrp * LANES, LANES), 1)
        if diagonal:
            rows = lax.broadcasted_iota(jnp.int32, (blk, LANES), 0)
            visible = [lane + g * LANES <= rows for g in range(ngrp)]
        for h in range(N_HEADS):
            hp, hh = divmod(h, 2)
            k16 = k_ref[0, pl.ds(koff, ngrp * LANES), hp * LANES:(hp + 1) * LANES]
            v16 = v_ref[0, pl.ds(koff, ngrp * LANES), hp * LANES:(hp + 1) * LANES]
            vkeep = (vlane < HEAD_DIM) if hh == 0 else (vlane >= HEAD_DIM)
            v16 = jnp.where(vkeep, v16, jnp.ones_like(v16))
            s = _dot_nt(qm_ref[h], k16)
            grp = [s[:, g * LANES:(g + 1) * LANES] - ck_ref[0, kg0 + g, h:h + 1, :] * LOG2E
                   for g in range(ngrp)]
            if diagonal:
                grp = [jnp.where(vis, g, NEG) for vis, g in zip(visible, grp)]
            mc = grp[0]
            for g in grp[1:]:
                mc = jnp.maximum(mc, g)
            cqr = cqr_ref[h]
            m_old = m_ref[h]
            m_new = jnp.maximum(m_old, jnp.max(mc, axis=1, keepdims=True) + cqr)
            shift = m_new - cqr
            p16 = jnp.concatenate([jnp.exp2(g - shift).astype(BF16) for g in grp], axis=1)
            acc_ref[h] = jnp.exp2(m_old - m_new) * acc_ref[h] + _dot(p16, v16)
            m_ref[h] = m_new

    def body(j, carry):
        update(j * (2 * ngrp), 2 * ngrp, False)
        return carry

    lax.fori_loop(0, i // 2, body, 0)

    @pl.when(i % 2 == 1)
    def _():
        update((i - 1) * ngrp, ngrp, False)

    update(i * ngrp, ngrp, True)
    for hp in range(N_HEADS // 2):
        a0 = acc_ref[2 * hp]
        a1 = acc_ref[2 * hp + 1]
        o = jnp.where(lane < HEAD_DIM, a0 / a0[:, HEAD_DIM:HEAD_DIM + 1], a1 / a1[:, 0:1])
        o_ref[0, :, hp * LANES:(hp + 1) * LANES] = o.astype(o_ref.dtype)


def _fox_prompt(qkv16, cum, cum_t, blk):
    bsz, t, _ = qkv16.shape
    nb = t // blk
    ng = t // LANES
    ck = cum_t.reshape(bsz, N_HEADS, ng, LANES).transpose(0, 2, 1, 3)
    return pl.pallas_call(
        functools.partial(_fox_prompt_kernel, blk=blk),
        grid=(bsz, nb),
        in_specs=[pl.BlockSpec((1, blk, A_WIDTH), lambda b, i: (b, i, 0)),
                  pl.BlockSpec((1, t, A_WIDTH), lambda b, i: (b, 0, 1)),
                  pl.BlockSpec((1, t, A_WIDTH), lambda b, i: (b, 0, 2)),
                  pl.BlockSpec((1, blk, LANES), lambda b, i: (b, i, 0)),
                  pl.BlockSpec((1, ng, N_HEADS, LANES), lambda b, i: (b, 0, 0, 0))],
        out_specs=pl.BlockSpec((1, blk, A_WIDTH), lambda b, i: (b, i, 0)),
        out_shape=jax.ShapeDtypeStruct((bsz, t, A_WIDTH), BF16),
        scratch_shapes=[pltpu.VMEM((N_HEADS, blk, LANES), BF16),
                        pltpu.VMEM((N_HEADS, blk, LANES), F32),
                        pltpu.VMEM((N_HEADS, blk, LANES), F32),
                        pltpu.VMEM((N_HEADS, blk, LANES), F32)],
        compiler_params=_params(("arbitrary", "arbitrary")),
        name="fox_prompt",
    )(qkv16, qkv16, qkv16, cum, ck)


def _fox_sample_kernel(qkv_ref, cq_ref, ckn_ref, lf_ref, upper_ref, kt_ref, vt_ref, o_ref,
                       ckc_ref, *, t, plen, chunk):
    lane = lax.broadcasted_iota(jnp.int32, (t, LANES), 1)
    carry = jnp.zeros((N_HEADS, 1), F32)
    for c in reversed(range(plen // chunk)):
        lf = lf_ref[0, :, c * chunk:(c + 1) * chunk]
        after = _dot3(_split3(lf), upper_ref[...], left=False)
        ckc_ref[:, c * chunk:(c + 1) * chunk] = -(after + carry)
        carry = carry + jnp.sum(lf, axis=1, keepdims=True)
    rows = lax.broadcasted_iota(jnp.int32, (t, t), 0)
    cols = lax.broadcasted_iota(jnp.int32, (t, t), 1)
    visible = cols <= rows

    def per_head(s, bias_ref_rows, mask):
        halves = []
        for hh in range(2):
            part = s[hh * t:(hh + 1) * t, :] - bias_ref_rows[hh] * LOG2E
            halves.append(jnp.where(visible, part, NEG) if mask else part)
        return jnp.concatenate(halves, axis=0)

    for hp in range(N_HEADS // 2):
        h0 = 2 * hp
        sl = slice(hp * LANES, (hp + 1) * LANES)
        q = qkv_ref[0, :, hp * LANES:(hp + 1) * LANES]
        zero = jnp.zeros_like(q)
        qs = jnp.concatenate([jnp.where(lane < HEAD_DIM, q, zero),
                              jnp.where(lane >= HEAD_DIM, q, zero)], axis=0)
        kn16 = qkv_ref[0, :, A_WIDTH + hp * LANES:A_WIDTH + (hp + 1) * LANES]
        vn16 = qkv_ref[0, :, 2 * A_WIDTH + hp * LANES:2 * A_WIDTH + (hp + 1) * LANES]
        cq = jnp.concatenate([cq_ref[0, :, h0:h0 + 1], cq_ref[0, :, h0 + 1:h0 + 2]], axis=0) * LOG2E
        kt16 = kt_ref[0, h0:h0 + 2].reshape(LANES, plen).astype(BF16)
        vt16 = vt_ref[0, h0:h0 + 2].reshape(LANES, plen).astype(BF16)
        s_new = per_head(_dot_nt(qs, kn16), [ckn_ref[0, h0:h0 + 1, :], ckn_ref[0, h0 + 1:h0 + 2, :]], True)
        s_old = per_head(_dot(qs, kt16), [ckc_ref[h0:h0 + 1, :], ckc_ref[h0 + 1:h0 + 2, :]], False)
        m = jnp.maximum(jnp.max(s_new, axis=1, keepdims=True), jnp.max(s_old, axis=1, keepdims=True)) + cq
        shift = m - cq
        p_new = jnp.exp2(s_new - shift)
        p_old = jnp.exp2(s_old - shift)
        l = jnp.sum(p_new, axis=1, keepdims=True) + jnp.sum(p_old, axis=1, keepdims=True)
        acc = _dot(p_new.astype(BF16), vn16) + _dot_nt(p_old.astype(BF16), vt16)
        o = acc / l
        o_ref[0, :, sl] = jnp.where(lane < HEAD_DIM, o[:t, :], o[t:, :]).astype(o_ref.dtype)


def _fox_sample(qkv16, cum, cum_t, kt, vt, lf_t, chunk=512):
    bsz, t, _ = qkv16.shape
    plen = kt.shape[-1]
    r = jnp.arange(chunk)
    upper = (r[:, None] > r[None, :]).astype(BF16)
    per_b = lambda shape: pl.BlockSpec((1,) + shape, lambda b: (b,) + (0,) * len(shape))
    return pl.pallas_call(
        functools.partial(_fox_sample_kernel, t=t, plen=plen, chunk=chunk),
        grid=(bsz,),
        in_specs=[per_b((t, 3 * A_WIDTH)), per_b((t, LANES)), per_b((N_HEADS, t)),
                  per_b((N_HEADS, plen)), _const_spec(upper.shape),
                  per_b((N_HEADS, HEAD_DIM, plen)), per_b((N_HEADS, HEAD_DIM, plen))],
        out_specs=per_b((t, A_WIDTH)),
        out_shape=jax.ShapeDtypeStruct((bsz, t, A_WIDTH), BF16),
        scratch_shapes=[pltpu.VMEM((N_HEADS, plen), F32)],
        compiler_params=_params(("arbitrary",)),
        name="fox_sample",
    )(qkv16, cum, cum_t, lf_t, upper, kt, vt)


def _shift_rows(h, halo, d):
    rolled = pltpu.roll(h, d, axis=0)
    rowid = lax.broadcasted_iota(jnp.int32, halo.shape, 0)
    head = jnp.where(rowid < d, pltpu.roll(halo, d, axis=0), rolled[:HALO, :])
    return jnp.concatenate([head, rolled[HALO:, :]], axis=0)


def _linear_scan(a, u, inits, lb):
    rows, width = a.shape
    ngroups = rows // SUBLANES
    a3 = a.reshape(ngroups, SUBLANES, width)
    u3 = u.reshape(ngroups, SUBLANES, width)
    sub = lax.broadcasted_iota(jnp.int32, a3.shape, 1)
    d = 1
    while d < SUBLANES:
        inside = sub >= d
        u3 = jnp.where(inside, a3 * pltpu.roll(u3, d, axis=1) + u3, u3)
        a3 = jnp.where(inside, a3 * pltpu.roll(a3, d, axis=1), a3)
        d *= 2
    out = []
    carry = None
    for g in range(ngroups):
        if g % (lb // SUBLANES) == 0:
            carry = inits[g // (lb // SUBLANES)]
        hg = u3[g] + a3[g] * carry
        out.append(hg)
        carry = hg[SUBLANES - 1:, :]
    return jnp.concatenate(out, axis=0)


def _even_tail_kernel(x_ref, attn_ref, gated_ref, woa_ref, wob_ref, g_ref, b_ref, y_ref):
    y = _dot(attn_ref[...], woa_ref[...]) + _dot(gated_ref[...], wob_ref[...])
    y_ref[...] = _layer_norm(ALPHA * x_ref[...] + y, g_ref[...], b_ref[...])


def _even_tail(x2d, attn16, gated16, prm):
    n = x2d.shape[0]
    tm = min(ROW_BLOCK, n)
    row = lambda w: pl.BlockSpec((tm, w), lambda i: (i, 0))
    consts = [prm[k] for k in ("wo_a", "wo_b", "ln_g", "ln_b")]
    return pl.pallas_call(
        _even_tail_kernel,
        grid=(n // tm,),
        in_specs=[row(D_MODEL), row(A_WIDTH), row(B_WIDTH)] + [_const_spec(c.shape) for c in consts],
        out_specs=row(D_MODEL),
        out_shape=jax.ShapeDtypeStruct((n, D_MODEL), F32),
        compiler_params=_params(("arbitrary",)),
        name="even_tail",
    )(x2d, attn16, gated16, *consts)


def _ffn_kernel(x_ref, st_ref, wup_ref, taps_ref, wd_ref, g_ref, b_ref,
                y_ref, so_ref, a_ref, carry_ref, *, tm, nseq, bps, tn):
    i = pl.program_id(0)
    lb = tm // nseq
    x = x_ref[...]
    x16 = x.astype(BF16)
    first = (i % bps) == 0

    def conv_branch(h, cols):
        taps = taps_ref[:, cols]
        outs = []
        for s in range(nseq):
            hs = h[s * lb:(s + 1) * lb, :]
            if bps > 1:
                halo = jnp.where(first, st_ref[0, :, cols], carry_ref[:, cols])
            else:
                halo = st_ref[s, :, cols]
            so_ref[s, :, cols] = hs[lb - HALO:, :]
            y = taps[3:4, :] + taps[2:3, :] * hs
            y = y + taps[1:2, :] * _shift_rows(hs, halo, 1)
            y = y + taps[0:1, :] * _shift_rows(hs, halo, 2)
            outs.append(y)
        if bps > 1:
            carry_ref[:, cols] = h[tm - HALO:, :]
        return outs

    for c in range(D_FF // tn):
        cg = slice(c * tn, (c + 1) * tn)
        cu = slice(D_FF + c * tn, D_FF + (c + 1) * tn)
        yg = conv_branch(_dot(x16, wup_ref[:, cg]), cg)
        yu = conv_branch(_dot(x16, wup_ref[:, cu]), cu)
        for s in range(nseq):
            a_ref[s * lb:(s + 1) * lb, cg] = (jax.nn.gelu(yg[s]) * yu[s]).astype(BF16)
    y = _dot(a_ref[...], wd_ref[...])
    y_ref[...] = _layer_norm(ALPHA * x + y, g_ref[...], b_ref[...])


def _ffn(x2d, state, seq_len, prm):
    n = x2d.shape[0]
    tm, nseq, bps = _plan(n, seq_len)
    nblk = n // tm
    bsz = state.shape[0]
    st = jnp.pad(state, ((0, 0), (HALO - (FFN_CONV - 1), 0), (0, 0)))
    row = pl.BlockSpec((tm, D_MODEL), lambda i: (i, 0))
    consts = [prm[k] for k in ("w_up", "taps", "w_d", "ln_g", "ln_b")]
    y, so = pl.pallas_call(
        functools.partial(_ffn_kernel, tm=tm, nseq=nseq, bps=bps, tn=FFN_CHUNK),
        grid=(nblk,),
        in_specs=[row, pl.BlockSpec((nseq, HALO, 2 * D_FF), lambda i: (i // bps, 0, 0))]
                 + [_const_spec(c.shape) for c in consts],
        out_specs=(row, pl.BlockSpec((nseq, HALO, 2 * D_FF), lambda i: (i, 0, 0))),
        out_shape=(jax.ShapeDtypeStruct((n, D_MODEL), F32),
                   jax.ShapeDtypeStruct((nblk * nseq, HALO, 2 * D_FF), F32)),
        scratch_shapes=[pltpu.VMEM((tm, D_FF), BF16),
                        pltpu.VMEM((HALO, 2 * D_FF), F32)],
        compiler_params=_params(("arbitrary",)),
        name="conv_ffn",
    )(x2d, st, *consts)
    so = so.reshape(bsz, -1, HALO, 2 * D_FF)[:, -1]
    return y, so[:, HALO - (FFN_CONV - 1):, :]


def _odd_kernel(x_ref, wu_ref, wv_ref, sg_ref, sb_ref, ws_ref, bias_ref, wo_ref, g_ref, b_ref,
                y_ref, *rest, tm, lc, emit_v):
    gate_ref = rest[-1]
    x = x_ref[...]
    x16 = x.astype(BF16)
    u = jax.nn.gelu(_dot(x16, wu_ref[...]))
    v = _layer_norm(jax.nn.gelu(_dot(x16, wv_ref[...])), sg_ref[...], sb_ref[...])
    if emit_v:
        rest[0][...] = v
    v16 = v.astype(BF16)
    gw = C_WIDTH // C_GROUPS
    for c in range(tm // lc):
        rows = slice(c * lc, (c + 1) * lc)
        for g in range(C_GROUPS):
            cols = slice(g * gw, (g + 1) * gw)
            mixed = _dot(ws_ref[g], v16[rows, cols]) + bias_ref[:, cols]
            gate_ref[rows, cols] = (u[rows, cols] * mixed).astype(BF16)
    y = _dot(gate_ref[...], wo_ref[...])
    y_ref[...] = _layer_norm(ALPHA * x + y, g_ref[...], b_ref[...])


def _odd(x2d, seq_len, prm, emit_v):
    n = x2d.shape[0]
    tm = min(ROW_BLOCK, n)
    lc = min(seq_len, C_LEN)
    gw = C_WIDTH // C_GROUPS
    ws16 = (prm["sgu_w"][:, :lc, :lc] * jnp.tril(jnp.ones((lc, lc), F32))).astype(BF16)
    bias = jnp.repeat(prm["sgu_bias"][:, :lc].T, gw, axis=1)
    consts = [prm["w_u"], prm["w_v"], prm["sgu_g"], prm["sgu_b"], ws16, bias,
              prm["w_o"], prm["ln_g"], prm["ln_b"]]
    row = pl.BlockSpec((tm, D_MODEL), lambda i: (i, 0))
    n_out = 2 if emit_v else 1
    return pl.pallas_call(
        functools.partial(_odd_kernel, tm=tm, lc=lc, emit_v=emit_v),
        grid=(n // tm,),
        in_specs=[row] + [_const_spec(c.shape) for c in consts],
        out_specs=(row, pl.BlockSpec((tm, C_WIDTH), lambda i: (i, 0)))[:n_out],
        out_shape=(jax.ShapeDtypeStruct((n, D_MODEL), F32),
                   jax.ShapeDtypeStruct((n, C_WIDTH), F32))[:n_out],
        scratch_shapes=[pltpu.VMEM((tm, C_WIDTH), BF16)],
        compiler_params=_params(("arbitrary",)),
        name="odd_mixer",
    )(x2d, *consts)


def _row(v):
    return v.astype(F32)[None, :]


def _block_diag(w):
    nb, bd, _ = w.shape
    eye = jnp.eye(nb, dtype=w.dtype)
    return jnp.einsum("ncd,nm->ncmd", w, eye).reshape(nb * bd, nb * bd)


def _even_params(w_in, b_f, rg_conv_w, rg_conv_b, rg_wa, rg_ba, rg_wx, rg_bx, rg_lam, w_out, ln_g, ln_b):
    nf = 3 * A_WIDTH
    wf = jnp.pad(w_in[:, nf:nf + N_HEADS], ((0, 0), (0, LANES - N_HEADS)))
    return {
        "wqkv": w_in[:, :nf].astype(BF16),
        "wf": wf.astype(BF16),
        "wxg": w_in[:, nf + N_HEADS:].astype(BF16),
        "bf": jnp.pad(b_f, (0, LANES - N_HEADS))[None, :],
        "rg_cw": rg_conv_w, "rg_cb": _row(rg_conv_b),
        "rg_wa": _block_diag(rg_wa).astype(BF16), "rg_ba": _row(rg_ba),
        "rg_wx": _block_diag(rg_wx).astype(BF16), "rg_bx": _row(rg_bx),
        "rg_lam": _row(rg_lam),
        "wo_a": w_out[:A_WIDTH].astype(BF16), "wo_b": w_out[A_WIDTH:].astype(BF16),
        "ln_g": _row(ln_g), "ln_b": _row(ln_b),
    }


def _ffn_params(w_up, conv_w, conv_b, w_down, ln_g, ln_b):
    taps = jnp.concatenate([conv_w, conv_b[None, :],
                            jnp.zeros((HALO - FFN_CONV - 1, 2 * D_FF), F32)], axis=0)
    return {"w_up": w_up.astype(BF16), "taps": taps, "w_d": w_down.astype(BF16),
            "ln_g": _row(ln_g), "ln_b": _row(ln_b)}


def _odd_params(w_in, sgu_g, sgu_b, sgu_w, sgu_bias, w_out, ln_g, ln_b):
    return {
        "w_u": w_in[:, :C_WIDTH].astype(BF16), "w_v": w_in[:, C_WIDTH:].astype(BF16),
        "sgu_g": _row(sgu_g), "sgu_b": _row(sgu_b), "sgu_w": sgu_w, "sgu_bias": sgu_bias,
        "w_o": w_out.astype(BF16), "ln_g": _row(ln_g), "ln_b": _row(ln_b),
    }


def _even_layer(x3d, past, prm):
    bsz, t, _ = x3d.shape
    x2d = x3d.reshape(bsz * t, D_MODEL)
    assert t >= HALO
    if past is None:
        conv_s = jnp.zeros((bsz, RG_CONV - 1, B_WIDTH), F32)
        h0 = jnp.zeros((bsz, B_WIDTH), F32)
    else:
        k_c, v_c, lf_c, conv_s, h0 = past
    kv_t = t >= ROW_BLOCK
    qkv16, k, v, logf, cum, gated16, xtail, hl = _even_in(x2d, t, conv_s, h0, prm, kv_t)
    qkv16 = qkv16.reshape(bsz, t, 3 * A_WIDTH)
    cum3 = cum.reshape(bsz, t, LANES)
    cum_t = cum3[:, :, :N_HEADS].transpose(0, 2, 1)
    if past is None:
        attn = _fox_prompt(qkv16, cum3, cum_t, ROW_BLOCK)
    else:
        attn = _fox_sample(qkv16, cum3, cum_t, k_c.transpose(0, 2, 3, 1), v_c.transpose(0, 2, 3, 1),
                           lf_c.transpose(0, 2, 1))
    y = _even_tail(x2d, attn.reshape(bsz * t, A_WIDTH), gated16, prm)
    if kv_t:
        new_k = k.reshape(bsz, N_HEADS, HEAD_DIM, t).transpose(0, 3, 1, 2)
        new_v = v.reshape(bsz, N_HEADS, HEAD_DIM, t).transpose(0, 3, 1, 2)
    else:
        new_k = k.reshape(bsz, t, N_HEADS, HEAD_DIM)
        new_v = v.reshape(bsz, t, N_HEADS, HEAD_DIM)
    new_logf = logf.reshape(bsz, t, LANES)[:, :, :N_HEADS]
    h_last = hl.reshape(bsz, -1, B_WIDTH)[:, -1, :]
    conv_new = xtail.reshape(bsz, -1, HALO, B_WIDTH)[:, -1, HALO - (RG_CONV - 1):, :]
    return y.reshape(bsz, t, D_MODEL), (new_k, new_v, new_logf, conv_new, h_last)


def _ffn_layer(x3d, state, prm):
    bsz, t, _ = x3d.shape
    y, new_state = _ffn(x3d.reshape(bsz * t, D_MODEL), state, t, prm)
    return y.reshape(bsz, t, D_MODEL), new_state


def _odd_layer(x3d, prm, emit_v):
    bsz, t, _ = x3d.shape
    outs = _odd(x3d.reshape(bsz * t, D_MODEL), t, prm, emit_v)
    v = outs[1].reshape(bsz, t, C_WIDTH) if emit_v else None
    return outs[0].reshape(bsz, t, D_MODEL), v


def kernel(x_prompt, x_sample, cache_k, cache_v, cache_logf, state_rglru_conv, state_rglru_h, state_ffn_conv, w_in_e, b_f, rg_conv_w, rg_conv_b, rg_wa, rg_ba, rg_wx, rg_bx, rg_lam, w_out_e, w_in_o, sgu_g, sgu_b, sgu_w, sgu_bias, w_out_o, ln_mix_g, ln_mix_b, ln_ffn_g, ln_ffn_b, ffn_w_up, ffn_conv_w, ffn_conv_b, ffn_w_down):
    xp, xs = x_prompt, x_sample
    even_p, even_s, ffn_p, ffn_s, sgu_v = [], [], [], [], []
    for layer in range(DEPTH):
        if layer % 2 == 0:
            e = layer // 2
            prm = _even_params(w_in_e[e], b_f[e], rg_conv_w[e], rg_conv_b[e], rg_wa[e], rg_ba[e],
                               rg_wx[e], rg_bx[e], rg_lam[e], w_out_e[e],
                               ln_mix_g[layer], ln_mix_b[layer])
            xp, st_p = _even_layer(xp, None, prm)
            past = (cache_k[e], cache_v[e], cache_logf[e], state_rglru_conv[e], state_rglru_h[e])
            xs, st_s = _even_layer(xs, past, prm)
            even_p.append(st_p)
            even_s.append(st_s)
        else:
            o = layer // 2
            prm = _odd_params(w_in_o[o], sgu_g[o], sgu_b[o], sgu_w[o], sgu_bias[o], w_out_o[o],
                              ln_mix_g[layer], ln_mix_b[layer])
            xp, _ = _odd_layer(xp, prm, emit_v=False)
            xs, sv = _odd_layer(xs, prm, emit_v=True)
            sgu_v.append(sv)
        fprm = _ffn_params(ffn_w_up[layer], ffn_conv_w[layer], ffn_conv_b[layer], ffn_w_down[layer],
                           ln_ffn_g[layer], ln_ffn_b[layer])
        zero_fc = jnp.zeros((xp.shape[0], FFN_CONV - 1, 2 * D_FF), F32)
        xp, fcp = _ffn_layer(xp, zero_fc, fprm)
        xs, fcs = _ffn_layer(xs, state_ffn_conv[layer], fprm)
        ffn_p.append(fcp)
        ffn_s.append(fcs)
    stack = lambda lst, idx: jnp.stack([item[idx] for item in lst])
    return (xp, xs,
            stack(even_p, 0), stack(even_p, 1), stack(even_p, 2),
            stack(even_s, 0), stack(even_s, 1), stack(even_s, 2),
            stack(even_p, 3), stack(even_p, 4), stack(even_s, 3), stack(even_s, 4),
            jnp.stack(ffn_p), jnp.stack(ffn_s), jnp.stack(sgu_v))
```

```python
import functools

import jax
import jax.numpy as jnp
from jax import lax
from jax.experimental import pallas as pl
from jax.experimental.pallas import tpu as pltpu

F32 = jnp.float32
BF16 = jnp.bfloat16

D_MODEL = 1024
N_HEADS = 8
HEAD_DIM = 64
A_WIDTH = N_HEADS * HEAD_DIM
B_WIDTH = 512
B_BLOCKS = 8
RG_CONV = 4
RG_C = 8.0
C_WIDTH = 1024
C_GROUPS = 8
C_LEN = 128
D_FF = 2816
FFN_CONV = 3
DEPTH = 2
ALPHA = (2 * DEPTH) ** 0.25
LN_EPS = 1e-5

LANES = 128
SUBLANES = 8
HALO = SUBLANES
NEG = -1e30
LOG2E = 1.4426950408889634
ROW_BLOCK = 512
FFN_CHUNK = 256
VMEM_LIMIT = 48 * 1024 * 1024


def _layer_norm(x, g, b):
    mu = jnp.mean(x, axis=-1, keepdims=True)
    xc = x - mu
    var = jnp.mean(xc * xc, axis=-1, keepdims=True)
    return xc * lax.rsqrt(var + LN_EPS) * g + b


def _log_sigmoid(x):
    return jnp.minimum(x, 0.0) - jnp.log1p(jnp.exp(-jnp.abs(x)))


def _softplus(x):
    return jnp.maximum(x, 0.0) + jnp.log1p(jnp.exp(-jnp.abs(x)))


def _split3(x):
    hi = x.astype(BF16)
    r1 = x - hi.astype(F32)
    mid = r1.astype(BF16)
    lo = (r1 - mid.astype(F32)).astype(BF16)
    return hi, mid, lo


def _dot(a, b):
    return jnp.dot(a, b, preferred_element_type=F32)


def _dot_nt(a, b):
    return lax.dot_general(a, b, (((1,), (1,)), ((), ())), preferred_element_type=F32)


def _dot3(pieces, ones_matrix, left):
    out = None
    for p in pieces:
        t = _dot(ones_matrix, p) if left else _dot(p, ones_matrix)
        out = t if out is None else out + t
    return out


def _const_spec(shape):
    nd = len(shape)
    return pl.BlockSpec(shape, lambda *_: (0,) * nd, pipeline_mode=pl.Buffered(1))


def _params(sem, vmem_limit=VMEM_LIMIT):
    return pltpu.CompilerParams(dimension_semantics=sem, vmem_limit_bytes=vmem_limit)


def _plan(n_rows, seq_len, row_block=ROW_BLOCK):
    tm = min(row_block, n_rows)
    nseq = max(1, tm // seq_len)
    bps = max(1, seq_len // tm)
    assert n_rows % tm == 0 and (seq_len % tm == 0 or tm % seq_len == 0)
    return tm, nseq, bps


def _even_in_kernel(x_ref, wqkv_ref, wf_ref, wxg_ref, bf_ref, tri_ref, cs_ref, h0_ref,
                    cw_ref, cb_ref, wa_ref, ba_ref, wx_ref, bx_ref, lam_ref,
                    qkv16_ref, k_ref, v_ref, logf_ref, cum_ref, gated_ref, xtail_ref, hl_ref,
                    carry_ref, ccarry_ref, hcarry_ref, *, tm, nseq, bps, kv_t):
    i = pl.program_id(0)
    lb = tm // nseq
    first = (i % bps) == 0
    x16 = x_ref[...].astype(BF16)
    xg = _dot(x16, wxg_ref[...])
    half = B_WIDTH // 2
    for cols in (slice(0, half), slice(half, B_WIDTH)):
        xb = xg[:, cols]
        if bps > 1:
            halos = [jnp.where(first, cs_ref[0, :, cols], ccarry_ref[:, cols])]
            inits = [jnp.where(first, h0_ref[0, :, cols], hcarry_ref[:, cols])]
            ccarry_ref[:, cols] = xb[tm - HALO:, :]
        else:
            halos = [cs_ref[s, :, cols] for s in range(nseq)]
            inits = [h0_ref[s, :, cols] for s in range(nseq)]
        pieces = []
        for s in range(nseq):
            xs = xb[s * lb:(s + 1) * lb, :]
            xtail_ref[s, :, cols] = xs[lb - HALO:, :]
            xc = cb_ref[:, cols] + cw_ref[RG_CONV - 1:RG_CONV, cols] * xs
            for d in range(1, RG_CONV):
                xc = xc + cw_ref[RG_CONV - 1 - d:RG_CONV - d, cols] * _shift_rows(xs, halos[s], d)
            pieces.append(xc)
        xc = pieces[0] if nseq == 1 else jnp.concatenate(pieces, axis=0)
        xc16 = xc.astype(BF16)
        gate_r = jax.nn.sigmoid(_dot(xc16, wa_ref[cols, cols]) + ba_ref[:, cols])
        gate_i = jax.nn.sigmoid(_dot(xc16, wx_ref[cols, cols]) + bx_ref[:, cols])
        log_a = -RG_C * gate_r * _softplus(-lam_ref[:, cols])
        a = jnp.exp(log_a)
        u = jnp.sqrt(jnp.tanh(-log_a) * (a * a + 1.0)) * (gate_i * xc)
        h = _linear_scan(a, u, inits, lb)
        if bps > 1:
            hcarry_ref[:, cols] = h[tm - 1:tm, :]
        for s in range(nseq):
            hl_ref[0, s:s + 1, cols] = h[(s + 1) * lb - 1:(s + 1) * lb, :]
        gb = xg[:, B_WIDTH + cols.start:B_WIDTH + cols.stop]
        gated_ref[:, cols] = (jax.nn.gelu(gb) * h).astype(BF16)

    qkv = _dot(x16, wqkv_ref[...])
    qkv16_ref[:, :A_WIDTH] = (qkv[:, :A_WIDTH] * (HEAD_DIM ** -0.5 * LOG2E)).astype(BF16)
    qkv16_ref[:, A_WIDTH:] = qkv[:, A_WIDTH:].astype(BF16)
    if kv_t:
        k_ref[0] = qkv[:, A_WIDTH:2 * A_WIDTH].T
        v_ref[0] = qkv[:, 2 * A_WIDTH:].T
    else:
        k_ref[...] = qkv[:, A_WIDTH:2 * A_WIDTH]
        v_ref[...] = qkv[:, 2 * A_WIDTH:]
    logf = _log_sigmoid(_dot(x16, wf_ref[...]) + bf_ref[...])
    logf_ref[...] = logf
    cs = _dot3(_split3(logf), tri_ref[...], left=True)
    if bps > 1:
        cs = cs + jnp.where(first, 0.0, carry_ref[...])
        carry_ref[...] = cs[tm - 1:tm, :]
    cum_ref[...] = cs


def _even_in(x2d, seq_len, conv_state, h0, prm, kv_t):
    n = x2d.shape[0]
    tm, nseq, bps = _plan(n, seq_len)
    nblk = n // tm
    cs = jnp.pad(conv_state, ((0, 0), (HALO - (RG_CONV - 1), 0), (0, 0)))
    h03 = h0.astype(F32)[:, None, :]
    state = lambda r: pl.BlockSpec((nseq, r, B_WIDTH), lambda i: (i // bps, 0, 0))
    consts = [prm[k] for k in ("wqkv", "wf", "wxg", "bf")]
    rg = [prm[k] for k in ("rg_cw", "rg_cb", "rg_wa", "rg_ba", "rg_wx", "rg_bx", "rg_lam")]
    if kv_t:
        assert nseq == 1
        kv_shape = jax.ShapeDtypeStruct((n // seq_len, A_WIDTH, seq_len), F32)
        kv_spec = pl.BlockSpec((1, A_WIDTH, tm), lambda i: (i // bps, 0, i % bps))
    else:
        kv_shape = jax.ShapeDtypeStruct((n, A_WIDTH), F32)
        kv_spec = pl.BlockSpec((tm, A_WIDTH), lambda i: (i, 0))
    r = jnp.arange(tm)
    tri = ((r[None, :] <= r[:, None]) & ((r[None, :] // seq_len) == (r[:, None] // seq_len))).astype(BF16)
    row = lambda w: pl.BlockSpec((tm, w), lambda i: (i, 0))
    out_shapes = (
        jax.ShapeDtypeStruct((n, 3 * A_WIDTH), BF16),
        kv_shape,
        kv_shape,
        jax.ShapeDtypeStruct((n, LANES), F32),
        jax.ShapeDtypeStruct((n, LANES), F32),
        jax.ShapeDtypeStruct((n, B_WIDTH), BF16),
        jax.ShapeDtypeStruct((nblk * nseq, HALO, B_WIDTH), F32),
        jax.ShapeDtypeStruct((nblk, nseq, B_WIDTH), F32),
    )
    return pl.pallas_call(
        functools.partial(_even_in_kernel, tm=tm, nseq=nseq, bps=bps, kv_t=kv_t),
        grid=(nblk,),
        in_specs=[row(D_MODEL)] + [_const_spec(c.shape) for c in consts] + [_const_spec(tri.shape)]
                 + [state(HALO), state(1)] + [_const_spec(c.shape) for c in rg],
        out_specs=(row(3 * A_WIDTH), kv_spec, kv_spec, row(LANES), row(LANES), row(B_WIDTH),
                   pl.BlockSpec((nseq, HALO, B_WIDTH), lambda i: (i, 0, 0)),
                   pl.BlockSpec((1, nseq, B_WIDTH), lambda i: (i, 0, 0))),
        out_shape=out_shapes,
        scratch_shapes=[pltpu.VMEM((1, LANES), F32),
                        pltpu.VMEM((HALO, B_WIDTH), F32),
                        pltpu.VMEM((1, B_WIDTH), F32)],
        compiler_params=_params(("arbitrary",)),
        name="even_in_proj",
    )(x2d, *consts, tri, cs, h03, *rg)


def _fox_prompt_kernel(q_ref, k_ref, v_ref, cq_ref, ck_ref, o_ref,
                       qm_ref, cqr_ref, m_ref, acc_ref, *, blk):
    i = pl.program_id(1)
    lane = lax.broadcasted_iota(jnp.int32, (blk, LANES), 1)
    ngrp = blk // LANES

    for h in range(N_HEADS):
        hp, hh = divmod(h, 2)
        q = q_ref[0, :, hp * LANES:(hp + 1) * LANES]
        keep = (lane < HEAD_DIM) if hh == 0 else (lane >= HEAD_DIM)
        qm_ref[h] = jnp.where(keep, q, jnp.zeros_like(q))
        cqr_ref[h] = jnp.broadcast_to(cq_ref[0, :, h:h + 1] * LOG2E, (blk, LANES))
    m_ref[...] = jnp.full(m_ref.shape, NEG, F32)
    acc_ref[...] = jnp.zeros_like(acc_ref)

    def update(kg0, ngrp, diagonal):
        koff = pl.multiple_of(kg0 * LANES, LANES)
        vlane = lax.broadcasted_iota(jnp.int32, (ngrp * LANES, LANES), 1)
        if diagonal:
            rows = lax.broadcasted_iota(jnp.int32, (blk, LANES), 0)
            own = ngrp - blk // LANES
            visible = [None] * own + [lane + g * LANES <= rows for g in range(blk // LANES)]
        for h in range(N_HEADS):
            hp, hh = divmod(h, 2)
            k16 = k_ref[0, pl.ds(koff, ngrp * LANES), hp * LANES:(hp + 1) * LANES]
            v16 = v_ref[0, pl.ds(koff, ngrp * LANES), hp * LANES:(hp + 1) * LANES]
            vkeep = (vlane < HEAD_DIM) if hh == 0 else (vlane >= HEAD_DIM)
            v16 = jnp.where(vkeep, v16, jnp.ones_like(v16))
            s = _dot_nt(qm_ref[h], k16)
            grp = [s[:, g * LANES:(g + 1) * LANES] - ck_ref[0, kg0 + g, h:h + 1, :] * LOG2E
                   for g in range(ngrp)]
            if diagonal:
                grp = [g if vis is None else jnp.where(vis, g, NEG) for vis, g in zip(visible, grp)]
            mc = grp[0]
            for g in grp[1:]:
                mc = jnp.maximum(mc, g)
            cqr = cqr_ref[h]
            m_old = m_ref[h]
            m_new = jnp.maximum(m_old, jnp.max(mc, axis=1, keepdims=True) + cqr)
            shift = m_new - cqr
            p16 = jnp.concatenate([jnp.exp2(g - shift).astype(BF16) for g in grp], axis=1)
            acc_ref[h] = jnp.exp2(m_old - m_new) * acc_ref[h] + _dot(p16, v16)
            m_ref[h] = m_new

    def body(j, carry):
        update(j * (2 * ngrp), 2 * ngrp, False)
        return carry

    lax.fori_loop(0, i // 2, body, 0)

    @pl.when(i % 2 == 1)
    def _():
        update((i - 1) * ngrp, 2 * ngrp, True)

    @pl.when(i % 2 == 0)
    def _():
        update(i * ngrp, ngrp, True)

    for hp in range(N_HEADS // 2):
        a0 = acc_ref[2 * hp]
        a1 = acc_ref[2 * hp + 1]
        o = jnp.where(lane < HEAD_DIM, a0 / a0[:, HEAD_DIM:HEAD_DIM + 1], a1 / a1[:, 0:1])
        o_ref[0, :, hp * LANES:(hp + 1) * LANES] = o.astype(o_ref.dtype)


def _fox_prompt(qkv16, cum, cum_t, blk):
    bsz, t, _ = qkv16.shape
    nb = t // blk
    ng = t // LANES
    ck = cum_t.reshape(bsz, N_HEADS, ng, LANES).transpose(0, 2, 1, 3)
    return pl.pallas_call(
        functools.partial(_fox_prompt_kernel, blk=blk),
        grid=(bsz, nb),
        in_specs=[pl.BlockSpec((1, blk, A_WIDTH), lambda b, i: (b, i, 0)),
                  pl.BlockSpec((1, t, A_WIDTH), lambda b, i: (b, 0, 1)),
                  pl.BlockSpec((1, t, A_WIDTH), lambda b, i: (b, 0, 2)),
                  pl.BlockSpec((1, blk, LANES), lambda b, i: (b, i, 0)),
                  pl.BlockSpec((1, ng, N_HEADS, LANES), lambda b, i: (b, 0, 0, 0))],
        out_specs=pl.BlockSpec((1, blk, A_WIDTH), lambda b, i: (b, i, 0)),
        out_shape=jax.ShapeDtypeStruct((bsz, t, A_WIDTH), BF16),
        scratch_shapes=[pltpu.VMEM((N_HEADS, blk, LANES), BF16),
                        pltpu.VMEM((N_HEADS, blk, LANES), F32),
                        pltpu.VMEM((N_HEADS, blk, LANES), F32),
                        pltpu.VMEM((N_HEADS, blk, LANES), F32)],
        compiler_params=_params(("arbitrary", "arbitrary")),
        name="fox_prompt",
    )(qkv16, qkv16, qkv16, cum, ck)


def _fox_sample_kernel(qkv_ref, cq_ref, ckn_ref, lf_ref, upper_ref, kt_ref, vt_ref, o_ref,
                       ckc_ref, *, t, plen, chunk):
    lane = lax.broadcasted_iota(jnp.int32, (t, LANES), 1)
    carry = jnp.zeros((N_HEADS, 1), F32)
    for c in reversed(range(plen // chunk)):
        lf = lf_ref[0, :, c * chunk:(c + 1) * chunk]
        after = _dot3(_split3(lf), upper_ref[...], left=False)
        ckc_ref[:, c * chunk:(c + 1) * chunk] = -(after + carry)
        carry = carry + jnp.sum(lf, axis=1, keepdims=True)
    rows = lax.broadcasted_iota(jnp.int32, (t, t), 0)
    cols = lax.broadcasted_iota(jnp.int32, (t, t), 1)
    visible = cols <= rows

    def per_head(s, bias_ref_rows, mask):
        halves = []
        for hh in range(2):
            part = s[hh * t:(hh + 1) * t, :] - bias_ref_rows[hh] * LOG2E
            halves.append(jnp.where(visible, part, NEG) if mask else part)
        return jnp.concatenate(halves, axis=0)

    for hp in range(N_HEADS // 2):
        h0 = 2 * hp
        sl = slice(hp * LANES, (hp + 1) * LANES)
        q = qkv_ref[0, :, hp * LANES:(hp + 1) * LANES]
        zero = jnp.zeros_like(q)
        qs = jnp.concatenate([jnp.where(lane < HEAD_DIM, q, zero),
                              jnp.where(lane >= HEAD_DIM, q, zero)], axis=0)
        kn16 = qkv_ref[0, :, A_WIDTH + hp * LANES:A_WIDTH + (hp + 1) * LANES]
        vn16 = qkv_ref[0, :, 2 * A_WIDTH + hp * LANES:2 * A_WIDTH + (hp + 1) * LANES]
        cq = jnp.concatenate([cq_ref[0, :, h0:h0 + 1], cq_ref[0, :, h0 + 1:h0 + 2]], axis=0) * LOG2E
        kt16 = kt_ref[0, h0:h0 + 2].reshape(LANES, plen).astype(BF16)
        vt16 = vt_ref[0, h0:h0 + 2].reshape(LANES, plen).astype(BF16)
        s_new = per_head(_dot_nt(qs, kn16), [ckn_ref[0, h0:h0 + 1, :], ckn_ref[0, h0 + 1:h0 + 2, :]], True)
        s_old = per_head(_dot(qs, kt16), [ckc_ref[h0:h0 + 1, :], ckc_ref[h0 + 1:h0 + 2, :]], False)
        m = jnp.maximum(jnp.max(s_new, axis=1, keepdims=True), jnp.max(s_old, axis=1, keepdims=True)) + cq
        shift = m - cq
        p_new = jnp.exp2(s_new - shift)
        p_old = jnp.exp2(s_old - shift)
        l = jnp.sum(p_new, axis=1, keepdims=True) + jnp.sum(p_old, axis=1, keepdims=True)
        acc = _dot(p_new.astype(BF16), vn16) + _dot_nt(p_old.astype(BF16), vt16)
        o = acc / l
        o_ref[0, :, sl] = jnp.where(lane < HEAD_DIM, o[:t, :], o[t:, :]).astype(o_ref.dtype)


def _fox_sample(qkv16, cum, cum_t, kt, vt, lf_t, chunk=512):
    bsz, t, _ = qkv16.shape
    plen = kt.shape[-1]
    r = jnp.arange(chunk)
    upper = (r[:, None] > r[None, :]).astype(BF16)
    per_b = lambda shape: pl.BlockSpec((1,) + shape, lambda b: (b,) + (0,) * len(shape))
    return pl.pallas_call(
        functools.partial(_fox_sample_kernel, t=t, plen=plen, chunk=chunk),
        grid=(bsz,),
        in_specs=[per_b((t, 3 * A_WIDTH)), per_b((t, LANES)), per_b((N_HEADS, t)),
                  per_b((N_HEADS, plen)), _const_spec(upper.shape),
                  per_b((N_HEADS, HEAD_DIM, plen)), per_b((N_HEADS, HEAD_DIM, plen))],
        out_specs=per_b((t, A_WIDTH)),
        out_shape=jax.ShapeDtypeStruct((bsz, t, A_WIDTH), BF16),
        scratch_shapes=[pltpu.VMEM((N_HEADS, plen), F32)],
        compiler_params=_params(("arbitrary",)),
        name="fox_sample",
    )(qkv16, cum, cum_t, lf_t, upper, kt, vt)


def _shift_rows(h, halo, d):
    rolled = pltpu.roll(h, d, axis=0)
    rowid = lax.broadcasted_iota(jnp.int32, halo.shape, 0)
    head = jnp.where(rowid < d, pltpu.roll(halo, d, axis=0), rolled[:HALO, :])
    return jnp.concatenate([head, rolled[HALO:, :]], axis=0)


def _linear_scan(a, u, inits, lb):
    rows, width = a.shape
    ngroups = rows // SUBLANES
    a3 = a.reshape(ngroups, SUBLANES, width)
    u3 = u.reshape(ngroups, SUBLANES, width)
    sub = lax.broadcasted_iota(jnp.int32, a3.shape, 1)
    d = 1
    while d < SUBLANES:
        inside = sub >= d
        u3 = jnp.where(inside, a3 * pltpu.roll(u3, d, axis=1) + u3, u3)
        a3 = jnp.where(inside, a3 * pltpu.roll(a3, d, axis=1), a3)
        d *= 2
    out = []
    carry = None
    for g in range(ngroups):
        if g % (lb // SUBLANES) == 0:
            carry = inits[g // (lb // SUBLANES)]
        hg = u3[g] + a3[g] * carry
        out.append(hg)
        carry = hg[SUBLANES - 1:, :]
    return jnp.concatenate(out, axis=0)


def _ffn_kernel(*refs, tm, nseq, bps, tn, mix):
    if mix:
        x_ref, attn_ref, gated_ref, woa_ref, wob_ref, mg_ref, mb_ref = refs[:7]
        refs = refs[7:]
    else:
        x_ref = refs[0]
        refs = refs[1:]
    st_ref, wup_ref, taps_ref, wd_ref, g_ref, b_ref, y_ref, so_ref, a_ref, carry_ref = refs
    i = pl.program_id(0)
    lb = tm // nseq
    x = x_ref[...]
    if mix:
        mixed = _dot(attn_ref[...], woa_ref[...]) + _dot(gated_ref[...], wob_ref[...])
        x = _layer_norm(ALPHA * x + mixed, mg_ref[...], mb_ref[...])
    x16 = x.astype(BF16)
    first = (i % bps) == 0

    def conv_branch(h, cols):
        taps = taps_ref[:, cols]
        outs = []
        for s in range(nseq):
            hs = h[s * lb:(s + 1) * lb, :]
            if bps > 1:
                halo = jnp.where(first, st_ref[0, :, cols], carry_ref[:, cols])
            else:
                halo = st_ref[s, :, cols]
            so_ref[s, :, cols] = hs[lb - HALO:, :]
            y = taps[3:4, :] + taps[2:3, :] * hs
            y = y + taps[1:2, :] * _shift_rows(hs, halo, 1)
            y = y + taps[0:1, :] * _shift_rows(hs, halo, 2)
            outs.append(y)
        if bps > 1:
            carry_ref[:, cols] = h[tm - HALO:, :]
        return outs

    for c in range(D_FF // tn):
        cg = slice(c * tn, (c + 1) * tn)
        cu = slice(D_FF + c * tn, D_FF + (c + 1) * tn)
        yg = conv_branch(_dot(x16, wup_ref[:, cg]), cg)
        yu = conv_branch(_dot(x16, wup_ref[:, cu]), cu)
        for s in range(nseq):
            a_ref[s * lb:(s + 1) * lb, cg] = (jax.nn.gelu(yg[s]) * yu[s]).astype(BF16)
    y = _dot(a_ref[...], wd_ref[...])
    y_ref[...] = _layer_norm(ALPHA * x + y, g_ref[...], b_ref[...])


def _ffn(x2d, state, seq_len, prm, mix=None):
    n = x2d.shape[0]
    tm, nseq, bps = _plan(n, seq_len)
    nblk = n // tm
    bsz = state.shape[0]
    st = jnp.pad(state, ((0, 0), (HALO - (FFN_CONV - 1), 0), (0, 0)))
    row = pl.BlockSpec((tm, D_MODEL), lambda i: (i, 0))
    consts = [prm[k] for k in ("w_up", "taps", "w_d", "ln_g", "ln_b")]
    lead, lead_specs = [x2d], [row]
    if mix is not None:
        attn16, gated16, eprm = mix
        mix_consts = [eprm[k] for k in ("wo_a", "wo_b", "ln_g", "ln_b")]
        lead += [attn16, gated16] + mix_consts
        lead_specs += [pl.BlockSpec((tm, A_WIDTH), lambda i: (i, 0)), pl.BlockSpec((tm, B_WIDTH), lambda i: (i, 0))]
        lead_specs += [_const_spec(c.shape) for c in mix_consts]
    y, so = pl.pallas_call(
        functools.partial(_ffn_kernel, tm=tm, nseq=nseq, bps=bps, tn=FFN_CHUNK, mix=mix is not None),
        grid=(nblk,),
        in_specs=lead_specs + [pl.BlockSpec((nseq, HALO, 2 * D_FF), lambda i: (i // bps, 0, 0))]
                 + [_const_spec(c.shape) for c in consts],
        out_specs=(row, pl.BlockSpec((nseq, HALO, 2 * D_FF), lambda i: (i, 0, 0))),
        out_shape=(jax.ShapeDtypeStruct((n, D_MODEL), F32),
                   jax.ShapeDtypeStruct((nblk * nseq, HALO, 2 * D_FF), F32)),
        scratch_shapes=[pltpu.VMEM((tm, D_FF), BF16),
                        pltpu.VMEM((HALO, 2 * D_FF), F32)],
        compiler_params=_params(("arbitrary",)),
        name="conv_ffn",
    )(*lead, st, *consts)
    so = so.reshape(bsz, -1, HALO, 2 * D_FF)[:, -1]
    return y, so[:, HALO - (FFN_CONV - 1):, :]


def _odd_kernel(x_ref, wu_ref, wv_ref, sg_ref, sb_ref, ws_ref, bias_ref, wo_ref, g_ref, b_ref,
                y_ref, *rest, tm, lc, emit_v):
    gate_ref = rest[-1]
    x = x_ref[...]
    x16 = x.astype(BF16)
    u = jax.nn.gelu(_dot(x16, wu_ref[...]))
    v = _layer_norm(jax.nn.gelu(_dot(x16, wv_ref[...])), sg_ref[...], sb_ref[...])
    if emit_v:
        rest[0][...] = v
    v16 = v.astype(BF16)
    gw = C_WIDTH // C_GROUPS
    for c in range(tm // lc):
        rows = slice(c * lc, (c + 1) * lc)
        for g in range(C_GROUPS):
            cols = slice(g * gw, (g + 1) * gw)
            mixed = _dot(ws_ref[g], v16[rows, cols]) + bias_ref[:, cols]
            gate_ref[rows, cols] = (u[rows, cols] * mixed).astype(BF16)
    y = _dot(gate_ref[...], wo_ref[...])
    y_ref[...] = _layer_norm(ALPHA * x + y, g_ref[...], b_ref[...])


def _odd(x2d, seq_len, prm, emit_v):
    n = x2d.shape[0]
    tm = min(ROW_BLOCK, n)
    lc = min(seq_len, C_LEN)
    gw = C_WIDTH // C_GROUPS
    ws16 = (prm["sgu_w"][:, :lc, :lc] * jnp.tril(jnp.ones((lc, lc), F32))).astype(BF16)
    bias = jnp.repeat(prm["sgu_bias"][:, :lc].T, gw, axis=1)
    consts = [prm["w_u"], prm["w_v"], prm["sgu_g"], prm["sgu_b"], ws16, bias,
              prm["w_o"], prm["ln_g"], prm["ln_b"]]
    row = pl.BlockSpec((tm, D_MODEL), lambda i: (i, 0))
    n_out = 2 if emit_v else 1
    return pl.pallas_call(
        functools.partial(_odd_kernel, tm=tm, lc=lc, emit_v=emit_v),
        grid=(n // tm,),
        in_specs=[row] + [_const_spec(c.shape) for c in consts],
        out_specs=(row, pl.BlockSpec((tm, C_WIDTH), lambda i: (i, 0)))[:n_out],
        out_shape=(jax.ShapeDtypeStruct((n, D_MODEL), F32),
                   jax.ShapeDtypeStruct((n, C_WIDTH), F32))[:n_out],
        scratch_shapes=[pltpu.VMEM((tm, C_WIDTH), BF16)],
        compiler_params=_params(("arbitrary",)),
        name="odd_mixer",
    )(x2d, *consts)


def _row(v):
    return v.astype(F32)[None, :]


def _block_diag(w):
    nb, bd, _ = w.shape
    eye = jnp.eye(nb, dtype=w.dtype)
    return jnp.einsum("ncd,nm->ncmd", w, eye).reshape(nb * bd, nb * bd)


def _even_params(w_in, b_f, rg_conv_w, rg_conv_b, rg_wa, rg_ba, rg_wx, rg_bx, rg_lam, w_out, ln_g, ln_b):
    nf = 3 * A_WIDTH
    wf = jnp.pad(w_in[:, nf:nf + N_HEADS], ((0, 0), (0, LANES - N_HEADS)))
    return {
        "wqkv": w_in[:, :nf].astype(BF16),
        "wf": wf.astype(BF16),
        "wxg": w_in[:, nf + N_HEADS:].astype(BF16),
        "bf": jnp.pad(b_f, (0, LANES - N_HEADS))[None, :],
        "rg_cw": rg_conv_w, "rg_cb": _row(rg_conv_b),
        "rg_wa": _block_diag(rg_wa).astype(BF16), "rg_ba": _row(rg_ba),
        "rg_wx": _block_diag(rg_wx).astype(BF16), "rg_bx": _row(rg_bx),
        "rg_lam": _row(rg_lam),
        "wo_a": w_out[:A_WIDTH].astype(BF16), "wo_b": w_out[A_WIDTH:].astype(BF16),
        "ln_g": _row(ln_g), "ln_b": _row(ln_b),
    }


def _ffn_params(w_up, conv_w, conv_b, w_down, ln_g, ln_b):
    taps = jnp.concatenate([conv_w, conv_b[None, :],
                            jnp.zeros((HALO - FFN_CONV - 1, 2 * D_FF), F32)], axis=0)
    return {"w_up": w_up.astype(BF16), "taps": taps, "w_d": w_down.astype(BF16),
            "ln_g": _row(ln_g), "ln_b": _row(ln_b)}


def _odd_params(w_in, sgu_g, sgu_b, sgu_w, sgu_bias, w_out, ln_g, ln_b):
    return {
        "w_u": w_in[:, :C_WIDTH].astype(BF16), "w_v": w_in[:, C_WIDTH:].astype(BF16),
        "sgu_g": _row(sgu_g), "sgu_b": _row(sgu_b), "sgu_w": sgu_w, "sgu_bias": sgu_bias,
        "w_o": w_out.astype(BF16), "ln_g": _row(ln_g), "ln_b": _row(ln_b),
    }


def _even_layer(x3d, past, prm):
    bsz, t, _ = x3d.shape
    x2d = x3d.reshape(bsz * t, D_MODEL)
    assert t >= HALO
    if past is None:
        conv_s = jnp.zeros((bsz, RG_CONV - 1, B_WIDTH), F32)
        h0 = jnp.zeros((bsz, B_WIDTH), F32)
    else:
        k_c, v_c, lf_c, conv_s, h0 = past
    kv_t = t >= ROW_BLOCK
    qkv16, k, v, logf, cum, gated16, xtail, hl = _even_in(x2d, t, conv_s, h0, prm, kv_t)
    qkv16 = qkv16.reshape(bsz, t, 3 * A_WIDTH)
    cum3 = cum.reshape(bsz, t, LANES)
    cum_t = cum3[:, :, :N_HEADS].transpose(0, 2, 1)
    if past is None:
        attn = _fox_prompt(qkv16, cum3, cum_t, ROW_BLOCK)
    else:
        attn = _fox_sample(qkv16, cum3, cum_t, k_c.transpose(0, 2, 3, 1), v_c.transpose(0, 2, 3, 1),
                           lf_c.transpose(0, 2, 1))
    if kv_t:
        new_k = k.reshape(bsz, N_HEADS, HEAD_DIM, t).transpose(0, 3, 1, 2)
        new_v = v.reshape(bsz, N_HEADS, HEAD_DIM, t).transpose(0, 3, 1, 2)
    else:
        new_k = k.reshape(bsz, t, N_HEADS, HEAD_DIM)
        new_v = v.reshape(bsz, t, N_HEADS, HEAD_DIM)
    new_logf = logf.reshape(bsz, t, LANES)[:, :, :N_HEADS]
    h_last = hl.reshape(bsz, -1, B_WIDTH)[:, -1, :]
    conv_new = xtail.reshape(bsz, -1, HALO, B_WIDTH)[:, -1, HALO - (RG_CONV - 1):, :]
    return (attn.reshape(bsz * t, A_WIDTH), gated16), (new_k, new_v, new_logf, conv_new, h_last)


def _ffn_layer(x3d, state, prm, mix=None):
    bsz, t, _ = x3d.shape
    y, new_state = _ffn(x3d.reshape(bsz * t, D_MODEL), state, t, prm, mix)
    return y.reshape(bsz, t, D_MODEL), new_state


def _odd_layer(x3d, prm, emit_v):
    bsz, t, _ = x3d.shape
    outs = _odd(x3d.reshape(bsz * t, D_MODEL), t, prm, emit_v)
    v = outs[1].reshape(bsz, t, C_WIDTH) if emit_v else None
    return outs[0].reshape(bsz, t, D_MODEL), v


def kernel(x_prompt, x_sample, cache_k, cache_v, cache_logf, state_rglru_conv, state_rglru_h, state_ffn_conv, w_in_e, b_f, rg_conv_w, rg_conv_b, rg_wa, rg_ba, rg_wx, rg_bx, rg_lam, w_out_e, w_in_o, sgu_g, sgu_b, sgu_w, sgu_bias, w_out_o, ln_mix_g, ln_mix_b, ln_ffn_g, ln_ffn_b, ffn_w_up, ffn_conv_w, ffn_conv_b, ffn_w_down):
    xp, xs = x_prompt, x_sample
    even_p, even_s, ffn_p, ffn_s, sgu_v = [], [], [], [], []
    for layer in range(DEPTH):
        if layer % 2 == 0:
            e = layer // 2
            prm = _even_params(w_in_e[e], b_f[e], rg_conv_w[e], rg_conv_b[e], rg_wa[e], rg_ba[e],
                               rg_wx[e], rg_bx[e], rg_lam[e], w_out_e[e],
                               ln_mix_g[layer], ln_mix_b[layer])
            branches_p, st_p = _even_layer(xp, None, prm)
            past = (cache_k[e], cache_v[e], cache_logf[e], state_rglru_conv[e], state_rglru_h[e])
            branches_s, st_s = _even_layer(xs, past, prm)
            mix_p, mix_s = branches_p + (prm,), branches_s + (prm,)
            even_p.append(st_p)
            even_s.append(st_s)
        else:
            o = layer // 2
            prm = _odd_params(w_in_o[o], sgu_g[o], sgu_b[o], sgu_w[o], sgu_bias[o], w_out_o[o],
                              ln_mix_g[layer], ln_mix_b[layer])
            xp, _ = _odd_layer(xp, prm, emit_v=False)
            xs, sv = _odd_layer(xs, prm, emit_v=True)
            sgu_v.append(sv)
            mix_p = mix_s = None
        fprm = _ffn_params(ffn_w_up[layer], ffn_conv_w[layer], ffn_conv_b[layer], ffn_w_down[layer],
                           ln_ffn_g[layer], ln_ffn_b[layer])
        zero_fc = jnp.zeros((xp.shape[0], FFN_CONV - 1, 2 * D_FF), F32)
        xp, fcp = _ffn_layer(xp, zero_fc, fprm, mix_p)
        xs, fcs = _ffn_layer(xs, state_ffn_conv[layer], fprm, mix_s)
        ffn_p.append(fcp)
        ffn_s.append(fcs)
    stack = lambda lst, idx: jnp.stack([item[idx] for item in lst])
    return (xp, xs,
            stack(even_p, 0), stack(even_p, 1), stack(even_p, 2),
            stack(even_s, 0), stack(even_s, 1), stack(even_s, 2),
            stack(even_p, 3), stack(even_p, 4), stack(even_s, 3), stack(even_s, 4),
            jnp.stack(ffn_p), jnp.stack(ffn_s), jnp.stack(sgu_v))
```

```python
import functools

import jax
import jax.numpy as jnp
from jax import lax
from jax.experimental import pallas as pl
from jax.experimental.pallas import tpu as pltpu

F32 = jnp.float32
BF16 = jnp.bfloat16

D_MODEL = 1024
N_HEADS = 8
HEAD_DIM = 64
A_WIDTH = N_HEADS * HEAD_DIM
B_WIDTH = 512
RG_CONV = 4
RG_C = 8.0
C_WIDTH = 1024
C_GROUPS = 8
C_LEN = 128
D_FF = 2816
FFN_CONV = 3
DEPTH = 2
ALPHA = (2 * DEPTH) ** 0.25
LN_EPS = 1e-5

LANES = 128
SUBLANES = 8
HALO = SUBLANES
NEG = -1e30
LOG2E = 1.4426950408889634
ROW_BLOCK = 512
FFN_CHUNK = 256
VMEM_LIMIT = 48 * 1024 * 1024


def _layer_norm(x, g, b):
    mu = jnp.mean(x, axis=-1, keepdims=True)
    xc = x - mu
    var = jnp.mean(xc * xc, axis=-1, keepdims=True)
    return xc * lax.rsqrt(var + LN_EPS) * g + b


def _log_sigmoid(x):
    return jnp.minimum(x, 0.0) - jnp.log1p(jnp.exp(-jnp.abs(x)))


def _softplus(x):
    return jnp.maximum(x, 0.0) + jnp.log1p(jnp.exp(-jnp.abs(x)))


def _dot(a, b):
    return jnp.dot(a, b, preferred_element_type=F32)


def _dot_nt(a, b):
    return lax.dot_general(a, b, (((1,), (1,)), ((), ())), preferred_element_type=F32)


def _const_spec(shape):
    nd = len(shape)
    return pl.BlockSpec(shape, lambda *_: (0,) * nd, pipeline_mode=pl.Buffered(1))


def _params(sem):
    return pltpu.CompilerParams(dimension_semantics=sem, vmem_limit_bytes=VMEM_LIMIT)


def _plan(n_rows, seq_len):
    tm = min(ROW_BLOCK, n_rows)
    nseq = max(1, tm // seq_len)
    bps = max(1, seq_len // tm)
    assert n_rows % tm == 0 and (seq_len % tm == 0 or tm % seq_len == 0)
    return tm, nseq, bps


def _even_in_kernel(x_ref, wqkv_ref, wf_ref, wxg_ref, bf_ref, cs_ref, h0_ref,
                    cw_ref, cb_ref, wa_ref, ba_ref, wx_ref, bx_ref, lam_ref,
                    qkv16_ref, k_ref, v_ref, logf_ref, cum_ref, gated_ref, xtail_ref, hl_ref,
                    carry_ref, ccarry_ref, hcarry_ref, *, tm, nseq, bps, kv_t):
    i = pl.program_id(0)
    lb = tm // nseq
    first = (i % bps) == 0
    x16 = x_ref[...].astype(BF16)
    xg = _dot(x16, wxg_ref[...])
    half = B_WIDTH // 2
    for cols in (slice(0, half), slice(half, B_WIDTH)):
        xb = xg[:, cols]
        if bps > 1:
            halos = [jnp.where(first, cs_ref[0, :, cols], ccarry_ref[:, cols])]
            inits = [jnp.where(first, h0_ref[0, :, cols], hcarry_ref[:, cols])]
            ccarry_ref[:, cols] = xb[tm - HALO:, :]
        else:
            halos = [cs_ref[s, :, cols] for s in range(nseq)]
            inits = [h0_ref[s, :, cols] for s in range(nseq)]
        pieces = []
        for s in range(nseq):
            xs = xb[s * lb:(s + 1) * lb, :]
            xtail_ref[s, :, cols] = xs[lb - HALO:, :]
            xc = cb_ref[:, cols] + cw_ref[RG_CONV - 1:RG_CONV, cols] * xs
            for d in range(1, RG_CONV):
                xc = xc + cw_ref[RG_CONV - 1 - d:RG_CONV - d, cols] * _shift_rows(xs, halos[s], d)
            pieces.append(xc)
        xc = pieces[0] if nseq == 1 else jnp.concatenate(pieces, axis=0)
        xc16 = xc.astype(BF16)
        gate_r = jax.nn.sigmoid(_dot(xc16, wa_ref[cols, cols]) + ba_ref[:, cols])
        gate_i = jax.nn.sigmoid(_dot(xc16, wx_ref[cols, cols]) + bx_ref[:, cols])
        log_a = -RG_C * gate_r * _softplus(-lam_ref[:, cols])
        a = jnp.exp(log_a)
        u = jnp.sqrt(jnp.tanh(-log_a) * (a * a + 1.0)) * (gate_i * xc)
        h = _linear_scan(a, u, inits, lb)
        if bps > 1:
            hcarry_ref[:, cols] = h[tm - 1:tm, :]
        for s in range(nseq):
            hl_ref[0, s:s + 1, cols] = h[(s + 1) * lb - 1:(s + 1) * lb, :]
        gb = xg[:, B_WIDTH + cols.start:B_WIDTH + cols.stop]
        gated_ref[:, cols] = (jax.nn.gelu(gb) * h).astype(BF16)

    qkv = _dot(x16, wqkv_ref[...])
    qkv16_ref[:, :A_WIDTH] = (qkv[:, :A_WIDTH] * (HEAD_DIM ** -0.5 * LOG2E)).astype(BF16)
    qkv16_ref[:, A_WIDTH:] = qkv[:, A_WIDTH:].astype(BF16)
    if kv_t:
        k_ref[0] = qkv[:, A_WIDTH:2 * A_WIDTH].T
        v_ref[0] = qkv[:, 2 * A_WIDTH:].T
    else:
        k_ref[...] = qkv[:, A_WIDTH:2 * A_WIDTH]
        v_ref[...] = qkv[:, 2 * A_WIDTH:]
    logf = _log_sigmoid(_dot(x16, wf_ref[...]) + bf_ref[...])
    logf_ref[...] = logf
    zero = jnp.zeros((1, LANES), F32)
    if bps > 1:
        cs = _cumsum_rows(logf, [jnp.where(first, zero, carry_ref[...])], lb)
        carry_ref[...] = cs[tm - 1:tm, :]
    else:
        cs = _cumsum_rows(logf, [zero] * nseq, lb)
    cum_ref[...] = cs


def _even_in(x2d, seq_len, conv_state, h0, prm, kv_t):
    n = x2d.shape[0]
    tm, nseq, bps = _plan(n, seq_len)
    nblk = n // tm
    cs = jnp.pad(conv_state, ((0, 0), (HALO - (RG_CONV - 1), 0), (0, 0)))
    h03 = h0.astype(F32)[:, None, :]
    state = lambda r: pl.BlockSpec((nseq, r, B_WIDTH), lambda i: (i // bps, 0, 0))
    consts = [prm[k] for k in ("wqkv", "wf", "wxg", "bf")]
    rg = [prm[k] for k in ("rg_cw", "rg_cb", "rg_wa", "rg_ba", "rg_wx", "rg_bx", "rg_lam")]
    if kv_t:
        assert nseq == 1
        kv_shape = jax.ShapeDtypeStruct((n // seq_len, A_WIDTH, seq_len), F32)
        kv_spec = pl.BlockSpec((1, A_WIDTH, tm), lambda i: (i // bps, 0, i % bps))
    else:
        kv_shape = jax.ShapeDtypeStruct((n, A_WIDTH), F32)
        kv_spec = pl.BlockSpec((tm, A_WIDTH), lambda i: (i, 0))
    row = lambda w: pl.BlockSpec((tm, w), lambda i: (i, 0))
    out_shapes = (
        jax.ShapeDtypeStruct((n, 3 * A_WIDTH), BF16),
        kv_shape,
        kv_shape,
        jax.ShapeDtypeStruct((n, LANES), F32),
        jax.ShapeDtypeStruct((n, LANES), F32),
        jax.ShapeDtypeStruct((n, B_WIDTH), BF16),
        jax.ShapeDtypeStruct((nblk * nseq, HALO, B_WIDTH), F32),
        jax.ShapeDtypeStruct((nblk, nseq, B_WIDTH), F32),
    )
    return pl.pallas_call(
        functools.partial(_even_in_kernel, tm=tm, nseq=nseq, bps=bps, kv_t=kv_t),
        grid=(nblk,),
        in_specs=[row(D_MODEL)] + [_const_spec(c.shape) for c in consts]
                 + [state(HALO), state(1)] + [_const_spec(c.shape) for c in rg],
        out_specs=(row(3 * A_WIDTH), kv_spec, kv_spec, row(LANES), row(LANES), row(B_WIDTH),
                   pl.BlockSpec((nseq, HALO, B_WIDTH), lambda i: (i, 0, 0)),
                   pl.BlockSpec((1, nseq, B_WIDTH), lambda i: (i, 0, 0))),
        out_shape=out_shapes,
        scratch_shapes=[pltpu.VMEM((1, LANES), F32),
                        pltpu.VMEM((HALO, B_WIDTH), F32),
                        pltpu.VMEM((1, B_WIDTH), F32)],
        compiler_params=_params(("arbitrary",)),
        name="even_in_proj",
    )(x2d, *consts, cs, h03, *rg)


def _fox_prompt_kernel(q_ref, k_ref, v_ref, cq_ref, ck_ref, o_ref,
                       qm_ref, cqr_ref, m_ref, acc_ref, *, blk):
    i = pl.program_id(1)
    lane = lax.broadcasted_iota(jnp.int32, (blk, LANES), 1)
    ngrp = blk // LANES

    for h in range(N_HEADS):
        hp, hh = divmod(h, 2)
        q = q_ref[0, :, hp * LANES:(hp + 1) * LANES]
        keep = (lane < HEAD_DIM) if hh == 0 else (lane >= HEAD_DIM)
        qm_ref[h] = jnp.where(keep, q, jnp.zeros_like(q))
        cqr_ref[h] = jnp.broadcast_to(cq_ref[0, :, h:h + 1] * LOG2E, (blk, LANES))
    m_ref[...] = jnp.full(m_ref.shape, NEG, F32)
    acc_ref[...] = jnp.zeros_like(acc_ref)

    def update(kg0, ngrp, diagonal):
        koff = pl.multiple_of(kg0 * LANES, LANES)
        vlane = lax.broadcasted_iota(jnp.int32, (ngrp * LANES, LANES), 1)
        if diagonal:
            rows = lax.broadcasted_iota(jnp.int32, (blk, LANES), 0)
            own = ngrp - blk // LANES
            visible = [None] * own + [lane + g * LANES <= rows for g in range(blk // LANES)]
        for h in range(N_HEADS):
            hp, hh = divmod(h, 2)
            k16 = k_ref[0, pl.ds(koff, ngrp * LANES), hp * LANES:(hp + 1) * LANES]
            v16 = v_ref[0, pl.ds(koff, ngrp * LANES), hp * LANES:(hp + 1) * LANES]
            vkeep = (vlane < HEAD_DIM) if hh == 0 else (vlane >= HEAD_DIM)
            v16 = jnp.where(vkeep, v16, jnp.ones_like(v16))
            s = _dot_nt(qm_ref[h], k16)
            grp = [s[:, g * LANES:(g + 1) * LANES] - ck_ref[0, kg0 + g, h:h + 1, :] * LOG2E
                   for g in range(ngrp)]
            if diagonal:
                grp = [g if vis is None else jnp.where(vis, g, NEG) for vis, g in zip(visible, grp)]
            mc = grp[0]
            for g in grp[1:]:
                mc = jnp.maximum(mc, g)
            cqr = cqr_ref[h]
            m_old = m_ref[h]
            m_new = jnp.maximum(m_old, jnp.max(mc, axis=1, keepdims=True) + cqr)
            shift = m_new - cqr
            p16 = jnp.concatenate([jnp.exp2(g - shift).astype(BF16) for g in grp], axis=1)
            acc_ref[h] = jnp.exp2(m_old - m_new) * acc_ref[h] + _dot(p16, v16)
            m_ref[h] = m_new

    def body(j, carry):
        update(j * (2 * ngrp), 2 * ngrp, False)
        return carry

    lax.fori_loop(0, i // 2, body, 0)

    @pl.when(i % 2 == 1)
    def _():
        update((i - 1) * ngrp, 2 * ngrp, True)

    @pl.when(i % 2 == 0)
    def _():
        update(i * ngrp, ngrp, True)

    for hp in range(N_HEADS // 2):
        a0 = acc_ref[2 * hp]
        a1 = acc_ref[2 * hp + 1]
        o = jnp.where(lane < HEAD_DIM, a0 / a0[:, HEAD_DIM:HEAD_DIM + 1], a1 / a1[:, 0:1])
        o_ref[0, :, hp * LANES:(hp + 1) * LANES] = o.astype(o_ref.dtype)


def _fox_prompt(qkv16, cum, cum_t, blk):
    bsz, t, _ = qkv16.shape
    nb = t // blk
    ng = t // LANES
    ck = cum_t.reshape(bsz, N_HEADS, ng, LANES).transpose(0, 2, 1, 3)
    return pl.pallas_call(
        functools.partial(_fox_prompt_kernel, blk=blk),
        grid=(bsz, nb),
        in_specs=[pl.BlockSpec((1, blk, A_WIDTH), lambda b, i: (b, i, 0)),
                  pl.BlockSpec((1, t, A_WIDTH), lambda b, i: (b, 0, 1)),
                  pl.BlockSpec((1, t, A_WIDTH), lambda b, i: (b, 0, 2)),
                  pl.BlockSpec((1, blk, LANES), lambda b, i: (b, i, 0)),
                  pl.BlockSpec((1, ng, N_HEADS, LANES), lambda b, i: (b, 0, 0, 0))],
        out_specs=pl.BlockSpec((1, blk, A_WIDTH), lambda b, i: (b, i, 0)),
        out_shape=jax.ShapeDtypeStruct((bsz, t, A_WIDTH), BF16),
        scratch_shapes=[pltpu.VMEM((N_HEADS, blk, LANES), BF16),
                        pltpu.VMEM((N_HEADS, blk, LANES), F32),
                        pltpu.VMEM((N_HEADS, blk, LANES), F32),
                        pltpu.VMEM((N_HEADS, blk, LANES), F32)],
        compiler_params=_params(("arbitrary", "arbitrary")),
        name="fox_prompt",
    )(qkv16, qkv16, qkv16, cum, ck)


def _fox_sample_kernel(qkv_ref, cq_ref, ckn_ref, lf_ref, kt_ref, vt_ref, o_ref, *, t, plen):
    lane = lax.broadcasted_iota(jnp.int32, (t, LANES), 1)
    lf = lf_ref[0]
    pos = lax.broadcasted_iota(jnp.int32, lf.shape, 1)
    suffix = lf
    d = 1
    while d < plen:
        suffix = suffix + jnp.where(pos + d < plen, pltpu.roll(suffix, plen - d, axis=1), 0.0)
        d *= 2
    ckc = lf - suffix
    rows = lax.broadcasted_iota(jnp.int32, (t, t), 0)
    cols = lax.broadcasted_iota(jnp.int32, (t, t), 1)
    visible = cols <= rows

    def per_head(s, bias_ref_rows, mask):
        halves = []
        for hh in range(2):
            part = s[hh * t:(hh + 1) * t, :] - bias_ref_rows[hh] * LOG2E
            halves.append(jnp.where(visible, part, NEG) if mask else part)
        return jnp.concatenate(halves, axis=0)

    for hp in range(N_HEADS // 2):
        h0 = 2 * hp
        sl = slice(hp * LANES, (hp + 1) * LANES)
        q = qkv_ref[0, :, hp * LANES:(hp + 1) * LANES]
        zero = jnp.zeros_like(q)
        qs = jnp.concatenate([jnp.where(lane < HEAD_DIM, q, zero),
                              jnp.where(lane >= HEAD_DIM, q, zero)], axis=0)
        kn16 = qkv_ref[0, :, A_WIDTH + hp * LANES:A_WIDTH + (hp + 1) * LANES]
        vn16 = qkv_ref[0, :, 2 * A_WIDTH + hp * LANES:2 * A_WIDTH + (hp + 1) * LANES]
        cq = jnp.concatenate([cq_ref[0, :, h0:h0 + 1], cq_ref[0, :, h0 + 1:h0 + 2]], axis=0) * LOG2E
        kt16 = kt_ref[0, h0:h0 + 2].reshape(LANES, plen).astype(BF16)
        vt16 = vt_ref[0, h0:h0 + 2].reshape(LANES, plen).astype(BF16)
        s_new = per_head(_dot_nt(qs, kn16), [ckn_ref[0, h0:h0 + 1, :], ckn_ref[0, h0 + 1:h0 + 2, :]], True)
        s_old = per_head(_dot(qs, kt16), [ckc[h0:h0 + 1, :], ckc[h0 + 1:h0 + 2, :]], False)
        m = jnp.maximum(jnp.max(s_new, axis=1, keepdims=True), jnp.max(s_old, axis=1, keepdims=True)) + cq
        shift = m - cq
        p_new = jnp.exp2(s_new - shift)
        p_old = jnp.exp2(s_old - shift)
        l = jnp.sum(p_new, axis=1, keepdims=True) + jnp.sum(p_old, axis=1, keepdims=True)
        acc = _dot(p_new.astype(BF16), vn16) + _dot_nt(p_old.astype(BF16), vt16)
        o = acc / l
        o_ref[0, :, sl] = jnp.where(lane < HEAD_DIM, o[:t, :], o[t:, :]).astype(o_ref.dtype)


def _fox_sample(qkv16, cum, cum_t, kt, vt, lf_t):
    bsz, t, _ = qkv16.shape
    plen = kt.shape[-1]
    per_b = lambda shape: pl.BlockSpec((1,) + shape, lambda b: (b,) + (0,) * len(shape))
    return pl.pallas_call(
        functools.partial(_fox_sample_kernel, t=t, plen=plen),
        grid=(bsz,),
        in_specs=[per_b((t, 3 * A_WIDTH)), per_b((t, LANES)), per_b((N_HEADS, t)),
                  per_b((N_HEADS, plen)),
                  per_b((N_HEADS, HEAD_DIM, plen)), per_b((N_HEADS, HEAD_DIM, plen))],
        out_specs=per_b((t, A_WIDTH)),
        out_shape=jax.ShapeDtypeStruct((bsz, t, A_WIDTH), BF16),
        compiler_params=_params(("arbitrary",)),
        name="fox_sample",
    )(qkv16, cum, cum_t, lf_t, kt, vt)


def _shift_rows(h, halo, d):
    rolled = pltpu.roll(h, d, axis=0)
    rowid = lax.broadcasted_iota(jnp.int32, halo.shape, 0)
    head = jnp.where(rowid < d, pltpu.roll(halo, d, axis=0), rolled[:HALO, :])
    return jnp.concatenate([head, rolled[HALO:, :]], axis=0)


def _linear_scan(a, u, inits, lb):
    rows, width = a.shape
    ngroups = rows // SUBLANES
    a3 = a.reshape(ngroups, SUBLANES, width)
    u3 = u.reshape(ngroups, SUBLANES, width)
    sub = lax.broadcasted_iota(jnp.int32, a3.shape, 1)
    d = 1
    while d < SUBLANES:
        inside = sub >= d
        u3 = jnp.where(inside, a3 * pltpu.roll(u3, d, axis=1) + u3, u3)
        a3 = jnp.where(inside, a3 * pltpu.roll(a3, d, axis=1), a3)
        d *= 2
    out = []
    carry = None
    for g in range(ngroups):
        if g % (lb // SUBLANES) == 0:
            carry = inits[g // (lb // SUBLANES)]
        hg = u3[g] + a3[g] * carry
        out.append(hg)
        carry = hg[SUBLANES - 1:, :]
    return jnp.concatenate(out, axis=0)


def _cumsum_rows(x, inits, lb):
    rows, width = x.shape
    ngroups = rows // SUBLANES
    x3 = x.reshape(ngroups, SUBLANES, width)
    sub = lax.broadcasted_iota(jnp.int32, x3.shape, 1)
    d = 1
    while d < SUBLANES:
        x3 = x3 + jnp.where(sub >= d, pltpu.roll(x3, d, axis=1), 0.0)
        d *= 2
    out = []
    carry = None
    for g in range(ngroups):
        if g % (lb // SUBLANES) == 0:
            carry = inits[g // (lb // SUBLANES)]
        cg = x3[g] + carry
        out.append(cg)
        carry = cg[SUBLANES - 1:, :]
    return jnp.concatenate(out, axis=0)


def _ffn_kernel(*refs, tm, nseq, bps, tn, mix):
    if mix:
        x_ref, attn_ref, gated_ref, woa_ref, wob_ref, mg_ref, mb_ref = refs[:7]
        refs = refs[7:]
    else:
        x_ref = refs[0]
        refs = refs[1:]
    st_ref, wup_ref, taps_ref, wd_ref, g_ref, b_ref, y_ref, so_ref, a_ref, carry_ref = refs
    i = pl.program_id(0)
    lb = tm // nseq
    x = x_ref[...]
    if mix:
        mixed = _dot(attn_ref[...], woa_ref[...]) + _dot(gated_ref[...], wob_ref[...])
        x = _layer_norm(ALPHA * x + mixed, mg_ref[...], mb_ref[...])
    x16 = x.astype(BF16)
    first = (i % bps) == 0

    def conv_branch(h, cols):
        taps = taps_ref[:, cols]
        outs = []
        for s in range(nseq):
            hs = h[s * lb:(s + 1) * lb, :]
            if bps > 1:
                halo = jnp.where(first, st_ref[0, :, cols], carry_ref[:, cols])
            else:
                halo = st_ref[s, :, cols]
            so_ref[s, :, cols] = hs[lb - HALO:, :]
            y = taps[3:4, :] + taps[2:3, :] * hs
            y = y + taps[1:2, :] * _shift_rows(hs, halo, 1)
            y = y + taps[0:1, :] * _shift_rows(hs, halo, 2)
            outs.append(y)
        if bps > 1:
            carry_ref[:, cols] = h[tm - HALO:, :]
        return outs

    for c in range(D_FF // tn):
        cg = slice(c * tn, (c + 1) * tn)
        cu = slice(D_FF + c * tn, D_FF + (c + 1) * tn)
        yg = conv_branch(_dot(x16, wup_ref[:, cg]), cg)
        yu = conv_branch(_dot(x16, wup_ref[:, cu]), cu)
        for s in range(nseq):
            a_ref[s * lb:(s + 1) * lb, cg] = (jax.nn.gelu(yg[s]) * yu[s]).astype(BF16)
    y = _dot(a_ref[...], wd_ref[...])
    y_ref[...] = _layer_norm(ALPHA * x + y, g_ref[...], b_ref[...])


def _ffn(x2d, state, seq_len, prm, mix=None):
    n = x2d.shape[0]
    tm, nseq, bps = _plan(n, seq_len)
    nblk = n // tm
    bsz = state.shape[0]
    st = jnp.pad(state, ((0, 0), (HALO - (FFN_CONV - 1), 0), (0, 0)))
    row = pl.BlockSpec((tm, D_MODEL), lambda i: (i, 0))
    consts = [prm[k] for k in ("w_up", "taps", "w_d", "ln_g", "ln_b")]
    lead, lead_specs = [x2d], [row]
    if mix is not None:
        attn16, gated16, eprm = mix
        mix_consts = [eprm[k] for k in ("wo_a", "wo_b", "ln_g", "ln_b")]
        lead += [attn16, gated16] + mix_consts
        lead_specs += [pl.BlockSpec((tm, A_WIDTH), lambda i: (i, 0)), pl.BlockSpec((tm, B_WIDTH), lambda i: (i, 0))]
        lead_specs += [_const_spec(c.shape) for c in mix_consts]
    y, so = pl.pallas_call(
        functools.partial(_ffn_kernel, tm=tm, nseq=nseq, bps=bps, tn=FFN_CHUNK, mix=mix is not None),
        grid=(nblk,),
        in_specs=lead_specs + [pl.BlockSpec((nseq, HALO, 2 * D_FF), lambda i: (i // bps, 0, 0))]
                 + [_const_spec(c.shape) for c in consts],
        out_specs=(row, pl.BlockSpec((nseq, HALO, 2 * D_FF), lambda i: (i, 0, 0))),
        out_shape=(jax.ShapeDtypeStruct((n, D_MODEL), F32),
                   jax.ShapeDtypeStruct((nblk * nseq, HALO, 2 * D_FF), F32)),
        scratch_shapes=[pltpu.VMEM((tm, D_FF), BF16),
                        pltpu.VMEM((HALO, 2 * D_FF), F32)],
        compiler_params=_params(("arbitrary",)),
        name="conv_ffn",
    )(*lead, st, *consts)
    so = so.reshape(bsz, -1, HALO, 2 * D_FF)[:, -1]
    return y, so[:, HALO - (FFN_CONV - 1):, :]


def _odd_kernel(x_ref, wu_ref, wv_ref, sg_ref, sb_ref, ws_ref, bias_ref, wo_ref, g_ref, b_ref,
                y_ref, *rest, tm, lc, emit_v):
    gate_ref = rest[-1]
    x = x_ref[...]
    x16 = x.astype(BF16)
    u = jax.nn.gelu(_dot(x16, wu_ref[...]))
    v = _layer_norm(jax.nn.gelu(_dot(x16, wv_ref[...])), sg_ref[...], sb_ref[...])
    if emit_v:
        rest[0][...] = v
    v16 = v.astype(BF16)
    gw = C_WIDTH // C_GROUPS
    for c in range(tm // lc):
        rows = slice(c * lc, (c + 1) * lc)
        for g in range(C_GROUPS):
            cols = slice(g * gw, (g + 1) * gw)
            mixed = _dot(ws_ref[g], v16[rows, cols]) + bias_ref[:, cols]
            gate_ref[rows, cols] = (u[rows, cols] * mixed).astype(BF16)
    y = _dot(gate_ref[...], wo_ref[...])
    y_ref[...] = _layer_norm(ALPHA * x + y, g_ref[...], b_ref[...])


def _odd(x2d, seq_len, prm, emit_v):
    n = x2d.shape[0]
    tm = min(ROW_BLOCK, n)
    lc = min(seq_len, C_LEN)
    gw = C_WIDTH // C_GROUPS
    ws16 = (prm["sgu_w"][:, :lc, :lc] * jnp.tril(jnp.ones((lc, lc), F32))).astype(BF16)
    bias = jnp.repeat(prm["sgu_bias"][:, :lc].T, gw, axis=1)
    consts = [prm["w_u"], prm["w_v"], prm["sgu_g"], prm["sgu_b"], ws16, bias,
              prm["w_o"], prm["ln_g"], prm["ln_b"]]
    row = pl.BlockSpec((tm, D_MODEL), lambda i: (i, 0))
    n_out = 2 if emit_v else 1
    return pl.pallas_call(
        functools.partial(_odd_kernel, tm=tm, lc=lc, emit_v=emit_v),
        grid=(n // tm,),
        in_specs=[row] + [_const_spec(c.shape) for c in consts],
        out_specs=(row, pl.BlockSpec((tm, C_WIDTH), lambda i: (i, 0)))[:n_out],
        out_shape=(jax.ShapeDtypeStruct((n, D_MODEL), F32),
                   jax.ShapeDtypeStruct((n, C_WIDTH), F32))[:n_out],
        scratch_shapes=[pltpu.VMEM((tm, C_WIDTH), BF16)],
        compiler_params=_params(("arbitrary",)),
        name="odd_mixer",
    )(x2d, *consts)


def _row(v):
    return v.astype(F32)[None, :]


def _block_diag(w):
    nb, bd, _ = w.shape
    eye = jnp.eye(nb, dtype=w.dtype)
    return jnp.einsum("ncd,nm->ncmd", w, eye).reshape(nb * bd, nb * bd)


def _even_params(w_in, b_f, rg_conv_w, rg_conv_b, rg_wa, rg_ba, rg_wx, rg_bx, rg_lam, w_out, ln_g, ln_b):
    nf = 3 * A_WIDTH
    wf = jnp.pad(w_in[:, nf:nf + N_HEADS], ((0, 0), (0, LANES - N_HEADS)))
    return {
        "wqkv": w_in[:, :nf].astype(BF16),
        "wf": wf.astype(BF16),
        "wxg": w_in[:, nf + N_HEADS:].astype(BF16),
        "bf": jnp.pad(b_f, (0, LANES - N_HEADS))[None, :],
        "rg_cw": rg_conv_w, "rg_cb": _row(rg_conv_b),
        "rg_wa": _block_diag(rg_wa).astype(BF16), "rg_ba": _row(rg_ba),
        "rg_wx": _block_diag(rg_wx).astype(BF16), "rg_bx": _row(rg_bx),
        "rg_lam": _row(rg_lam),
        "wo_a": w_out[:A_WIDTH].astype(BF16), "wo_b": w_out[A_WIDTH:].astype(BF16),
        "ln_g": _row(ln_g), "ln_b": _row(ln_b),
    }


def _ffn_params(w_up, conv_w, conv_b, w_down, ln_g, ln_b):
    taps = jnp.concatenate([conv_w, conv_b[None, :],
                            jnp.zeros((HALO - FFN_CONV - 1, 2 * D_FF), F32)], axis=0)
    return {"w_up": w_up.astype(BF16), "taps": taps, "w_d": w_down.astype(BF16),
            "ln_g": _row(ln_g), "ln_b": _row(ln_b)}


def _odd_params(w_in, sgu_g, sgu_b, sgu_w, sgu_bias, w_out, ln_g, ln_b):
    return {
        "w_u": w_in[:, :C_WIDTH].astype(BF16), "w_v": w_in[:, C_WIDTH:].astype(BF16),
        "sgu_g": _row(sgu_g), "sgu_b": _row(sgu_b), "sgu_w": sgu_w, "sgu_bias": sgu_bias,
        "w_o": w_out.astype(BF16), "ln_g": _row(ln_g), "ln_b": _row(ln_b),
    }


def _even_layer(x3d, past, prm):
    bsz, t, _ = x3d.shape
    x2d = x3d.reshape(bsz * t, D_MODEL)
    assert t >= HALO
    if past is None:
        conv_s = jnp.zeros((bsz, RG_CONV - 1, B_WIDTH), F32)
        h0 = jnp.zeros((bsz, B_WIDTH), F32)
    else:
        k_c, v_c, lf_c, conv_s, h0 = past
    kv_t = t >= ROW_BLOCK
    qkv16, k, v, logf, cum, gated16, xtail, hl = _even_in(x2d, t, conv_s, h0, prm, kv_t)
    qkv16 = qkv16.reshape(bsz, t, 3 * A_WIDTH)
    cum3 = cum.reshape(bsz, t, LANES)
    cum_t = cum3[:, :, :N_HEADS].transpose(0, 2, 1)
    if past is None:
        attn = _fox_prompt(qkv16, cum3, cum_t, ROW_BLOCK)
    else:
        attn = _fox_sample(qkv16, cum3, cum_t, k_c.transpose(0, 2, 3, 1), v_c.transpose(0, 2, 3, 1),
                           lf_c.transpose(0, 2, 1))
    if kv_t:
        new_k = k.reshape(bsz, N_HEADS, HEAD_DIM, t).transpose(0, 3, 1, 2)
        new_v = v.reshape(bsz, N_HEADS, HEAD_DIM, t).transpose(0, 3, 1, 2)
    else:
        new_k = k.reshape(bsz, t, N_HEADS, HEAD_DIM)
        new_v = v.reshape(bsz, t, N_HEADS, HEAD_DIM)
    new_logf = logf.reshape(bsz, t, LANES)[:, :, :N_HEADS]
    h_last = hl.reshape(bsz, -1, B_WIDTH)[:, -1, :]
    conv_new = xtail.reshape(bsz, -1, HALO, B_WIDTH)[:, -1, HALO - (RG_CONV - 1):, :]
    return (attn.reshape(bsz * t, A_WIDTH), gated16), (new_k, new_v, new_logf, conv_new, h_last)


def _ffn_layer(x3d, state, prm, mix=None):
    bsz, t, _ = x3d.shape
    y, new_state = _ffn(x3d.reshape(bsz * t, D_MODEL), state, t, prm, mix)
    return y.reshape(bsz, t, D_MODEL), new_state


def _odd_layer(x3d, prm, emit_v):
    bsz, t, _ = x3d.shape
    outs = _odd(x3d.reshape(bsz * t, D_MODEL), t, prm, emit_v)
    v = outs[1].reshape(bsz, t, C_WIDTH) if emit_v else None
    return outs[0].reshape(bsz, t, D_MODEL), v


def kernel(x_prompt, x_sample, cache_k, cache_v, cache_logf, state_rglru_conv, state_rglru_h, state_ffn_conv, w_in_e, b_f, rg_conv_w, rg_conv_b, rg_wa, rg_ba, rg_wx, rg_bx, rg_lam, w_out_e, w_in_o, sgu_g, sgu_b, sgu_w, sgu_bias, w_out_o, ln_mix_g, ln_mix_b, ln_ffn_g, ln_ffn_b, ffn_w_up, ffn_conv_w, ffn_conv_b, ffn_w_down):
    xp, xs = x_prompt, x_sample
    even_p, even_s, ffn_p, ffn_s, sgu_v = [], [], [], [], []
    for layer in range(DEPTH):
        if layer % 2 == 0:
            e = layer // 2
            prm = _even_params(w_in_e[e], b_f[e], rg_conv_w[e], rg_conv_b[e], rg_wa[e], rg_ba[e],
                               rg_wx[e], rg_bx[e], rg_lam[e], w_out_e[e],
                               ln_mix_g[layer], ln_mix_b[layer])
            branches_p, st_p = _even_layer(xp, None, prm)
            past = (cache_k[e], cache_v[e], cache_logf[e], state_rglru_conv[e], state_rglru_h[e])
            branches_s, st_s = _even_layer(xs, past, prm)
            mix_p, mix_s = branches_p + (prm,), branches_s + (prm,)
            even_p.append(st_p)
            even_s.append(st_s)
        else:
            o = layer // 2
            prm = _odd_params(w_in_o[o], sgu_g[o], sgu_b[o], sgu_w[o], sgu_bias[o], w_out_o[o],
                              ln_mix_g[layer], ln_mix_b[layer])
            xp, _ = _odd_layer(xp, prm, emit_v=False)
            xs, sv = _odd_layer(xs, prm, emit_v=True)
            sgu_v.append(sv)
            mix_p = mix_s = None
        fprm = _ffn_params(ffn_w_up[layer], ffn_conv_w[layer], ffn_conv_b[layer], ffn_w_down[layer],
                           ln_ffn_g[layer], ln_ffn_b[layer])
        zero_fc = jnp.zeros((xp.shape[0], FFN_CONV - 1, 2 * D_FF), F32)
        xp, fcp = _ffn_layer(xp, zero_fc, fprm, mix_p)
        xs, fcs = _ffn_layer(xs, state_ffn_conv[layer], fprm, mix_s)
        ffn_p.append(fcp)
        ffn_s.append(fcs)
    stack = lambda lst, idx: jnp.stack([item[idx] for item in lst])
    return (xp, xs,
            stack(even_p, 0), stack(even_p, 1), stack(even_p, 2),
            stack(even_s, 0), stack(even_s, 1), stack(even_s, 2),
            stack(even_p, 3), stack(even_p, 4), stack(even_s, 3), stack(even_s, 4),
            jnp.stack(ffn_p), jnp.stack(ffn_s), jnp.stack(sgu_v))
```

```python
import functools

import jax
import jax.numpy as jnp
from jax import lax
from jax.experimental import pallas as pl
from jax.experimental.pallas import tpu as pltpu

F32 = jnp.float32
BF16 = jnp.bfloat16

D_MODEL = 1024
N_HEADS = 8
HEAD_DIM = 64
A_WIDTH = N_HEADS * HEAD_DIM
B_WIDTH = 512
RG_CONV = 4
RG_C = 8.0
C_WIDTH = 1024
C_GROUPS = 8
C_LEN = 128
D_FF = 2816
FFN_CONV = 3
DEPTH = 2
ALPHA = (2 * DEPTH) ** 0.25
LN_EPS = 1e-5

LANES = 128
SUBLANES = 8
HALO = SUBLANES
NEG = -1e30
LOG2E = 1.4426950408889634
ROW_BLOCK = 512
FFN_CHUNK = 256
VMEM_LIMIT = 48 * 1024 * 1024


def _layer_norm(x, g, b):
    mu = jnp.mean(x, axis=-1, keepdims=True)
    xc = x - mu
    var = jnp.mean(xc * xc, axis=-1, keepdims=True)
    return xc * lax.rsqrt(var + LN_EPS) * g + b


def _log_sigmoid(x):
    return jnp.minimum(x, 0.0) - jnp.log1p(jnp.exp(-jnp.abs(x)))


def _softplus(x):
    return jnp.maximum(x, 0.0) + jnp.log1p(jnp.exp(-jnp.abs(x)))


def _dot(a, b):
    return jnp.dot(a, b, preferred_element_type=F32)


def _dot_nt(a, b):
    return lax.dot_general(a, b, (((1,), (1,)), ((), ())), preferred_element_type=F32)


def _const_spec(shape):
    nd = len(shape)
    return pl.BlockSpec(shape, lambda *_: (0,) * nd, pipeline_mode=pl.Buffered(1))


def _params(sem):
    return pltpu.CompilerParams(dimension_semantics=sem, vmem_limit_bytes=VMEM_LIMIT)


def _plan(n_rows, seq_len):
    tm = min(ROW_BLOCK, n_rows)
    nseq = max(1, tm // seq_len)
    bps = max(1, seq_len // tm)
    assert n_rows % tm == 0 and (seq_len % tm == 0 or tm % seq_len == 0)
    return tm, nseq, bps


def _even_in_kernel(x_ref, wqkv_ref, wf_ref, wxg_ref, bf_ref, cs_ref, h0_ref,
                    cw_ref, cb_ref, wa_ref, ba_ref, wx_ref, bx_ref, lam_ref,
                    qkv16_ref, k_ref, v_ref, logf_ref, cum_ref, gated_ref, xtail_ref, hl_ref,
                    carry_ref, ccarry_ref, hcarry_ref, *, tm, nseq, bps, kv_t):
    i = pl.program_id(0)
    lb = tm // nseq
    first = (i % bps) == 0
    x16 = x_ref[...].astype(BF16)
    xg = _dot(x16, wxg_ref[...])
    half = B_WIDTH // 2
    for cols in (slice(0, half), slice(half, B_WIDTH)):
        xb = xg[:, cols]
        if bps > 1:
            halos = [jnp.where(first, cs_ref[0, :, cols], ccarry_ref[:, cols])]
            inits = [jnp.where(first, h0_ref[0, :, cols], hcarry_ref[:, cols])]
            ccarry_ref[:, cols] = xb[tm - HALO:, :]
        else:
            halos = [cs_ref[s, :, cols] for s in range(nseq)]
            inits = [h0_ref[s, :, cols] for s in range(nseq)]
        pieces = []
        for s in range(nseq):
            xs = xb[s * lb:(s + 1) * lb, :]
            xtail_ref[s, :, cols] = xs[lb - HALO:, :]
            xc = cb_ref[:, cols] + cw_ref[RG_CONV - 1:RG_CONV, cols] * xs
            for d in range(1, RG_CONV):
                xc = xc + cw_ref[RG_CONV - 1 - d:RG_CONV - d, cols] * _shift_rows(xs, halos[s], d)
            pieces.append(xc)
        xc = pieces[0] if nseq == 1 else jnp.concatenate(pieces, axis=0)
        xc16 = xc.astype(BF16)
        gate_r = jax.nn.sigmoid(_dot(xc16, wa_ref[cols, cols]) + ba_ref[:, cols])
        gate_i = jax.nn.sigmoid(_dot(xc16, wx_ref[cols, cols]) + bx_ref[:, cols])
        log_a = -RG_C * gate_r * _softplus(-lam_ref[:, cols])
        a = jnp.exp(log_a)
        u = jnp.sqrt(jnp.tanh(-log_a) * (a * a + 1.0)) * (gate_i * xc)
        h = _linear_scan(a, u, inits, lb)
        if bps > 1:
            hcarry_ref[:, cols] = h[tm - 1:tm, :]
        for s in range(nseq):
            hl_ref[0, s:s + 1, cols] = h[(s + 1) * lb - 1:(s + 1) * lb, :]
        gb = xg[:, B_WIDTH + cols.start:B_WIDTH + cols.stop]
        gated_ref[:, cols] = (jax.nn.gelu(gb) * h).astype(BF16)

    qkv = _dot(x16, wqkv_ref[...])
    qkv16_ref[:, :A_WIDTH] = (qkv[:, :A_WIDTH] * (HEAD_DIM ** -0.5 * LOG2E)).astype(BF16)
    qkv16_ref[:, A_WIDTH:] = qkv[:, A_WIDTH:].astype(BF16)
    if kv_t:
        k_ref[0] = qkv[:, A_WIDTH:2 * A_WIDTH].T
        v_ref[0] = qkv[:, 2 * A_WIDTH:].T
    else:
        k_ref[...] = qkv[:, A_WIDTH:2 * A_WIDTH]
        v_ref[...] = qkv[:, 2 * A_WIDTH:]
    logf = _log_sigmoid(_dot(x16, wf_ref[...]) + bf_ref[...])
    logf_ref[...] = logf
    zero = jnp.zeros((1, LANES), F32)
    if bps > 1:
        cs = _cumsum_rows(logf, [jnp.where(first, zero, carry_ref[...])], lb)
        carry_ref[...] = cs[tm - 1:tm, :]
    else:
        cs = _cumsum_rows(logf, [zero] * nseq, lb)
    cum_ref[...] = cs


def _even_in(x2d, seq_len, conv_state, h0, prm, kv_t):
    n = x2d.shape[0]
    tm, nseq, bps = _plan(n, seq_len)
    nblk = n // tm
    cs = jnp.pad(conv_state, ((0, 0), (HALO - (RG_CONV - 1), 0), (0, 0)))
    h03 = h0.astype(F32)[:, None, :]
    state = lambda r: pl.BlockSpec((nseq, r, B_WIDTH), lambda i: (i // bps, 0, 0))
    consts = [prm[k] for k in ("wqkv", "wf", "wxg", "bf")]
    rg = [prm[k] for k in ("rg_cw", "rg_cb", "rg_wa", "rg_ba", "rg_wx", "rg_bx", "rg_lam")]
    if kv_t:
        assert nseq == 1
        kv_shape = jax.ShapeDtypeStruct((n // seq_len, A_WIDTH, seq_len), F32)
        kv_spec = pl.BlockSpec((1, A_WIDTH, tm), lambda i: (i // bps, 0, i % bps))
    else:
        kv_shape = jax.ShapeDtypeStruct((n, A_WIDTH), F32)
        kv_spec = pl.BlockSpec((tm, A_WIDTH), lambda i: (i, 0))
    row = lambda w: pl.BlockSpec((tm, w), lambda i: (i, 0))
    out_shapes = (
        jax.ShapeDtypeStruct((n, 3 * A_WIDTH), BF16),
        kv_shape,
        kv_shape,
        jax.ShapeDtypeStruct((n, LANES), F32),
        jax.ShapeDtypeStruct((n, LANES), F32),
        jax.ShapeDtypeStruct((n, B_WIDTH), BF16),
        jax.ShapeDtypeStruct((nblk * nseq, HALO, B_WIDTH), F32),
        jax.ShapeDtypeStruct((nblk, nseq, B_WIDTH), F32),
    )
    return pl.pallas_call(
        functools.partial(_even_in_kernel, tm=tm, nseq=nseq, bps=bps, kv_t=kv_t),
        grid=(nblk,),
        in_specs=[row(D_MODEL)] + [_const_spec(c.shape) for c in consts]
                 + [state(HALO), state(1)] + [_const_spec(c.shape) for c in rg],
        out_specs=(row(3 * A_WIDTH), kv_spec, kv_spec, row(LANES), row(LANES), row(B_WIDTH),
                   pl.BlockSpec((nseq, HALO, B_WIDTH), lambda i: (i, 0, 0)),
                   pl.BlockSpec((1, nseq, B_WIDTH), lambda i: (i, 0, 0))),
        out_shape=out_shapes,
        scratch_shapes=[pltpu.VMEM((1, LANES), F32),
                        pltpu.VMEM((HALO, B_WIDTH), F32),
                        pltpu.VMEM((1, B_WIDTH), F32)],
        compiler_params=_params(("arbitrary",)),
        name="even_in_proj",
    )(x2d, *consts, cs, h03, *rg)


def _fox_prompt_kernel(q_ref, k_ref, v_ref, cq_ref, ck_ref, o_ref,
                       qm_ref, cqr_ref, m_ref, acc_ref, *, blk):
    i = pl.program_id(1)
    lane = lax.broadcasted_iota(jnp.int32, (blk, LANES), 1)
    ngrp = blk // LANES

    for h in range(N_HEADS):
        hp, hh = divmod(h, 2)
        q = q_ref[0, :, hp * LANES:(hp + 1) * LANES]
        keep = (lane < HEAD_DIM) if hh == 0 else (lane >= HEAD_DIM)
        qm_ref[h] = jnp.where(keep, q, jnp.zeros_like(q))
        cqr_ref[h] = jnp.broadcast_to(cq_ref[0, :, h:h + 1] * LOG2E, (blk, LANES))
    m_ref[...] = jnp.full(m_ref.shape, NEG, F32)
    acc_ref[...] = jnp.zeros_like(acc_ref)

    def update(kg0, ngrp, diagonal):
        koff = pl.multiple_of(kg0 * LANES, LANES)
        vlane = lax.broadcasted_iota(jnp.int32, (ngrp * LANES, LANES), 1)
        if diagonal:
            rows = lax.broadcasted_iota(jnp.int32, (blk, LANES), 0)
            own = ngrp - blk // LANES
            visible = [None] * own + [lane + g * LANES <= rows for g in range(blk // LANES)]
        for h in range(N_HEADS):
            hp, hh = divmod(h, 2)
            k16 = k_ref[0, pl.ds(koff, ngrp * LANES), hp * LANES:(hp + 1) * LANES]
            v16 = v_ref[0, pl.ds(koff, ngrp * LANES), hp * LANES:(hp + 1) * LANES]
            vkeep = (vlane < HEAD_DIM) if hh == 0 else (vlane >= HEAD_DIM)
            v16 = jnp.where(vkeep, v16, jnp.ones_like(v16))
            s = _dot_nt(qm_ref[h], k16)
            grp = [s[:, g * LANES:(g + 1) * LANES] - ck_ref[0, kg0 + g, h:h + 1, :] * LOG2E
                   for g in range(ngrp)]
            if diagonal:
                grp = [g if vis is None else jnp.where(vis, g, NEG) for vis, g in zip(visible, grp)]
            mc = grp[0]
            for g in grp[1:]:
                mc = jnp.maximum(mc, g)
            cqr = cqr_ref[h]
            m_old = m_ref[h]
            m_new = jnp.maximum(m_old, jnp.max(mc, axis=1, keepdims=True) + cqr)
            shift = m_new - cqr
            p16 = jnp.concatenate([jnp.exp2(g - shift).astype(BF16) for g in grp], axis=1)
            acc_ref[h] = jnp.exp2(m_old - m_new) * acc_ref[h] + _dot(p16, v16)
            m_ref[h] = m_new

    def body(j, carry):
        update(j * (2 * ngrp), 2 * ngrp, False)
        return carry

    lax.fori_loop(0, i // 2, body, 0)

    @pl.when(i % 2 == 1)
    def _():
        update((i - 1) * ngrp, 2 * ngrp, True)

    @pl.when(i % 2 == 0)
    def _():
        update(i * ngrp, ngrp, True)

    for hp in range(N_HEADS // 2):
        a0 = acc_ref[2 * hp]
        a1 = acc_ref[2 * hp + 1]
        o = jnp.where(lane < HEAD_DIM, a0 / a0[:, HEAD_DIM:HEAD_DIM + 1], a1 / a1[:, 0:1])
        o_ref[0, :, hp * LANES:(hp + 1) * LANES] = o.astype(o_ref.dtype)


def _fox_prompt(qkv16, cum, cum_t, blk):
    bsz, t, _ = qkv16.shape
    nb = t // blk
    ng = t // LANES
    ck = cum_t.reshape(bsz, N_HEADS, ng, LANES).transpose(0, 2, 1, 3)
    return pl.pallas_call(
        functools.partial(_fox_prompt_kernel, blk=blk),
        grid=(bsz, nb),
        in_specs=[pl.BlockSpec((1, blk, A_WIDTH), lambda b, i: (b, i, 0)),
                  pl.BlockSpec((1, t, A_WIDTH), lambda b, i: (b, 0, 1)),
                  pl.BlockSpec((1, t, A_WIDTH), lambda b, i: (b, 0, 2)),
                  pl.BlockSpec((1, blk, LANES), lambda b, i: (b, i, 0)),
                  pl.BlockSpec((1, ng, N_HEADS, LANES), lambda b, i: (b, 0, 0, 0))],
        out_specs=pl.BlockSpec((1, blk, A_WIDTH), lambda b, i: (b, i, 0)),
        out_shape=jax.ShapeDtypeStruct((bsz, t, A_WIDTH), BF16),
        scratch_shapes=[pltpu.VMEM((N_HEADS, blk, LANES), BF16),
                        pltpu.VMEM((N_HEADS, blk, LANES), F32),
                        pltpu.VMEM((N_HEADS, blk, LANES), F32),
                        pltpu.VMEM((N_HEADS, blk, LANES), F32)],
        compiler_params=_params(("arbitrary", "arbitrary")),
        name="fox_prompt",
    )(qkv16, qkv16, qkv16, cum, ck)


def _fox_sample_kernel(qkv_ref, cq_ref, ckn_ref, lf_ref, kt_ref, vt_ref, o_ref, *, t, plen):
    lane = lax.broadcasted_iota(jnp.int32, (t, LANES), 1)
    lf = lf_ref[0]
    pos = lax.broadcasted_iota(jnp.int32, lf.shape, 1)
    suffix = lf
    d = 1
    while d < plen:
        suffix = suffix + jnp.where(pos + d < plen, pltpu.roll(suffix, plen - d, axis=1), 0.0)
        d *= 2
    ckc = lf - suffix
    rows = lax.broadcasted_iota(jnp.int32, (t, t), 0)
    cols = lax.broadcasted_iota(jnp.int32, (t, t), 1)
    visible = cols <= rows

    def per_head(s, bias_ref_rows, mask):
        halves = []
        for hh in range(2):
            part = s[hh * t:(hh + 1) * t, :] - bias_ref_rows[hh] * LOG2E
            halves.append(jnp.where(visible, part, NEG) if mask else part)
        return jnp.concatenate(halves, axis=0)

    for hp in range(N_HEADS // 2):
        h0 = 2 * hp
        sl = slice(hp * LANES, (hp + 1) * LANES)
        q = qkv_ref[0, :, hp * LANES:(hp + 1) * LANES]
        zero = jnp.zeros_like(q)
        qs = jnp.concatenate([jnp.where(lane < HEAD_DIM, q, zero),
                              jnp.where(lane >= HEAD_DIM, q, zero)], axis=0)
        kn16 = qkv_ref[0, :, A_WIDTH + hp * LANES:A_WIDTH + (hp + 1) * LANES]
        vn16 = qkv_ref[0, :, 2 * A_WIDTH + hp * LANES:2 * A_WIDTH + (hp + 1) * LANES]
        cq = jnp.concatenate([cq_ref[0, :, h0:h0 + 1], cq_ref[0, :, h0 + 1:h0 + 2]], axis=0) * LOG2E
        kt16 = kt_ref[0, h0:h0 + 2].reshape(LANES, plen).astype(BF16)
        vt16 = vt_ref[0, h0:h0 + 2].reshape(LANES, plen).astype(BF16)
        s_new = per_head(_dot_nt(qs, kn16), [ckn_ref[0, h0:h0 + 1, :], ckn_ref[0, h0 + 1:h0 + 2, :]], True)
        s_old = per_head(_dot(qs, kt16), [ckc[h0:h0 + 1, :], ckc[h0 + 1:h0 + 2, :]], False)
        m = jnp.maximum(jnp.max(s_new, axis=1, keepdims=True), jnp.max(s_old, axis=1, keepdims=True)) + cq
        shift = m - cq
        p_new = jnp.exp2(s_new - shift)
        p_old = jnp.exp2(s_old - shift)
        l = jnp.sum(p_new, axis=1, keepdims=True) + jnp.sum(p_old, axis=1, keepdims=True)
        acc = _dot(p_new.astype(BF16), vn16) + _dot_nt(p_old.astype(BF16), vt16)
        o = acc / l
        o_ref[0, :, sl] = jnp.where(lane < HEAD_DIM, o[:t, :], o[t:, :]).astype(o_ref.dtype)


def _fox_sample(qkv16, cum, cum_t, kt, vt, lf_t):
    bsz, t, _ = qkv16.shape
    plen = kt.shape[-1]
    per_b = lambda shape: pl.BlockSpec((1,) + shape, lambda b: (b,) + (0,) * len(shape))
    return pl.pallas_call(
        functools.partial(_fox_sample_kernel, t=t, plen=plen),
        grid=(bsz,),
        in_specs=[per_b((t, 3 * A_WIDTH)), per_b((t, LANES)), per_b((N_HEADS, t)),
                  per_b((N_HEADS, plen)),
                  per_b((N_HEADS, HEAD_DIM, plen)), per_b((N_HEADS, HEAD_DIM, plen))],
        out_specs=per_b((t, A_WIDTH)),
        out_shape=jax.ShapeDtypeStruct((bsz, t, A_WIDTH), BF16),
        compiler_params=_params(("arbitrary",)),
        name="fox_sample",
    )(qkv16, cum, cum_t, lf_t, kt, vt)


def _shift_rows(h, halo, d):
    rolled = pltpu.roll(h, d, axis=0)
    rowid = lax.broadcasted_iota(jnp.int32, halo.shape, 0)
    head = jnp.where(rowid < d, pltpu.roll(halo, d, axis=0), rolled[:HALO, :])
    return jnp.concatenate([head, rolled[HALO:, :]], axis=0)


def _linear_scan(a, u, inits, lb):
    rows, width = a.shape
    ngroups = rows // SUBLANES
    a3 = a.reshape(ngroups, SUBLANES, width)
    u3 = u.reshape(ngroups, SUBLANES, width)
    sub = lax.broadcasted_iota(jnp.int32, a3.shape, 1)
    d = 1
    while d < SUBLANES:
        inside = sub >= d
        u3 = jnp.where(inside, a3 * pltpu.roll(u3, d, axis=1) + u3, u3)
        a3 = jnp.where(inside, a3 * pltpu.roll(a3, d, axis=1), a3)
        d *= 2
    out = []
    carry = None
    for g in range(ngroups):
        if g % (lb // SUBLANES) == 0:
            carry = inits[g // (lb // SUBLANES)]
        hg = u3[g] + a3[g] * carry
        out.append(hg)
        carry = hg[SUBLANES - 1:, :]
    return jnp.concatenate(out, axis=0)


def _cumsum_rows(x, inits, lb):
    rows, width = x.shape
    ngroups = rows // SUBLANES
    x3 = x.reshape(ngroups, SUBLANES, width)
    sub = lax.broadcasted_iota(jnp.int32, x3.shape, 1)
    d = 1
    while d < SUBLANES:
        x3 = x3 + jnp.where(sub >= d, pltpu.roll(x3, d, axis=1), 0.0)
        d *= 2
    out = []
    carry = None
    for g in range(ngroups):
        if g % (lb // SUBLANES) == 0:
            carry = inits[g // (lb // SUBLANES)]
        cg = x3[g] + carry
        out.append(cg)
        carry = cg[SUBLANES - 1:, :]
    return jnp.concatenate(out, axis=0)


def _ffn_kernel(*refs, tm, nseq, bps, tn, mix):
    if mix:
        x_ref, attn_ref, gated_ref, woa_ref, wob_ref, mg_ref, mb_ref = refs[:7]
        refs = refs[7:]
    else:
        x_ref = refs[0]
        refs = refs[1:]
    st_ref, wup_ref, taps_ref, wd_ref, g_ref, b_ref, y_ref, so_ref, a_ref, carry_ref = refs
    i = pl.program_id(0)
    lb = tm // nseq
    x = x_ref[...]
    if mix:
        mixed = _dot(attn_ref[...], woa_ref[...]) + _dot(gated_ref[...], wob_ref[...])
        x = _layer_norm(ALPHA * x + mixed, mg_ref[...], mb_ref[...])
    x16 = x.astype(BF16)
    first = (i % bps) == 0

    def conv_branch(h, cols):
        taps = taps_ref[:, cols]
        outs = []
        for s in range(nseq):
            hs = h[s * lb:(s + 1) * lb, :]
            if bps > 1:
                halo = jnp.where(first, st_ref[0, :, cols], carry_ref[:, cols])
            else:
                halo = st_ref[s, :, cols]
            so_ref[s, :, cols] = hs[lb - HALO:, :]
            y = taps[3:4, :] + taps[2:3, :] * hs
            y = y + taps[1:2, :] * _shift_rows(hs, halo, 1)
            y = y + taps[0:1, :] * _shift_rows(hs, halo, 2)
            outs.append(y)
        if bps > 1:
            carry_ref[:, cols] = h[tm - HALO:, :]
        return outs

    for c in range(D_FF // tn):
        cg = slice(c * tn, (c + 1) * tn)
        cu = slice(D_FF + c * tn, D_FF + (c + 1) * tn)
        yg = conv_branch(_dot(x16, wup_ref[:, cg]), cg)
        yu = conv_branch(_dot(x16, wup_ref[:, cu]), cu)
        for s in range(nseq):
            a_ref[s * lb:(s + 1) * lb, cg] = (jax.nn.gelu(yg[s]) * yu[s]).astype(BF16)
    y = _dot(a_ref[...], wd_ref[...])
    y_ref[...] = _layer_norm(ALPHA * x + y, g_ref[...], b_ref[...])


def _ffn(x2d, state, seq_len, prm, mix=None):
    n = x2d.shape[0]
    tm, nseq, bps = _plan(n, seq_len)
    nblk = n // tm
    bsz = state.shape[0]
    st = jnp.pad(state, ((0, 0), (HALO - (FFN_CONV - 1), 0), (0, 0)))
    row = pl.BlockSpec((tm, D_MODEL), lambda i: (i, 0))
    consts = [prm[k] for k in ("w_up", "taps", "w_d", "ln_g", "ln_b")]
    layer = prm["layer"]
    of_layer = lambda w: pl.BlockSpec((None,) + w.shape[1:], lambda i: (layer, 0, 0), pipeline_mode=pl.Buffered(1))
    const_specs = [of_layer(c) if c.ndim == 3 else _const_spec(c.shape) for c in consts]
    lead, lead_specs = [x2d], [row]
    if mix is not None:
        attn16, gated16, eprm = mix
        mix_consts = [eprm[k] for k in ("wo_a", "wo_b", "ln_g", "ln_b")]
        lead += [attn16, gated16] + mix_consts
        lead_specs += [pl.BlockSpec((tm, A_WIDTH), lambda i: (i, 0)), pl.BlockSpec((tm, B_WIDTH), lambda i: (i, 0))]
        lead_specs += [_const_spec(c.shape) for c in mix_consts]
    y, so = pl.pallas_call(
        functools.partial(_ffn_kernel, tm=tm, nseq=nseq, bps=bps, tn=FFN_CHUNK, mix=mix is not None),
        grid=(nblk,),
        in_specs=lead_specs + [pl.BlockSpec((nseq, HALO, 2 * D_FF), lambda i: (i // bps, 0, 0))] + const_specs,
        out_specs=(row, pl.BlockSpec((nseq, HALO, 2 * D_FF), lambda i: (i, 0, 0))),
        out_shape=(jax.ShapeDtypeStruct((n, D_MODEL), F32),
                   jax.ShapeDtypeStruct((nblk * nseq, HALO, 2 * D_FF), F32)),
        scratch_shapes=[pltpu.VMEM((tm, D_FF), BF16),
                        pltpu.VMEM((HALO, 2 * D_FF), F32)],
        compiler_params=_params(("arbitrary",)),
        name="conv_ffn",
    )(*lead, st, *consts)
    so = so.reshape(bsz, -1, HALO, 2 * D_FF)[:, -1]
    return y, so[:, HALO - (FFN_CONV - 1):, :]


def _odd_kernel(x_ref, wu_ref, wv_ref, sg_ref, sb_ref, ws_ref, bias_ref, wo_ref, g_ref, b_ref,
                y_ref, *rest, tm, lc, emit_v):
    gate_ref = rest[-1]
    x = x_ref[...]
    x16 = x.astype(BF16)
    u = jax.nn.gelu(_dot(x16, wu_ref[...]))
    v = _layer_norm(jax.nn.gelu(_dot(x16, wv_ref[...])), sg_ref[...], sb_ref[...])
    if emit_v:
        rest[0][...] = v
    v16 = v.astype(BF16)
    gw = C_WIDTH // C_GROUPS
    for c in range(tm // lc):
        rows = slice(c * lc, (c + 1) * lc)
        for g in range(C_GROUPS):
            cols = slice(g * gw, (g + 1) * gw)
            mixed = _dot(ws_ref[g], v16[rows, cols]) + bias_ref[:, cols]
            gate_ref[rows, cols] = (u[rows, cols] * mixed).astype(BF16)
    y = _dot(gate_ref[...], wo_ref[...])
    y_ref[...] = _layer_norm(ALPHA * x + y, g_ref[...], b_ref[...])


def _odd(x2d, seq_len, prm, emit_v):
    n = x2d.shape[0]
    tm = min(ROW_BLOCK, n)
    lc = min(seq_len, C_LEN)
    gw = C_WIDTH // C_GROUPS
    ws16 = (prm["sgu_w"][:, :lc, :lc] * jnp.tril(jnp.ones((lc, lc), F32))).astype(BF16)
    bias = jnp.repeat(prm["sgu_bias"][:, :lc].T, gw, axis=1)
    consts = [prm["w_u"], prm["w_v"], prm["sgu_g"], prm["sgu_b"], ws16, bias,
              prm["w_o"], prm["ln_g"], prm["ln_b"]]
    row = pl.BlockSpec((tm, D_MODEL), lambda i: (i, 0))
    n_out = 2 if emit_v else 1
    return pl.pallas_call(
        functools.partial(_odd_kernel, tm=tm, lc=lc, emit_v=emit_v),
        grid=(n // tm,),
        in_specs=[row] + [_const_spec(c.shape) for c in consts],
        out_specs=(row, pl.BlockSpec((tm, C_WIDTH), lambda i: (i, 0)))[:n_out],
        out_shape=(jax.ShapeDtypeStruct((n, D_MODEL), F32),
                   jax.ShapeDtypeStruct((n, C_WIDTH), F32))[:n_out],
        scratch_shapes=[pltpu.VMEM((tm, C_WIDTH), BF16)],
        compiler_params=_params(("arbitrary",)),
        name="odd_mixer",
    )(x2d, *consts)


def _row(v):
    return v.astype(F32)[None, :]


def _block_diag(w):
    nb, bd, _ = w.shape
    eye = jnp.eye(nb, dtype=w.dtype)
    return jnp.einsum("ncd,nm->ncmd", w, eye).reshape(nb * bd, nb * bd)


def _even_params(w_in, b_f, rg_conv_w, rg_conv_b, rg_wa, rg_ba, rg_wx, rg_bx, rg_lam, w_out, ln_g, ln_b):
    nf = 3 * A_WIDTH
    wf = jnp.pad(w_in[:, nf:nf + N_HEADS], ((0, 0), (0, LANES - N_HEADS)))
    return {
        "wqkv": w_in[:, :nf].astype(BF16),
        "wf": wf.astype(BF16),
        "wxg": w_in[:, nf + N_HEADS:].astype(BF16),
        "bf": jnp.pad(b_f, (0, LANES - N_HEADS))[None, :],
        "rg_cw": rg_conv_w, "rg_cb": _row(rg_conv_b),
        "rg_wa": _block_diag(rg_wa).astype(BF16), "rg_ba": _row(rg_ba),
        "rg_wx": _block_diag(rg_wx).astype(BF16), "rg_bx": _row(rg_bx),
        "rg_lam": _row(rg_lam),
        "wo_a": w_out[:A_WIDTH].astype(BF16), "wo_b": w_out[A_WIDTH:].astype(BF16),
        "ln_g": _row(ln_g), "ln_b": _row(ln_b),
    }


def _ffn_params(layer, w_up16_all, conv_w, conv_b, w_down16_all, ln_g, ln_b):
    taps = jnp.concatenate([conv_w, conv_b[None, :],
                            jnp.zeros((HALO - FFN_CONV - 1, 2 * D_FF), F32)], axis=0)
    return {"layer": layer, "w_up": w_up16_all, "taps": taps, "w_d": w_down16_all,
            "ln_g": _row(ln_g), "ln_b": _row(ln_b)}


def _odd_params(w_in, sgu_g, sgu_b, sgu_w, sgu_bias, w_out, ln_g, ln_b):
    return {
        "w_u": w_in[:, :C_WIDTH].astype(BF16), "w_v": w_in[:, C_WIDTH:].astype(BF16),
        "sgu_g": _row(sgu_g), "sgu_b": _row(sgu_b), "sgu_w": sgu_w, "sgu_bias": sgu_bias,
        "w_o": w_out.astype(BF16), "ln_g": _row(ln_g), "ln_b": _row(ln_b),
    }


def _even_layer(x3d, past, prm):
    bsz, t, _ = x3d.shape
    x2d = x3d.reshape(bsz * t, D_MODEL)
    assert t >= HALO
    if past is None:
        conv_s = jnp.zeros((bsz, RG_CONV - 1, B_WIDTH), F32)
        h0 = jnp.zeros((bsz, B_WIDTH), F32)
    else:
        k_c, v_c, lf_c, conv_s, h0 = past
    kv_t = t >= ROW_BLOCK
    qkv16, k, v, logf, cum, gated16, xtail, hl = _even_in(x2d, t, conv_s, h0, prm, kv_t)
    qkv16 = qkv16.reshape(bsz, t, 3 * A_WIDTH)
    cum3 = cum.reshape(bsz, t, LANES)
    cum_t = cum3[:, :, :N_HEADS].transpose(0, 2, 1)
    if past is None:
        attn = _fox_prompt(qkv16, cum3, cum_t, ROW_BLOCK)
    else:
        attn = _fox_sample(qkv16, cum3, cum_t, k_c.transpose(0, 2, 3, 1), v_c.transpose(0, 2, 3, 1),
                           lf_c.transpose(0, 2, 1))
    if kv_t:
        new_k = k.reshape(bsz, N_HEADS, HEAD_DIM, t).transpose(0, 3, 1, 2)
        new_v = v.reshape(bsz, N_HEADS, HEAD_DIM, t).transpose(0, 3, 1, 2)
    else:
        new_k = k.reshape(bsz, t, N_HEADS, HEAD_DIM)
        new_v = v.reshape(bsz, t, N_HEADS, HEAD_DIM)
    new_logf = logf.reshape(bsz, t, LANES)[:, :, :N_HEADS]
    h_last = hl.reshape(bsz, -1, B_WIDTH)[:, -1, :]
    conv_new = xtail.reshape(bsz, -1, HALO, B_WIDTH)[:, -1, HALO - (RG_CONV - 1):, :]
    return (attn.reshape(bsz * t, A_WIDTH), gated16), (new_k, new_v, new_logf, conv_new, h_last)


def _ffn_layer(x3d, state, prm, mix=None):
    bsz, t, _ = x3d.shape
    y, new_state = _ffn(x3d.reshape(bsz * t, D_MODEL), state, t, prm, mix)
    return y.reshape(bsz, t, D_MODEL), new_state


def _odd_layer(x3d, prm, emit_v):
    bsz, t, _ = x3d.shape
    outs = _odd(x3d.reshape(bsz * t, D_MODEL), t, prm, emit_v)
    v = outs[1].reshape(bsz, t, C_WIDTH) if emit_v else None
    return outs[0].reshape(bsz, t, D_MODEL), v


def kernel(x_prompt, x_sample, cache_k, cache_v, cache_logf, state_rglru_conv, state_rglru_h, state_ffn_conv, w_in_e, b_f, rg_conv_w, rg_conv_b, rg_wa, rg_ba, rg_wx, rg_bx, rg_lam, w_out_e, w_in_o, sgu_g, sgu_b, sgu_w, sgu_bias, w_out_o, ln_mix_g, ln_mix_b, ln_ffn_g, ln_ffn_b, ffn_w_up, ffn_conv_w, ffn_conv_b, ffn_w_down):
    xp, xs = x_prompt, x_sample
    w_up16_all = ffn_w_up.astype(BF16)
    w_down16_all = ffn_w_down.astype(BF16)
    even_p, even_s, ffn_p, ffn_s, sgu_v = [], [], [], [], []
    for layer in range(DEPTH):
        if layer % 2 == 0:
            e = layer // 2
            prm = _even_params(w_in_e[e], b_f[e], rg_conv_w[e], rg_conv_b[e], rg_wa[e], rg_ba[e],
                               rg_wx[e], rg_bx[e], rg_lam[e], w_out_e[e],
                               ln_mix_g[layer], ln_mix_b[layer])
            branches_p, st_p = _even_layer(xp, None, prm)
            past = (cache_k[e], cache_v[e], cache_logf[e], state_rglru_conv[e], state_rglru_h[e])
            branches_s, st_s = _even_layer(xs, past, prm)
            mix_p, mix_s = branches_p + (prm,), branches_s + (prm,)
            even_p.append(st_p)
            even_s.append(st_s)
        else:
            o = layer // 2
            prm = _odd_params(w_in_o[o], sgu_g[o], sgu_b[o], sgu_w[o], sgu_bias[o], w_out_o[o],
                              ln_mix_g[layer], ln_mix_b[layer])
            xp, _ = _odd_layer(xp, prm, emit_v=False)
            xs, sv = _odd_layer(xs, prm, emit_v=True)
            sgu_v.append(sv)
            mix_p = mix_s = None
        fprm = _ffn_params(layer, w_up16_all, ffn_conv_w[layer], ffn_conv_b[layer], w_down16_all,
                           ln_ffn_g[layer], ln_ffn_b[layer])
        zero_fc = jnp.zeros((xp.shape[0], FFN_CONV - 1, 2 * D_FF), F32)
        xp, fcp = _ffn_layer(xp, zero_fc, fprm, mix_p)
        xs, fcs = _ffn_layer(xs, state_ffn_conv[layer], fprm, mix_s)
        ffn_p.append(fcp)
        ffn_s.append(fcs)
    stack = lambda lst, idx: jnp.stack([item[idx] for item in lst])
    return (xp, xs,
            stack(even_p, 0), stack(even_p, 1), stack(even_p, 2),
            stack(even_s, 0), stack(even_s, 1), stack(even_s, 2),
            stack(even_p, 3), stack(even_p, 4), stack(even_s, 3), stack(even_s, 4),
            jnp.stack(ffn_p), jnp.stack(ffn_s), jnp.stack(sgu_v))
```

```python
import functools

import jax
import jax.numpy as jnp
from jax import lax
from jax.experimental import pallas as pl
from jax.experimental.pallas import tpu as pltpu

F32 = jnp.float32
BF16 = jnp.bfloat16

D_MODEL = 1024
N_HEADS = 8
HEAD_DIM = 64
A_WIDTH = N_HEADS * HEAD_DIM
B_WIDTH = 512
RG_CONV = 4
RG_C = 8.0
C_WIDTH = 1024
C_GROUPS = 8
C_LEN = 128
D_FF = 2816
FFN_CONV = 3
DEPTH = 2
ALPHA = (2 * DEPTH) ** 0.25
LN_EPS = 1e-5

LANES = 128
SUBLANES = 8
HALO = SUBLANES
NEG = -1e30
LOG2E = 1.4426950408889634
ROW_BLOCK = 512
FFN_CHUNK = 256
VMEM_LIMIT = 48 * 1024 * 1024


def _layer_norm(x, g, b):
    mu = jnp.mean(x, axis=-1, keepdims=True)
    xc = x - mu
    var = jnp.mean(xc * xc, axis=-1, keepdims=True)
    return xc * lax.rsqrt(var + LN_EPS) * g + b


def _log_sigmoid(x):
    return jnp.minimum(x, 0.0) - jnp.log1p(jnp.exp(-jnp.abs(x)))


def _softplus(x):
    return jnp.maximum(x, 0.0) + jnp.log1p(jnp.exp(-jnp.abs(x)))


def _dot(a, b):
    return jnp.dot(a, b, preferred_element_type=F32)


def _dot_nt(a, b):
    return lax.dot_general(a, b, (((1,), (1,)), ((), ())), preferred_element_type=F32)


def _const_spec(shape):
    nd = len(shape)
    return pl.BlockSpec(shape, lambda *_: (0,) * nd, pipeline_mode=pl.Buffered(1))


def _params(sem):
    return pltpu.CompilerParams(dimension_semantics=sem, vmem_limit_bytes=VMEM_LIMIT)


def _plan(n_rows, seq_len):
    tm = min(ROW_BLOCK, n_rows)
    nseq = max(1, tm // seq_len)
    bps = max(1, seq_len // tm)
    assert n_rows % tm == 0 and (seq_len % tm == 0 or tm % seq_len == 0)
    return tm, nseq, bps


def _even_in_kernel(x_ref, wqkv_ref, wf_ref, wxg_ref, bf_ref, cs_ref, h0_ref,
                    cw_ref, cb_ref, wa_ref, ba_ref, wx_ref, bx_ref, lam_ref,
                    qkv16_ref, k_ref, v_ref, logf_ref, cum_ref, gated_ref, xtail_ref, hl_ref,
                    carry_ref, ccarry_ref, hcarry_ref, *, tm, nseq, bps, kv_t):
    i = pl.program_id(0)
    lb = tm // nseq
    first = (i % bps) == 0
    x16 = x_ref[...].astype(BF16)
    xg = _dot(x16, wxg_ref[...])
    half = B_WIDTH // 2
    for cols in (slice(0, half), slice(half, B_WIDTH)):
        xb = xg[:, cols]
        if bps > 1:
            halos = [jnp.where(first, cs_ref[0, :, cols], ccarry_ref[:, cols])]
            inits = [jnp.where(first, h0_ref[0, :, cols], hcarry_ref[:, cols])]
            ccarry_ref[:, cols] = xb[tm - HALO:, :]
        else:
            halos = [cs_ref[s, :, cols] for s in range(nseq)]
            inits = [h0_ref[s, :, cols] for s in range(nseq)]
        pieces = []
        for s in range(nseq):
            xs = xb[s * lb:(s + 1) * lb, :]
            xtail_ref[s, :, cols] = xs[lb - HALO:, :]
            xc = cb_ref[:, cols] + cw_ref[RG_CONV - 1:RG_CONV, cols] * xs
            for d in range(1, RG_CONV):
                xc = xc + cw_ref[RG_CONV - 1 - d:RG_CONV - d, cols] * _shift_rows(xs, halos[s], d)
            pieces.append(xc)
        xc = pieces[0] if nseq == 1 else jnp.concatenate(pieces, axis=0)
        xc16 = xc.astype(BF16)
        gate_r = jax.nn.sigmoid(_dot(xc16, wa_ref[cols, cols]) + ba_ref[:, cols])
        gate_i = jax.nn.sigmoid(_dot(xc16, wx_ref[cols, cols]) + bx_ref[:, cols])
        log_a = -RG_C * gate_r * _softplus(-lam_ref[:, cols])
        a = jnp.exp(log_a)
        u = jnp.sqrt(jnp.tanh(-log_a) * (a * a + 1.0)) * (gate_i * xc)
        h = _linear_scan(a, u, inits, lb)
        if bps > 1:
            hcarry_ref[:, cols] = h[tm - 1:tm, :]
        for s in range(nseq):
            hl_ref[0, s:s + 1, cols] = h[(s + 1) * lb - 1:(s + 1) * lb, :]
        gb = xg[:, B_WIDTH + cols.start:B_WIDTH + cols.stop]
        gated_ref[:, cols] = (jax.nn.gelu(gb) * h).astype(BF16)

    qkv = _dot(x16, wqkv_ref[...])
    qkv16_ref[:, :A_WIDTH] = (qkv[:, :A_WIDTH] * (HEAD_DIM ** -0.5 * LOG2E)).astype(BF16)
    qkv16_ref[:, A_WIDTH:] = qkv[:, A_WIDTH:].astype(BF16)
    if kv_t:
        k_ref[0] = qkv[:, A_WIDTH:2 * A_WIDTH].T
        v_ref[0] = qkv[:, 2 * A_WIDTH:].T
    else:
        k_ref[...] = qkv[:, A_WIDTH:2 * A_WIDTH]
        v_ref[...] = qkv[:, 2 * A_WIDTH:]
    logf = _log_sigmoid(_dot(x16, wf_ref[...]) + bf_ref[...])
    logf_ref[...] = logf
    zero = jnp.zeros((1, LANES), F32)
    if bps > 1:
        cs = _cumsum_rows(logf, [jnp.where(first, zero, carry_ref[...])], lb)
        carry_ref[...] = cs[tm - 1:tm, :]
    else:
        cs = _cumsum_rows(logf, [zero] * nseq, lb)
    cum_ref[...] = cs


def _even_in(x2d, seq_len, conv_state, h0, prm, kv_t):
    n = x2d.shape[0]
    tm, nseq, bps = _plan(n, seq_len)
    nblk = n // tm
    cs = jnp.pad(conv_state, ((0, 0), (HALO - (RG_CONV - 1), 0), (0, 0)))
    h03 = h0.astype(F32)[:, None, :]
    state = lambda r: pl.BlockSpec((nseq, r, B_WIDTH), lambda i: (i // bps, 0, 0))
    consts = [prm[k] for k in ("wqkv", "wf", "wxg", "bf")]
    rg = [prm[k] for k in ("rg_cw", "rg_cb", "rg_wa", "rg_ba", "rg_wx", "rg_bx", "rg_lam")]
    if kv_t:
        assert nseq == 1
        kv_shape = jax.ShapeDtypeStruct((n // seq_len, A_WIDTH, seq_len), F32)
        kv_spec = pl.BlockSpec((1, A_WIDTH, tm), lambda i: (i // bps, 0, i % bps))
    else:
        kv_shape = jax.ShapeDtypeStruct((n, A_WIDTH), F32)
        kv_spec = pl.BlockSpec((tm, A_WIDTH), lambda i: (i, 0))
    row = lambda w: pl.BlockSpec((tm, w), lambda i: (i, 0))
    out_shapes = (
        jax.ShapeDtypeStruct((n, 3 * A_WIDTH), BF16),
        kv_shape,
        kv_shape,
        jax.ShapeDtypeStruct((n, LANES), F32),
        jax.ShapeDtypeStruct((n, LANES), F32),
        jax.ShapeDtypeStruct((n, B_WIDTH), BF16),
        jax.ShapeDtypeStruct((nblk * nseq, HALO, B_WIDTH), F32),
        jax.ShapeDtypeStruct((nblk, nseq, B_WIDTH), F32),
    )
    return pl.pallas_call(
        functools.partial(_even_in_kernel, tm=tm, nseq=nseq, bps=bps, kv_t=kv_t),
        grid=(nblk,),
        in_specs=[row(D_MODEL)] + [_const_spec(c.shape) for c in consts]
                 + [state(HALO), state(1)] + [_const_spec(c.shape) for c in rg],
        out_specs=(row(3 * A_WIDTH), kv_spec, kv_spec, row(LANES), row(LANES), row(B_WIDTH),
                   pl.BlockSpec((nseq, HALO, B_WIDTH), lambda i: (i, 0, 0)),
                   pl.BlockSpec((1, nseq, B_WIDTH), lambda i: (i, 0, 0))),
        out_shape=out_shapes,
        scratch_shapes=[pltpu.VMEM((1, LANES), F32),
                        pltpu.VMEM((HALO, B_WIDTH), F32),
                        pltpu.VMEM((1, B_WIDTH), F32)],
        compiler_params=_params(("arbitrary",)),
        name="even_in_proj",
    )(x2d, *consts, cs, h03, *rg)


def _fox_prompt_kernel(q_ref, k_ref, v_ref, cq_ref, ck_ref, *rest, blk, ncast):
    cast_in, o_ref, cast_out = rest[:ncast], rest[ncast], rest[ncast + 1:2 * ncast + 1]
    qm_ref, cqr_ref, m_ref, acc_ref = rest[2 * ncast + 1:]
    for src, dst in zip(cast_in, cast_out):
        dst[...] = src[...].astype(BF16)

    i = pl.program_id(1)
    lane = lax.broadcasted_iota(jnp.int32, (blk, LANES), 1)
    ngrp = blk // LANES

    for h in range(N_HEADS):
        hp, hh = divmod(h, 2)
        q = q_ref[0, :, hp * LANES:(hp + 1) * LANES]
        keep = (lane < HEAD_DIM) if hh == 0 else (lane >= HEAD_DIM)
        qm_ref[h] = jnp.where(keep, q, jnp.zeros_like(q))
        cqr_ref[h] = jnp.broadcast_to(cq_ref[0, :, h:h + 1] * LOG2E, (blk, LANES))
    m_ref[...] = jnp.full(m_ref.shape, NEG, F32)
    acc_ref[...] = jnp.zeros_like(acc_ref)

    def update(kg0, ngrp, diagonal):
        koff = pl.multiple_of(kg0 * LANES, LANES)
        vlane = lax.broadcasted_iota(jnp.int32, (ngrp * LANES, LANES), 1)
        if diagonal:
            rows = lax.broadcasted_iota(jnp.int32, (blk, LANES), 0)
            own = ngrp - blk // LANES
            visible = [None] * own + [lane + g * LANES <= rows for g in range(blk // LANES)]
        for h in range(N_HEADS):
            hp, hh = divmod(h, 2)
            k16 = k_ref[0, pl.ds(koff, ngrp * LANES), hp * LANES:(hp + 1) * LANES]
            v16 = v_ref[0, pl.ds(koff, ngrp * LANES), hp * LANES:(hp + 1) * LANES]
            vkeep = (vlane < HEAD_DIM) if hh == 0 else (vlane >= HEAD_DIM)
            v16 = jnp.where(vkeep, v16, jnp.ones_like(v16))
            s = _dot_nt(qm_ref[h], k16)
            grp = [s[:, g * LANES:(g + 1) * LANES] - ck_ref[0, kg0 + g, h:h + 1, :] * LOG2E
                   for g in range(ngrp)]
            if diagonal:
                grp = [g if vis is None else jnp.where(vis, g, NEG) for vis, g in zip(visible, grp)]
            mc = grp[0]
            for g in grp[1:]:
                mc = jnp.maximum(mc, g)
            cqr = cqr_ref[h]
            m_old = m_ref[h]
            m_new = jnp.maximum(m_old, jnp.max(mc, axis=1, keepdims=True) + cqr)
            shift = m_new - cqr
            p16 = jnp.concatenate([jnp.exp2(g - shift).astype(BF16) for g in grp], axis=1)
            acc_ref[h] = jnp.exp2(m_old - m_new) * acc_ref[h] + _dot(p16, v16)
            m_ref[h] = m_new

    def body(j, carry):
        update(j * (2 * ngrp), 2 * ngrp, False)
        return carry

    lax.fori_loop(0, i // 2, body, 0)

    @pl.when(i % 2 == 1)
    def _():
        update((i - 1) * ngrp, 2 * ngrp, True)

    @pl.when(i % 2 == 0)
    def _():
        update(i * ngrp, ngrp, True)

    for hp in range(N_HEADS // 2):
        a0 = acc_ref[2 * hp]
        a1 = acc_ref[2 * hp + 1]
        o = jnp.where(lane < HEAD_DIM, a0 / a0[:, HEAD_DIM:HEAD_DIM + 1], a1 / a1[:, 0:1])
        o_ref[0, :, hp * LANES:(hp + 1) * LANES] = o.astype(o_ref.dtype)


def _fox_prompt(qkv16, cum, cum_t, blk, casts):
    bsz, t, _ = qkv16.shape
    nb = t // blk
    ng = t // LANES
    nsteps = bsz * nb
    ck = cum_t.reshape(bsz, N_HEADS, ng, LANES).transpose(0, 2, 1, 3)
    slab = lambda w: pl.BlockSpec((w.shape[0] // nsteps, w.shape[1]), lambda b, i: (b * nb + i, 0))
    for w in casts:
        assert w.shape[0] % (nsteps * 2 * SUBLANES) == 0
    outs = pl.pallas_call(
        functools.partial(_fox_prompt_kernel, blk=blk, ncast=len(casts)),
        grid=(bsz, nb),
        in_specs=[pl.BlockSpec((1, blk, A_WIDTH), lambda b, i: (b, i, 0)),
                  pl.BlockSpec((1, t, A_WIDTH), lambda b, i: (b, 0, 1)),
                  pl.BlockSpec((1, t, A_WIDTH), lambda b, i: (b, 0, 2)),
                  pl.BlockSpec((1, blk, LANES), lambda b, i: (b, i, 0)),
                  pl.BlockSpec((1, ng, N_HEADS, LANES), lambda b, i: (b, 0, 0, 0))]
                 + [slab(w) for w in casts],
        out_specs=[pl.BlockSpec((1, blk, A_WIDTH), lambda b, i: (b, i, 0))] + [slab(w) for w in casts],
        out_shape=[jax.ShapeDtypeStruct((bsz, t, A_WIDTH), BF16)]
                  + [jax.ShapeDtypeStruct(w.shape, BF16) for w in casts],
        scratch_shapes=[pltpu.VMEM((N_HEADS, blk, LANES), BF16),
                        pltpu.VMEM((N_HEADS, blk, LANES), F32),
                        pltpu.VMEM((N_HEADS, blk, LANES), F32),
                        pltpu.VMEM((N_HEADS, blk, LANES), F32)],
        compiler_params=_params(("arbitrary", "arbitrary")),
        name="fox_prompt",
    )(qkv16, qkv16, qkv16, cum, ck, *casts)
    return outs[0], outs[1:]


def _fox_sample_kernel(qkv_ref, cq_ref, ckn_ref, lf_ref, kt_ref, vt_ref, o_ref, *, t, plen):
    lane = lax.broadcasted_iota(jnp.int32, (t, LANES), 1)
    lf = lf_ref[0]
    pos = lax.broadcasted_iota(jnp.int32, lf.shape, 1)
    suffix = lf
    d = 1
    while d < plen:
        suffix = suffix + jnp.where(pos + d < plen, pltpu.roll(suffix, plen - d, axis=1), 0.0)
        d *= 2
    ckc = lf - suffix
    rows = lax.broadcasted_iota(jnp.int32, (t, t), 0)
    cols = lax.broadcasted_iota(jnp.int32, (t, t), 1)
    visible = cols <= rows

    def per_head(s, bias_ref_rows, mask):
        halves = []
        for hh in range(2):
            part = s[hh * t:(hh + 1) * t, :] - bias_ref_rows[hh] * LOG2E
            halves.append(jnp.where(visible, part, NEG) if mask else part)
        return jnp.concatenate(halves, axis=0)

    for hp in range(N_HEADS // 2):
        h0 = 2 * hp
        sl = slice(hp * LANES, (hp + 1) * LANES)
        q = qkv_ref[0, :, hp * LANES:(hp + 1) * LANES]
        zero = jnp.zeros_like(q)
        qs = jnp.concatenate([jnp.where(lane < HEAD_DIM, q, zero),
                              jnp.where(lane >= HEAD_DIM, q, zero)], axis=0)
        kn16 = qkv_ref[0, :, A_WIDTH + hp * LANES:A_WIDTH + (hp + 1) * LANES]
        vn16 = qkv_ref[0, :, 2 * A_WIDTH + hp * LANES:2 * A_WIDTH + (hp + 1) * LANES]
        cq = jnp.concatenate([cq_ref[0, :, h0:h0 + 1], cq_ref[0, :, h0 + 1:h0 + 2]], axis=0) * LOG2E
        kt16 = kt_ref[0, h0:h0 + 2].reshape(LANES, plen).astype(BF16)
        vt16 = vt_ref[0, h0:h0 + 2].reshape(LANES, plen).astype(BF16)
        s_new = per_head(_dot_nt(qs, kn16), [ckn_ref[0, h0:h0 + 1, :], ckn_ref[0, h0 + 1:h0 + 2, :]], True)
        s_old = per_head(_dot(qs, kt16), [ckc[h0:h0 + 1, :], ckc[h0 + 1:h0 + 2, :]], False)
        m = jnp.maximum(jnp.max(s_new, axis=1, keepdims=True), jnp.max(s_old, axis=1, keepdims=True)) + cq
        shift = m - cq
        p_new = jnp.exp2(s_new - shift)
        p_old = jnp.exp2(s_old - shift)
        l = jnp.sum(p_new, axis=1, keepdims=True) + jnp.sum(p_old, axis=1, keepdims=True)
        acc = _dot(p_new.astype(BF16), vn16) + _dot_nt(p_old.astype(BF16), vt16)
        o = acc / l
        o_ref[0, :, sl] = jnp.where(lane < HEAD_DIM, o[:t, :], o[t:, :]).astype(o_ref.dtype)


def _fox_sample(qkv16, cum, cum_t, kt, vt, lf_t):
    bsz, t, _ = qkv16.shape
    plen = kt.shape[-1]
    per_b = lambda shape: pl.BlockSpec((1,) + shape, lambda b: (b,) + (0,) * len(shape))
    return pl.pallas_call(
        functools.partial(_fox_sample_kernel, t=t, plen=plen),
        grid=(bsz,),
        in_specs=[per_b((t, 3 * A_WIDTH)), per_b((t, LANES)), per_b((N_HEADS, t)),
                  per_b((N_HEADS, plen)),
                  per_b((N_HEADS, HEAD_DIM, plen)), per_b((N_HEADS, HEAD_DIM, plen))],
        out_specs=per_b((t, A_WIDTH)),
        out_shape=jax.ShapeDtypeStruct((bsz, t, A_WIDTH), BF16),
        compiler_params=_params(("arbitrary",)),
        name="fox_sample",
    )(qkv16, cum, cum_t, lf_t, kt, vt)


def _shift_rows(h, halo, d):
    rolled = pltpu.roll(h, d, axis=0)
    rowid = lax.broadcasted_iota(jnp.int32, halo.shape, 0)
    head = jnp.where(rowid < d, pltpu.roll(halo, d, axis=0), rolled[:HALO, :])
    return jnp.concatenate([head, rolled[HALO:, :]], axis=0)


def _linear_scan(a, u, inits, lb):
    rows, width = a.shape
    ngroups = rows // SUBLANES
    a3 = a.reshape(ngroups, SUBLANES, width)
    u3 = u.reshape(ngroups, SUBLANES, width)
    sub = lax.broadcasted_iota(jnp.int32, a3.shape, 1)
    d = 1
    while d < SUBLANES:
        inside = sub >= d
        u3 = jnp.where(inside, a3 * pltpu.roll(u3, d, axis=1) + u3, u3)
        a3 = jnp.where(inside, a3 * pltpu.roll(a3, d, axis=1), a3)
        d *= 2
    out = []
    carry = None
    for g in range(ngroups):
        if g % (lb // SUBLANES) == 0:
            carry = inits[g // (lb // SUBLANES)]
        hg = u3[g] + a3[g] * carry
        out.append(hg)
        carry = hg[SUBLANES - 1:, :]
    return jnp.concatenate(out, axis=0)


def _cumsum_rows(x, inits, lb):
    rows, width = x.shape
    ngroups = rows // SUBLANES
    x3 = x.reshape(ngroups, SUBLANES, width)
    sub = lax.broadcasted_iota(jnp.int32, x3.shape, 1)
    d = 1
    while d < SUBLANES:
        x3 = x3 + jnp.where(sub >= d, pltpu.roll(x3, d, axis=1), 0.0)
        d *= 2
    out = []
    carry = None
    for g in range(ngroups):
        if g % (lb // SUBLANES) == 0:
            carry = inits[g // (lb // SUBLANES)]
        cg = x3[g] + carry
        out.append(cg)
        carry = cg[SUBLANES - 1:, :]
    return jnp.concatenate(out, axis=0)


def _ffn_kernel(*refs, tm, nseq, bps, tn, mix):
    if mix:
        x_ref, attn_ref, gated_ref, woa_ref, wob_ref, mg_ref, mb_ref = refs[:7]
        refs = refs[7:]
    else:
        x_ref = refs[0]
        refs = refs[1:]
    st_ref, wup_ref, taps_ref, wd_ref, g_ref, b_ref, y_ref, so_ref, a_ref, carry_ref = refs
    i = pl.program_id(0)
    lb = tm // nseq
    x = x_ref[...]
    if mix:
        mixed = _dot(attn_ref[...], woa_ref[...]) + _dot(gated_ref[...], wob_ref[...])
        x = _layer_norm(ALPHA * x + mixed, mg_ref[...], mb_ref[...])
    x16 = x.astype(BF16)
    first = (i % bps) == 0

    def conv_branch(h, cols):
        taps = taps_ref[:, cols]
        outs = []
        for s in range(nseq):
            hs = h[s * lb:(s + 1) * lb, :]
            if bps > 1:
                halo = jnp.where(first, st_ref[0, :, cols], carry_ref[:, cols])
            else:
                halo = st_ref[s, :, cols]
            so_ref[s, :, cols] = hs[lb - HALO:, :]
            y = taps[3:4, :] + taps[2:3, :] * hs
            y = y + taps[1:2, :] * _shift_rows(hs, halo, 1)
            y = y + taps[0:1, :] * _shift_rows(hs, halo, 2)
            outs.append(y)
        if bps > 1:
            carry_ref[:, cols] = h[tm - HALO:, :]
        return outs

    for c in range(D_FF // tn):
        cg = slice(c * tn, (c + 1) * tn)
        cu = slice(D_FF + c * tn, D_FF + (c + 1) * tn)
        yg = conv_branch(_dot(x16, wup_ref[:, cg]), cg)
        yu = conv_branch(_dot(x16, wup_ref[:, cu]), cu)
        for s in range(nseq):
            a_ref[s * lb:(s + 1) * lb, cg] = (jax.nn.gelu(yg[s]) * yu[s]).astype(BF16)
    y = _dot(a_ref[...], wd_ref[...])
    y_ref[...] = _layer_norm(ALPHA * x + y, g_ref[...], b_ref[...])


def _ffn(x2d, state, seq_len, prm, mix=None):
    n = x2d.shape[0]
    tm, nseq, bps = _plan(n, seq_len)
    nblk = n // tm
    bsz = state.shape[0]
    st = jnp.pad(state, ((0, 0), (HALO - (FFN_CONV - 1), 0), (0, 0)))
    row = pl.BlockSpec((tm, D_MODEL), lambda i: (i, 0))
    consts = [prm[k] for k in ("w_up", "taps", "w_d", "ln_g", "ln_b")]
    layer = prm["layer"]
    of_layer = lambda w: pl.BlockSpec((None,) + w.shape[1:], lambda i: (layer, 0, 0), pipeline_mode=pl.Buffered(1))
    const_specs = [of_layer(c) if c.ndim == 3 else _const_spec(c.shape) for c in consts]
    lead, lead_specs = [x2d], [row]
    if mix is not None:
        attn16, gated16, eprm = mix
        mix_consts = [eprm[k] for k in ("wo_a", "wo_b", "ln_g", "ln_b")]
        lead += [attn16, gated16] + mix_consts
        lead_specs += [pl.BlockSpec((tm, A_WIDTH), lambda i: (i, 0)), pl.BlockSpec((tm, B_WIDTH), lambda i: (i, 0))]
        lead_specs += [_const_spec(c.shape) for c in mix_consts]
    y, so = pl.pallas_call(
        functools.partial(_ffn_kernel, tm=tm, nseq=nseq, bps=bps, tn=FFN_CHUNK, mix=mix is not None),
        grid=(nblk,),
        in_specs=lead_specs + [pl.BlockSpec((nseq, HALO, 2 * D_FF), lambda i: (i // bps, 0, 0))] + const_specs,
        out_specs=(row, pl.BlockSpec((nseq, HALO, 2 * D_FF), lambda i: (i, 0, 0))),
        out_shape=(jax.ShapeDtypeStruct((n, D_MODEL), F32),
                   jax.ShapeDtypeStruct((nblk * nseq, HALO, 2 * D_FF), F32)),
        scratch_shapes=[pltpu.VMEM((tm, D_FF), BF16),
                        pltpu.VMEM((HALO, 2 * D_FF), F32)],
        compiler_params=_params(("arbitrary",)),
        name="conv_ffn",
    )(*lead, st, *consts)
    so = so.reshape(bsz, -1, HALO, 2 * D_FF)[:, -1]
    return y, so[:, HALO - (FFN_CONV - 1):, :]


def _odd_kernel(x_ref, wu_ref, wv_ref, sg_ref, sb_ref, ws_ref, bias_ref, wo_ref, g_ref, b_ref,
                y_ref, *rest, tm, lc, emit_v):
    gate_ref = rest[-1]
    x = x_ref[...]
    x16 = x.astype(BF16)
    u = jax.nn.gelu(_dot(x16, wu_ref[...]))
    v = _layer_norm(jax.nn.gelu(_dot(x16, wv_ref[...])), sg_ref[...], sb_ref[...])
    if emit_v:
        rest[0][...] = v
    v16 = v.astype(BF16)
    gw = C_WIDTH // C_GROUPS
    for c in range(tm // lc):
        rows = slice(c * lc, (c + 1) * lc)
        for g in range(C_GROUPS):
            cols = slice(g * gw, (g + 1) * gw)
            mixed = _dot(ws_ref[g], v16[rows, cols]) + bias_ref[:, cols]
            gate_ref[rows, cols] = (u[rows, cols] * mixed).astype(BF16)
    y = _dot(gate_ref[...], wo_ref[...])
    y_ref[...] = _layer_norm(ALPHA * x + y, g_ref[...], b_ref[...])


def _odd(x2d, seq_len, prm, emit_v):
    n = x2d.shape[0]
    tm = min(ROW_BLOCK, n)
    lc = min(seq_len, C_LEN)
    gw = C_WIDTH // C_GROUPS
    ws16 = (prm["sgu_w"][:, :lc, :lc] * jnp.tril(jnp.ones((lc, lc), F32))).astype(BF16)
    bias = jnp.repeat(prm["sgu_bias"][:, :lc].T, gw, axis=1)
    consts = [prm["sgu_g"], prm["sgu_b"], ws16, bias, prm["w_o"], prm["ln_g"], prm["ln_b"]]
    w_half = lambda j: pl.BlockSpec((D_MODEL, C_WIDTH), lambda i: (0, j), pipeline_mode=pl.Buffered(1))
    row = pl.BlockSpec((tm, D_MODEL), lambda i: (i, 0))
    n_out = 2 if emit_v else 1
    return pl.pallas_call(
        functools.partial(_odd_kernel, tm=tm, lc=lc, emit_v=emit_v),
        grid=(n // tm,),
        in_specs=[row, w_half(0), w_half(1)] + [_const_spec(c.shape) for c in consts],
        out_specs=(row, pl.BlockSpec((tm, C_WIDTH), lambda i: (i, 0)))[:n_out],
        out_shape=(jax.ShapeDtypeStruct((n, D_MODEL), F32),
                   jax.ShapeDtypeStruct((n, C_WIDTH), F32))[:n_out],
        scratch_shapes=[pltpu.VMEM((tm, C_WIDTH), BF16)],
        compiler_params=_params(("arbitrary",)),
        name="odd_mixer",
    )(x2d, prm["w_in"], prm["w_in"], *consts)


def _row(v):
    return v.astype(F32)[None, :]


def _block_diag(w):
    nb, bd, _ = w.shape
    eye = jnp.eye(nb, dtype=w.dtype)
    return jnp.einsum("ncd,nm->ncmd", w, eye).reshape(nb * bd, nb * bd)


def _even_params(w_in, b_f, rg_conv_w, rg_conv_b, rg_wa, rg_ba, rg_wx, rg_bx, rg_lam, w_out, ln_g, ln_b):
    nf = 3 * A_WIDTH
    wf = jnp.pad(w_in[:, nf:nf + N_HEADS], ((0, 0), (0, LANES - N_HEADS)))
    return {
        "wqkv": w_in[:, :nf].astype(BF16),
        "wf": wf.astype(BF16),
        "wxg": w_in[:, nf + N_HEADS:].astype(BF16),
        "bf": jnp.pad(b_f, (0, LANES - N_HEADS))[None, :],
        "rg_cw": rg_conv_w, "rg_cb": _row(rg_conv_b),
        "rg_wa": _block_diag(rg_wa).astype(BF16), "rg_ba": _row(rg_ba),
        "rg_wx": _block_diag(rg_wx).astype(BF16), "rg_bx": _row(rg_bx),
        "rg_lam": _row(rg_lam),
        "wo_a": w_out[:A_WIDTH].astype(BF16), "wo_b": w_out[A_WIDTH:].astype(BF16),
        "ln_g": _row(ln_g), "ln_b": _row(ln_b),
    }


def _ffn_params(layer, w_up16_all, conv_w, conv_b, w_down16_all, ln_g, ln_b):
    taps = jnp.concatenate([conv_w, conv_b[None, :],
                            jnp.zeros((HALO - FFN_CONV - 1, 2 * D_FF), F32)], axis=0)
    return {"layer": layer, "w_up": w_up16_all, "taps": taps, "w_d": w_down16_all,
            "ln_g": _row(ln_g), "ln_b": _row(ln_b)}


def _odd_params(w_in16, sgu_g, sgu_b, sgu_w, sgu_bias, w_out16, ln_g, ln_b):
    return {
        "w_in": w_in16,
        "sgu_g": _row(sgu_g), "sgu_b": _row(sgu_b), "sgu_w": sgu_w, "sgu_bias": sgu_bias,
        "w_o": w_out16, "ln_g": _row(ln_g), "ln_b": _row(ln_b),
    }


def _even_layer(x3d, past, prm, casts=()):
    bsz, t, _ = x3d.shape
    x2d = x3d.reshape(bsz * t, D_MODEL)
    assert t >= HALO
    if past is None:
        conv_s = jnp.zeros((bsz, RG_CONV - 1, B_WIDTH), F32)
        h0 = jnp.zeros((bsz, B_WIDTH), F32)
    else:
        k_c, v_c, lf_c, conv_s, h0 = past
    kv_t = t >= ROW_BLOCK
    qkv16, k, v, logf, cum, gated16, xtail, hl = _even_in(x2d, t, conv_s, h0, prm, kv_t)
    qkv16 = qkv16.reshape(bsz, t, 3 * A_WIDTH)
    cum3 = cum.reshape(bsz, t, LANES)
    cum_t = cum3[:, :, :N_HEADS].transpose(0, 2, 1)
    cast16 = ()
    if past is None:
        attn, cast16 = _fox_prompt(qkv16, cum3, cum_t, ROW_BLOCK, list(casts))
    else:
        attn = _fox_sample(qkv16, cum3, cum_t, k_c.transpose(0, 2, 3, 1), v_c.transpose(0, 2, 3, 1),
                           lf_c.transpose(0, 2, 1))
    if kv_t:
        new_k = k.reshape(bsz, N_HEADS, HEAD_DIM, t).transpose(0, 3, 1, 2)
        new_v = v.reshape(bsz, N_HEADS, HEAD_DIM, t).transpose(0, 3, 1, 2)
    else:
        new_k = k.reshape(bsz, t, N_HEADS, HEAD_DIM)
        new_v = v.reshape(bsz, t, N_HEADS, HEAD_DIM)
    new_logf = logf.reshape(bsz, t, LANES)[:, :, :N_HEADS]
    h_last = hl.reshape(bsz, -1, B_WIDTH)[:, -1, :]
    conv_new = xtail.reshape(bsz, -1, HALO, B_WIDTH)[:, -1, HALO - (RG_CONV - 1):, :]
    return (attn.reshape(bsz * t, A_WIDTH), gated16), (new_k, new_v, new_logf, conv_new, h_last), cast16


def _ffn_layer(x3d, state, prm, mix=None):
    bsz, t, _ = x3d.shape
    y, new_state = _ffn(x3d.reshape(bsz * t, D_MODEL), state, t, prm, mix)
    return y.reshape(bsz, t, D_MODEL), new_state


def _odd_layer(x3d, prm, emit_v):
    bsz, t, _ = x3d.shape
    outs = _odd(x3d.reshape(bsz * t, D_MODEL), t, prm, emit_v)
    v = outs[1].reshape(bsz, t, C_WIDTH) if emit_v else None
    return outs[0].reshape(bsz, t, D_MODEL), v


def kernel(x_prompt, x_sample, cache_k, cache_v, cache_logf, state_rglru_conv, state_rglru_h, state_ffn_conv, w_in_e, b_f, rg_conv_w, rg_conv_b, rg_wa, rg_ba, rg_wx, rg_bx, rg_lam, w_out_e, w_in_o, sgu_g, sgu_b, sgu_w, sgu_bias, w_out_o, ln_mix_g, ln_mix_b, ln_ffn_g, ln_ffn_b, ffn_w_up, ffn_conv_w, ffn_conv_b, ffn_w_down):
    xp, xs = x_prompt, x_sample
    later_weights = [ffn_w_up.reshape(-1, 2 * D_FF), ffn_w_down.reshape(-1, D_MODEL),
                     w_in_o.reshape(-1, 2 * C_WIDTH), w_out_o.reshape(-1, D_MODEL)]
    even_p, even_s, ffn_p, ffn_s, sgu_v = [], [], [], [], []
    for layer in range(DEPTH):
        if layer % 2 == 0:
            e = layer // 2
            prm = _even_params(w_in_e[e], b_f[e], rg_conv_w[e], rg_conv_b[e], rg_wa[e], rg_ba[e],
                               rg_wx[e], rg_bx[e], rg_lam[e], w_out_e[e],
                               ln_mix_g[layer], ln_mix_b[layer])
            branches_p, st_p, cast16 = _even_layer(xp, None, prm, later_weights if layer == 0 else ())
            if layer == 0:
                w_up16_all = cast16[0].reshape(ffn_w_up.shape)
                w_down16_all = cast16[1].reshape(ffn_w_down.shape)
                w_in_o16 = cast16[2].reshape(w_in_o.shape)
                w_out_o16 = cast16[3].reshape(w_out_o.shape)
            past = (cache_k[e], cache_v[e], cache_logf[e], state_rglru_conv[e], state_rglru_h[e])
            branches_s, st_s, _ = _even_layer(xs, past, prm)
            mix_p, mix_s = branches_p + (prm,), branches_s + (prm,)
            even_p.append(st_p)
            even_s.append(st_s)
        else:
            o = layer // 2
            prm = _odd_params(w_in_o16[o], sgu_g[o], sgu_b[o], sgu_w[o], sgu_bias[o], w_out_o16[o],
                              ln_mix_g[layer], ln_mix_b[layer])
            xp, _ = _odd_layer(xp, prm, emit_v=False)
            xs, sv = _odd_layer(xs, prm, emit_v=True)
            sgu_v.append(sv)
            mix_p = mix_s = None
        fprm = _ffn_params(layer, w_up16_all, ffn_conv_w[layer], ffn_conv_b[layer], w_down16_all,
                           ln_ffn_g[layer], ln_ffn_b[layer])
        zero_fc = jnp.zeros((xp.shape[0], FFN_CONV - 1, 2 * D_FF), F32)
        xp, fcp = _ffn_layer(xp, zero_fc, fprm, mix_p)
        xs, fcs = _ffn_layer(xs, state_ffn_conv[layer], fprm, mix_s)
        ffn_p.append(fcp)
        ffn_s.append(fcs)
    stack = lambda lst, idx: jnp.stack([item[idx] for item in lst])
    return (xp, xs,
            stack(even_p, 0), stack(even_p, 1), stack(even_p, 2),
            stack(even_s, 0), stack(even_s, 1), stack(even_s, 2),
            stack(even_p, 3), stack(even_p, 4), stack(even_s, 3), stack(even_s, 4),
            jnp.stack(ffn_p), jnp.stack(ffn_s), jnp.stack(sgu_v))
```

```python
import functools

import jax
import jax.numpy as jnp
from jax import lax
from jax.experimental import pallas as pl
from jax.experimental.pallas import tpu as pltpu

F32 = jnp.float32
BF16 = jnp.bfloat16

D_MODEL = 1024
N_HEADS = 8
HEAD_DIM = 64
A_WIDTH = N_HEADS * HEAD_DIM
B_WIDTH = 512
RG_CONV = 4
RG_C = 8.0
C_WIDTH = 1024
C_GROUPS = 8
C_LEN = 128
D_FF = 2816
FFN_CONV = 3
DEPTH = 2
ALPHA = (2 * DEPTH) ** 0.25
LN_EPS = 1e-5

LANES = 128
SUBLANES = 8
HALO = SUBLANES
NEG = -1e30
LOG2E = 1.4426950408889634
ROW_BLOCK = 512
FFN_CHUNK = 256
VMEM_LIMIT = 48 * 1024 * 1024


def _layer_norm(x, g, b):
    mu = jnp.mean(x, axis=-1, keepdims=True)
    xc = x - mu
    var = jnp.mean(xc * xc, axis=-1, keepdims=True)
    return xc * lax.rsqrt(var + LN_EPS) * g + b


def _log_sigmoid(x):
    return jnp.minimum(x, 0.0) - jnp.log1p(jnp.exp(-jnp.abs(x)))


def _softplus(x):
    return jnp.maximum(x, 0.0) + jnp.log1p(jnp.exp(-jnp.abs(x)))


def _dot(a, b):
    return jnp.dot(a, b, preferred_element_type=F32)


def _dot_nt(a, b):
    return lax.dot_general(a, b, (((1,), (1,)), ((), ())), preferred_element_type=F32)


def _const_spec(shape):
    nd = len(shape)
    return pl.BlockSpec(shape, lambda *_: (0,) * nd, pipeline_mode=pl.Buffered(1))


def _params(sem):
    return pltpu.CompilerParams(dimension_semantics=sem, vmem_limit_bytes=VMEM_LIMIT)


def _plan(n_rows, seq_len):
    tm = min(ROW_BLOCK, n_rows)
    nseq = max(1, tm // seq_len)
    bps = max(1, seq_len // tm)
    assert n_rows % tm == 0 and (seq_len % tm == 0 or tm % seq_len == 0)
    return tm, nseq, bps


def _even_in_kernel(x_ref, wqkv_ref, wf_ref, wxg_ref, bf_ref, cs_ref, h0_ref,
                    cw_ref, cb_ref, wa_ref, ba_ref, wx_ref, bx_ref, lam_ref,
                    qkv16_ref, k_ref, v_ref, logf_ref, cum_ref, ck_ref, gated_ref, xtail_ref, hl_ref,
                    carry_ref, ccarry_ref, hcarry_ref, *, tm, nseq, bps, kv_t):
    i = pl.program_id(0)
    lb = tm // nseq
    first = (i % bps) == 0
    x16 = x_ref[...].astype(BF16)
    xg = _dot(x16, wxg_ref[...])
    half = B_WIDTH // 2
    for cols in (slice(0, half), slice(half, B_WIDTH)):
        xb = xg[:, cols]
        if bps > 1:
            halos = [jnp.where(first, cs_ref[0, :, cols], ccarry_ref[:, cols])]
            inits = [jnp.where(first, h0_ref[0, :, cols], hcarry_ref[:, cols])]
            ccarry_ref[:, cols] = xb[tm - HALO:, :]
        else:
            halos = [cs_ref[s, :, cols] for s in range(nseq)]
            inits = [h0_ref[s, :, cols] for s in range(nseq)]
        pieces = []
        for s in range(nseq):
            xs = xb[s * lb:(s + 1) * lb, :]
            xtail_ref[s, :, cols] = xs[lb - HALO:, :]
            xc = cb_ref[:, cols] + cw_ref[RG_CONV - 1:RG_CONV, cols] * xs
            for d in range(1, RG_CONV):
                xc = xc + cw_ref[RG_CONV - 1 - d:RG_CONV - d, cols] * _shift_rows(xs, halos[s], d)
            pieces.append(xc)
        xc = pieces[0] if nseq == 1 else jnp.concatenate(pieces, axis=0)
        xc16 = xc.astype(BF16)
        gate_r = jax.nn.sigmoid(_dot(xc16, wa_ref[cols, cols]) + ba_ref[:, cols])
        gate_i = jax.nn.sigmoid(_dot(xc16, wx_ref[cols, cols]) + bx_ref[:, cols])
        log_a = -RG_C * gate_r * _softplus(-lam_ref[:, cols])
        a = jnp.exp(log_a)
        u = jnp.sqrt(jnp.tanh(-log_a) * (a * a + 1.0)) * (gate_i * xc)
        h = _linear_scan(a, u, inits, lb)
        if bps > 1:
            hcarry_ref[:, cols] = h[tm - 1:tm, :]
        for s in range(nseq):
            hl_ref[0, s:s + 1, cols] = h[(s + 1) * lb - 1:(s + 1) * lb, :]
        gb = xg[:, B_WIDTH + cols.start:B_WIDTH + cols.stop]
        gated_ref[:, cols] = (jax.nn.gelu(gb) * h).astype(BF16)

    qkv = _dot(x16, wqkv_ref[...])
    qkv16_ref[:, :A_WIDTH] = (qkv[:, :A_WIDTH] * (HEAD_DIM ** -0.5 * LOG2E)).astype(BF16)
    qkv16_ref[:, A_WIDTH:] = qkv[:, A_WIDTH:].astype(BF16)
    if kv_t:
        k_ref[0] = qkv[:, A_WIDTH:2 * A_WIDTH].T
        v_ref[0] = qkv[:, 2 * A_WIDTH:].T
    else:
        k_ref[...] = qkv[:, A_WIDTH:2 * A_WIDTH]
        v_ref[...] = qkv[:, 2 * A_WIDTH:]
    logf = _log_sigmoid(_dot(x16, wf_ref[...]) + bf_ref[...])
    logf_ref[...] = logf
    zero = jnp.zeros((1, LANES), F32)
    if bps > 1:
        cs = _cumsum_rows(logf, [jnp.where(first, zero, carry_ref[...])], lb)
        carry_ref[...] = cs[tm - 1:tm, :]
    else:
        cs = _cumsum_rows(logf, [zero] * nseq, lb)
    cum_ref[...] = cs
    cst = cs.T
    for g in range(tm // LANES):
        ck_ref[g] = cst[:N_HEADS, g * LANES:(g + 1) * LANES]


def _even_in(x2d, seq_len, conv_state, h0, prm, kv_t):
    n = x2d.shape[0]
    tm, nseq, bps = _plan(n, seq_len)
    nblk = n // tm
    cs = jnp.pad(conv_state, ((0, 0), (HALO - (RG_CONV - 1), 0), (0, 0)))
    h03 = h0.astype(F32)[:, None, :]
    state = lambda r: pl.BlockSpec((nseq, r, B_WIDTH), lambda i: (i // bps, 0, 0))
    consts = [prm[k] for k in ("wqkv", "wf", "wxg", "bf")]
    rg = [prm[k] for k in ("rg_cw", "rg_cb", "rg_wa", "rg_ba", "rg_wx", "rg_bx", "rg_lam")]
    if kv_t:
        assert nseq == 1
        kv_shape = jax.ShapeDtypeStruct((n // seq_len, A_WIDTH, seq_len), F32)
        kv_spec = pl.BlockSpec((1, A_WIDTH, tm), lambda i: (i // bps, 0, i % bps))
    else:
        kv_shape = jax.ShapeDtypeStruct((n, A_WIDTH), F32)
        kv_spec = pl.BlockSpec((tm, A_WIDTH), lambda i: (i, 0))
    row = lambda w: pl.BlockSpec((tm, w), lambda i: (i, 0))
    out_shapes = (
        jax.ShapeDtypeStruct((n, 3 * A_WIDTH), BF16),
        kv_shape,
        kv_shape,
        jax.ShapeDtypeStruct((n, LANES), F32),
        jax.ShapeDtypeStruct((n, LANES), F32),
        jax.ShapeDtypeStruct((n // LANES, N_HEADS, LANES), F32),
        jax.ShapeDtypeStruct((n, B_WIDTH), BF16),
        jax.ShapeDtypeStruct((nblk * nseq, HALO, B_WIDTH), F32),
        jax.ShapeDtypeStruct((nblk, nseq, B_WIDTH), F32),
    )
    return pl.pallas_call(
        functools.partial(_even_in_kernel, tm=tm, nseq=nseq, bps=bps, kv_t=kv_t),
        grid=(nblk,),
        in_specs=[row(D_MODEL)] + [_const_spec(c.shape) for c in consts]
                 + [state(HALO), state(1)] + [_const_spec(c.shape) for c in rg],
        out_specs=(row(3 * A_WIDTH), kv_spec, kv_spec, row(LANES), row(LANES),
                   pl.BlockSpec((tm // LANES, N_HEADS, LANES), lambda i: (i, 0, 0)), row(B_WIDTH),
                   pl.BlockSpec((nseq, HALO, B_WIDTH), lambda i: (i, 0, 0)),
                   pl.BlockSpec((1, nseq, B_WIDTH), lambda i: (i, 0, 0))),
        out_shape=out_shapes,
        scratch_shapes=[pltpu.VMEM((1, LANES), F32),
                        pltpu.VMEM((HALO, B_WIDTH), F32),
                        pltpu.VMEM((1, B_WIDTH), F32)],
        compiler_params=_params(("arbitrary",)),
        name="even_in_proj",
    )(x2d, *consts, cs, h03, *rg)


def _fox_prompt_kernel(q_ref, k_ref, v_ref, cq_ref, ck_ref, *rest, blk, ncast):
    cast_in, o_ref, cast_out = rest[:ncast], rest[ncast], rest[ncast + 1:2 * ncast + 1]
    qm_ref, cqr_ref, m_ref, acc_ref = rest[2 * ncast + 1:]
    for src, dst in zip(cast_in, cast_out):
        dst[...] = src[...].astype(BF16)

    i = pl.program_id(1)
    lane = lax.broadcasted_iota(jnp.int32, (blk, LANES), 1)
    ngrp = blk // LANES

    for h in range(N_HEADS):
        hp, hh = divmod(h, 2)
        q = q_ref[0, :, hp * LANES:(hp + 1) * LANES]
        keep = (lane < HEAD_DIM) if hh == 0 else (lane >= HEAD_DIM)
        qm_ref[h] = jnp.where(keep, q, jnp.zeros_like(q))
        cqr_ref[h] = jnp.broadcast_to(cq_ref[0, :, h:h + 1] * LOG2E, (blk, LANES))
    m_ref[...] = jnp.full(m_ref.shape, NEG, F32)
    acc_ref[...] = jnp.zeros_like(acc_ref)

    def update(kg0, ngrp, diagonal):
        koff = pl.multiple_of(kg0 * LANES, LANES)
        vlane = lax.broadcasted_iota(jnp.int32, (ngrp * LANES, LANES), 1)
        if diagonal:
            rows = lax.broadcasted_iota(jnp.int32, (blk, LANES), 0)
            own = ngrp - blk // LANES
            visible = [None] * own + [lane + g * LANES <= rows for g in range(blk // LANES)]
        for h in range(N_HEADS):
            hp, hh = divmod(h, 2)
            k16 = k_ref[0, pl.ds(koff, ngrp * LANES), hp * LANES:(hp + 1) * LANES]
            v16 = v_ref[0, pl.ds(koff, ngrp * LANES), hp * LANES:(hp + 1) * LANES]
            vkeep = (vlane < HEAD_DIM) if hh == 0 else (vlane >= HEAD_DIM)
            v16 = jnp.where(vkeep, v16, jnp.ones_like(v16))
            s = _dot_nt(qm_ref[h], k16)
            grp = [s[:, g * LANES:(g + 1) * LANES] - ck_ref[0, kg0 + g, h:h + 1, :] * LOG2E
                   for g in range(ngrp)]
            if diagonal:
                grp = [g if vis is None else jnp.where(vis, g, NEG) for vis, g in zip(visible, grp)]
            mc = grp[0]
            for g in grp[1:]:
                mc = jnp.maximum(mc, g)
            cqr = cqr_ref[h]
            m_old = m_ref[h]
            m_new = jnp.maximum(m_old, jnp.max(mc, axis=1, keepdims=True) + cqr)
            shift = m_new - cqr
            p16 = jnp.concatenate([jnp.exp2(g - shift).astype(BF16) for g in grp], axis=1)
            acc_ref[h] = jnp.exp2(m_old - m_new) * acc_ref[h] + _dot(p16, v16)
            m_ref[h] = m_new

    def body(j, carry):
        update(j * (2 * ngrp), 2 * ngrp, False)
        return carry

    lax.fori_loop(0, i // 2, body, 0)

    @pl.when(i % 2 == 1)
    def _():
        update((i - 1) * ngrp, 2 * ngrp, True)

    @pl.when(i % 2 == 0)
    def _():
        update(i * ngrp, ngrp, True)

    for hp in range(N_HEADS // 2):
        a0 = acc_ref[2 * hp]
        a1 = acc_ref[2 * hp + 1]
        o = jnp.where(lane < HEAD_DIM, a0 / a0[:, HEAD_DIM:HEAD_DIM + 1], a1 / a1[:, 0:1])
        o_ref[0, :, hp * LANES:(hp + 1) * LANES] = o.astype(o_ref.dtype)


def _fox_prompt(qkv16, cum, ck, blk, casts):
    bsz, t, _ = qkv16.shape
    nb = t // blk
    ng = t // LANES
    nsteps = bsz * nb
    ck = ck.reshape(bsz, ng, N_HEADS, LANES)
    slab = lambda w: pl.BlockSpec((w.shape[0] // nsteps, w.shape[1]), lambda b, i: (b * nb + i, 0))
    for w in casts:
        assert w.shape[0] % (nsteps * 2 * SUBLANES) == 0
    outs = pl.pallas_call(
        functools.partial(_fox_prompt_kernel, blk=blk, ncast=len(casts)),
        grid=(bsz, nb),
        in_specs=[pl.BlockSpec((1, blk, A_WIDTH), lambda b, i: (b, i, 0)),
                  pl.BlockSpec((1, t, A_WIDTH), lambda b, i: (b, 0, 1)),
                  pl.BlockSpec((1, t, A_WIDTH), lambda b, i: (b, 0, 2)),
                  pl.BlockSpec((1, blk, LANES), lambda b, i: (b, i, 0)),
                  pl.BlockSpec((1, ng, N_HEADS, LANES), lambda b, i: (b, 0, 0, 0))]
                 + [slab(w) for w in casts],
        out_specs=[pl.BlockSpec((1, blk, A_WIDTH), lambda b, i: (b, i, 0))] + [slab(w) for w in casts],
        out_shape=[jax.ShapeDtypeStruct((bsz, t, A_WIDTH), BF16)]
                  + [jax.ShapeDtypeStruct(w.shape, BF16) for w in casts],
        scratch_shapes=[pltpu.VMEM((N_HEADS, blk, LANES), BF16),
                        pltpu.VMEM((N_HEADS, blk, LANES), F32),
                        pltpu.VMEM((N_HEADS, blk, LANES), F32),
                        pltpu.VMEM((N_HEADS, blk, LANES), F32)],
        compiler_params=_params(("arbitrary", "arbitrary")),
        name="fox_prompt",
    )(qkv16, qkv16, qkv16, cum, ck, *casts)
    return outs[0], outs[1:]


def _fox_sample_kernel(qkv_ref, cq_ref, ckn_ref, lf_ref, kt_ref, vt_ref, o_ref, *, t, plen):
    lane = lax.broadcasted_iota(jnp.int32, (t, LANES), 1)
    lf = lf_ref[0]
    pos = lax.broadcasted_iota(jnp.int32, lf.shape, 1)
    suffix = lf
    d = 1
    while d < plen:
        suffix = suffix + jnp.where(pos + d < plen, pltpu.roll(suffix, plen - d, axis=1), 0.0)
        d *= 2
    ckc = lf - suffix
    rows = lax.broadcasted_iota(jnp.int32, (t, t), 0)
    cols = lax.broadcasted_iota(jnp.int32, (t, t), 1)
    visible = cols <= rows

    def per_head(s, bias_ref_rows, mask):
        halves = []
        for hh in range(2):
            part = s[hh * t:(hh + 1) * t, :] - bias_ref_rows[hh] * LOG2E
            halves.append(jnp.where(visible, part, NEG) if mask else part)
        return jnp.concatenate(halves, axis=0)

    for hp in range(N_HEADS // 2):
        h0 = 2 * hp
        sl = slice(hp * LANES, (hp + 1) * LANES)
        q = qkv_ref[0, :, hp * LANES:(hp + 1) * LANES]
        zero = jnp.zeros_like(q)
        qs = jnp.concatenate([jnp.where(lane < HEAD_DIM, q, zero),
                              jnp.where(lane >= HEAD_DIM, q, zero)], axis=0)
        kn16 = qkv_ref[0, :, A_WIDTH + hp * LANES:A_WIDTH + (hp + 1) * LANES]
        vn16 = qkv_ref[0, :, 2 * A_WIDTH + hp * LANES:2 * A_WIDTH + (hp + 1) * LANES]
        cq = jnp.concatenate([cq_ref[0, :, h0:h0 + 1], cq_ref[0, :, h0 + 1:h0 + 2]], axis=0) * LOG2E
        kt16 = kt_ref[0, h0:h0 + 2].reshape(LANES, plen).astype(BF16)
        vt16 = vt_ref[0, h0:h0 + 2].reshape(LANES, plen).astype(BF16)
        s_new = per_head(_dot_nt(qs, kn16), [ckn_ref[0, h0:h0 + 1, :], ckn_ref[0, h0 + 1:h0 + 2, :]], True)
        s_old = per_head(_dot(qs, kt16), [ckc[h0:h0 + 1, :], ckc[h0 + 1:h0 + 2, :]], False)
        m = jnp.maximum(jnp.max(s_new, axis=1, keepdims=True), jnp.max(s_old, axis=1, keepdims=True)) + cq
        shift = m - cq
        p_new = jnp.exp2(s_new - shift)
        p_old = jnp.exp2(s_old - shift)
        l = jnp.sum(p_new, axis=1, keepdims=True) + jnp.sum(p_old, axis=1, keepdims=True)
        acc = _dot(p_new.astype(BF16), vn16) + _dot_nt(p_old.astype(BF16), vt16)
        o = acc / l
        o_ref[0, :, sl] = jnp.where(lane < HEAD_DIM, o[:t, :], o[t:, :]).astype(o_ref.dtype)


def _fox_sample(qkv16, cum, cum_t, kt, vt, lf_t):
    bsz, t, _ = qkv16.shape
    plen = kt.shape[-1]
    per_b = lambda shape: pl.BlockSpec((1,) + shape, lambda b: (b,) + (0,) * len(shape))
    return pl.pallas_call(
        functools.partial(_fox_sample_kernel, t=t, plen=plen),
        grid=(bsz,),
        in_specs=[per_b((t, 3 * A_WIDTH)), per_b((t, LANES)), per_b((N_HEADS, t)),
                  per_b((N_HEADS, plen)),
                  per_b((N_HEADS, HEAD_DIM, plen)), per_b((N_HEADS, HEAD_DIM, plen))],
        out_specs=per_b((t, A_WIDTH)),
        out_shape=jax.ShapeDtypeStruct((bsz, t, A_WIDTH), BF16),
        compiler_params=_params(("arbitrary",)),
        name="fox_sample",
    )(qkv16, cum, cum_t, lf_t, kt, vt)


def _shift_rows(h, halo, d):
    rolled = pltpu.roll(h, d, axis=0)
    rowid = lax.broadcasted_iota(jnp.int32, halo.shape, 0)
    head = jnp.where(rowid < d, pltpu.roll(halo, d, axis=0), rolled[:HALO, :])
    return jnp.concatenate([head, rolled[HALO:, :]], axis=0)


def _linear_scan(a, u, inits, lb):
    rows, width = a.shape
    ngroups = rows // SUBLANES
    a3 = a.reshape(ngroups, SUBLANES, width)
    u3 = u.reshape(ngroups, SUBLANES, width)
    sub = lax.broadcasted_iota(jnp.int32, a3.shape, 1)
    d = 1
    while d < SUBLANES:
        inside = sub >= d
        u3 = jnp.where(inside, a3 * pltpu.roll(u3, d, axis=1) + u3, u3)
        a3 = jnp.where(inside, a3 * pltpu.roll(a3, d, axis=1), a3)
        d *= 2
    out = []
    carry = None
    for g in range(ngroups):
        if g % (lb // SUBLANES) == 0:
            carry = inits[g // (lb // SUBLANES)]
        hg = u3[g] + a3[g] * carry
        out.append(hg)
        carry = hg[SUBLANES - 1:, :]
    return jnp.concatenate(out, axis=0)


def _cumsum_rows(x, inits, lb):
    rows, width = x.shape
    ngroups = rows // SUBLANES
    x3 = x.reshape(ngroups, SUBLANES, width)
    sub = lax.broadcasted_iota(jnp.int32, x3.shape, 1)
    d = 1
    while d < SUBLANES:
        x3 = x3 + jnp.where(sub >= d, pltpu.roll(x3, d, axis=1), 0.0)
        d *= 2
    out = []
    carry = None
    for g in range(ngroups):
        if g % (lb // SUBLANES) == 0:
            carry = inits[g // (lb // SUBLANES)]
        cg = x3[g] + carry
        out.append(cg)
        carry = cg[SUBLANES - 1:, :]
    return jnp.concatenate(out, axis=0)


def _ffn_kernel(*refs, tm, nseq, bps, tn, mix):
    if mix:
        x_ref, attn_ref, gated_ref, woa_ref, wob_ref, mg_ref, mb_ref = refs[:7]
        refs = refs[7:]
    else:
        x_ref = refs[0]
        refs = refs[1:]
    st_ref, wup_ref, taps_ref, wd_ref, g_ref, b_ref, y_ref, so_ref, a_ref, carry_ref = refs
    i = pl.program_id(0)
    lb = tm // nseq
    x = x_ref[...]
    if mix:
        mixed = _dot(attn_ref[...], woa_ref[...]) + _dot(gated_ref[...], wob_ref[...])
        x = _layer_norm(ALPHA * x + mixed, mg_ref[...], mb_ref[...])
    x16 = x.astype(BF16)
    first = (i % bps) == 0

    def conv_branch(h, cols):
        taps = taps_ref[:, cols]
        outs = []
        for s in range(nseq):
            hs = h[s * lb:(s + 1) * lb, :]
            if bps > 1:
                halo = jnp.where(first, st_ref[0, :, cols], carry_ref[:, cols])
            else:
                halo = st_ref[s, :, cols]
            so_ref[s, :, cols] = hs[lb - HALO:, :]
            y = taps[3:4, :] + taps[2:3, :] * hs
            y = y + taps[1:2, :] * _shift_rows(hs, halo, 1)
            y = y + taps[0:1, :] * _shift_rows(hs, halo, 2)
            outs.append(y)
        if bps > 1:
            carry_ref[:, cols] = h[tm - HALO:, :]
        return outs

    for c in range(D_FF // tn):
        cg = slice(c * tn, (c + 1) * tn)
        cu = slice(D_FF + c * tn, D_FF + (c + 1) * tn)
        yg = conv_branch(_dot(x16, wup_ref[:, cg]), cg)
        yu = conv_branch(_dot(x16, wup_ref[:, cu]), cu)
        for s in range(nseq):
            a_ref[s * lb:(s + 1) * lb, cg] = (jax.nn.gelu(yg[s]) * yu[s]).astype(BF16)
    y = _dot(a_ref[...], wd_ref[...])
    y_ref[...] = _layer_norm(ALPHA * x + y, g_ref[...], b_ref[...])


def _ffn(x2d, state, seq_len, prm, mix=None):
    n = x2d.shape[0]
    tm, nseq, bps = _plan(n, seq_len)
    nblk = n // tm
    bsz = state.shape[0]
    st = jnp.pad(state, ((0, 0), (HALO - (FFN_CONV - 1), 0), (0, 0)))
    row = pl.BlockSpec((tm, D_MODEL), lambda i: (i, 0))
    consts = [prm[k] for k in ("w_up", "taps", "w_d", "ln_g", "ln_b")]
    layer = prm["layer"]
    of_layer = lambda w: pl.BlockSpec((None,) + w.shape[1:], lambda i: (layer, 0, 0), pipeline_mode=pl.Buffered(1))
    const_specs = [of_layer(c) if c.ndim == 3 else _const_spec(c.shape) for c in consts]
    lead, lead_specs = [x2d], [row]
    if mix is not None:
        attn16, gated16, eprm = mix
        mix_consts = [eprm[k] for k in ("wo_a", "wo_b", "ln_g", "ln_b")]
        lead += [attn16, gated16] + mix_consts
        lead_specs += [pl.BlockSpec((tm, A_WIDTH), lambda i: (i, 0)), pl.BlockSpec((tm, B_WIDTH), lambda i: (i, 0))]
        lead_specs += [_const_spec(c.shape) for c in mix_consts]
    y, so = pl.pallas_call(
        functools.partial(_ffn_kernel, tm=tm, nseq=nseq, bps=bps, tn=FFN_CHUNK, mix=mix is not None),
        grid=(nblk,),
        in_specs=lead_specs + [pl.BlockSpec((nseq, HALO, 2 * D_FF), lambda i: (i // bps, 0, 0))] + const_specs,
        out_specs=(row, pl.BlockSpec((nseq, HALO, 2 * D_FF), lambda i: (i, 0, 0))),
        out_shape=(jax.ShapeDtypeStruct((n, D_MODEL), F32),
                   jax.ShapeDtypeStruct((nblk * nseq, HALO, 2 * D_FF), F32)),
        scratch_shapes=[pltpu.VMEM((tm, D_FF), BF16),
                        pltpu.VMEM((HALO, 2 * D_FF), F32)],
        compiler_params=_params(("arbitrary",)),
        name="conv_ffn",
    )(*lead, st, *consts)
    so = so.reshape(bsz, -1, HALO, 2 * D_FF)[:, -1]
    return y, so[:, HALO - (FFN_CONV - 1):, :]


def _odd_kernel(x_ref, wu_ref, wv_ref, sg_ref, sb_ref, ws_ref, bias_ref, wo_ref, g_ref, b_ref,
                y_ref, *rest, tm, lc, emit_v):
    gate_ref = rest[-1]
    x = x_ref[...]
    x16 = x.astype(BF16)
    u = jax.nn.gelu(_dot(x16, wu_ref[...]))
    v = _layer_norm(jax.nn.gelu(_dot(x16, wv_ref[...])), sg_ref[...], sb_ref[...])
    if emit_v:
        rest[0][...] = v
    v16 = v.astype(BF16)
    gw = C_WIDTH // C_GROUPS
    for c in range(tm // lc):
        rows = slice(c * lc, (c + 1) * lc)
        for g in range(C_GROUPS):
            cols = slice(g * gw, (g + 1) * gw)
            mixed = _dot(ws_ref[g], v16[rows, cols]) + bias_ref[:, cols]
            gate_ref[rows, cols] = (u[rows, cols] * mixed).astype(BF16)
    y = _dot(gate_ref[...], wo_ref[...])
    y_ref[...] = _layer_norm(ALPHA * x + y, g_ref[...], b_ref[...])


def _odd(x2d, seq_len, prm, emit_v):
    n = x2d.shape[0]
    tm = min(ROW_BLOCK, n)
    lc = min(seq_len, C_LEN)
    gw = C_WIDTH // C_GROUPS
    ws16 = (prm["sgu_w"][:, :lc, :lc] * jnp.tril(jnp.ones((lc, lc), F32))).astype(BF16)
    bias = jnp.repeat(prm["sgu_bias"][:, :lc].T, gw, axis=1)
    consts = [prm["sgu_g"], prm["sgu_b"], ws16, bias, prm["w_o"], prm["ln_g"], prm["ln_b"]]
    w_half = lambda j: pl.BlockSpec((D_MODEL, C_WIDTH), lambda i: (0, j), pipeline_mode=pl.Buffered(1))
    row = pl.BlockSpec((tm, D_MODEL), lambda i: (i, 0))
    n_out = 2 if emit_v else 1
    return pl.pallas_call(
        functools.partial(_odd_kernel, tm=tm, lc=lc, emit_v=emit_v),
        grid=(n // tm,),
        in_specs=[row, w_half(0), w_half(1)] + [_const_spec(c.shape) for c in consts],
        out_specs=(row, pl.BlockSpec((tm, C_WIDTH), lambda i: (i, 0)))[:n_out],
        out_shape=(jax.ShapeDtypeStruct((n, D_MODEL), F32),
                   jax.ShapeDtypeStruct((n, C_WIDTH), F32))[:n_out],
        scratch_shapes=[pltpu.VMEM((tm, C_WIDTH), BF16)],
        compiler_params=_params(("arbitrary",)),
        name="odd_mixer",
    )(x2d, prm["w_in"], prm["w_in"], *consts)


def _row(v):
    return v.astype(F32)[None, :]


def _block_diag(w):
    nb, bd, _ = w.shape
    eye = jnp.eye(nb, dtype=w.dtype)
    return jnp.einsum("ncd,nm->ncmd", w, eye).reshape(nb * bd, nb * bd)


def _even_params(w_in, b_f, rg_conv_w, rg_conv_b, rg_wa, rg_ba, rg_wx, rg_bx, rg_lam, w_out, ln_g, ln_b):
    nf = 3 * A_WIDTH
    wf = jnp.pad(w_in[:, nf:nf + N_HEADS], ((0, 0), (0, LANES - N_HEADS)))
    return {
        "wqkv": w_in[:, :nf].astype(BF16),
        "wf": wf.astype(BF16),
        "wxg": w_in[:, nf + N_HEADS:].astype(BF16),
        "bf": jnp.pad(b_f, (0, LANES - N_HEADS))[None, :],
        "rg_cw": rg_conv_w, "rg_cb": _row(rg_conv_b),
        "rg_wa": _block_diag(rg_wa).astype(BF16), "rg_ba": _row(rg_ba),
        "rg_wx": _block_diag(rg_wx).astype(BF16), "rg_bx": _row(rg_bx),
        "rg_lam": _row(rg_lam),
        "wo_a": w_out[:A_WIDTH].astype(BF16), "wo_b": w_out[A_WIDTH:].astype(BF16),
        "ln_g": _row(ln_g), "ln_b": _row(ln_b),
    }


def _ffn_params(layer, w_up16_all, conv_w, conv_b, w_down16_all, ln_g, ln_b):
    taps = jnp.concatenate([conv_w, conv_b[None, :],
                            jnp.zeros((HALO - FFN_CONV - 1, 2 * D_FF), F32)], axis=0)
    return {"layer": layer, "w_up": w_up16_all, "taps": taps, "w_d": w_down16_all,
            "ln_g": _row(ln_g), "ln_b": _row(ln_b)}


def _odd_params(w_in16, sgu_g, sgu_b, sgu_w, sgu_bias, w_out16, ln_g, ln_b):
    return {
        "w_in": w_in16,
        "sgu_g": _row(sgu_g), "sgu_b": _row(sgu_b), "sgu_w": sgu_w, "sgu_bias": sgu_bias,
        "w_o": w_out16, "ln_g": _row(ln_g), "ln_b": _row(ln_b),
    }


def _even_layer(x3d, past, prm, casts=()):
    bsz, t, _ = x3d.shape
    x2d = x3d.reshape(bsz * t, D_MODEL)
    assert t >= HALO
    if past is None:
        conv_s = jnp.zeros((bsz, RG_CONV - 1, B_WIDTH), F32)
        h0 = jnp.zeros((bsz, B_WIDTH), F32)
    else:
        k_c, v_c, lf_c, conv_s, h0 = past
    kv_t = t >= ROW_BLOCK
    qkv16, k, v, logf, cum, ck, gated16, xtail, hl = _even_in(x2d, t, conv_s, h0, prm, kv_t)
    qkv16 = qkv16.reshape(bsz, t, 3 * A_WIDTH)
    cum3 = cum.reshape(bsz, t, LANES)
    cast16 = ()
    if past is None:
        attn, cast16 = _fox_prompt(qkv16, cum3, ck, ROW_BLOCK, list(casts))
    else:
        cum_t = cum3[:, :, :N_HEADS].transpose(0, 2, 1)
        attn = _fox_sample(qkv16, cum3, cum_t, k_c.transpose(0, 2, 3, 1), v_c.transpose(0, 2, 3, 1),
                           lf_c.transpose(0, 2, 1))
    if kv_t:
        new_k = k.reshape(bsz, N_HEADS, HEAD_DIM, t).transpose(0, 3, 1, 2)
        new_v = v.reshape(bsz, N_HEADS, HEAD_DIM, t).transpose(0, 3, 1, 2)
    else:
        new_k = k.reshape(bsz, t, N_HEADS, HEAD_DIM)
        new_v = v.reshape(bsz, t, N_HEADS, HEAD_DIM)
    new_logf = logf.reshape(bsz, t, LANES)[:, :, :N_HEADS]
    h_last = hl.reshape(bsz, -1, B_WIDTH)[:, -1, :]
    conv_new = xtail.reshape(bsz, -1, HALO, B_WIDTH)[:, -1, HALO - (RG_CONV - 1):, :]
    return (attn.reshape(bsz * t, A_WIDTH), gated16), (new_k, new_v, new_logf, conv_new, h_last), cast16


def _ffn_layer(x3d, state, prm, mix=None):
    bsz, t, _ = x3d.shape
    y, new_state = _ffn(x3d.reshape(bsz * t, D_MODEL), state, t, prm, mix)
    return y.reshape(bsz, t, D_MODEL), new_state


def _odd_layer(x3d, prm, emit_v):
    bsz, t, _ = x3d.shape
    outs = _odd(x3d.reshape(bsz * t, D_MODEL), t, prm, emit_v)
    v = outs[1].reshape(bsz, t, C_WIDTH) if emit_v else None
    return outs[0].reshape(bsz, t, D_MODEL), v


def kernel(x_prompt, x_sample, cache_k, cache_v, cache_logf, state_rglru_conv, state_rglru_h, state_ffn_conv, w_in_e, b_f, rg_conv_w, rg_conv_b, rg_wa, rg_ba, rg_wx, rg_bx, rg_lam, w_out_e, w_in_o, sgu_g, sgu_b, sgu_w, sgu_bias, w_out_o, ln_mix_g, ln_mix_b, ln_ffn_g, ln_ffn_b, ffn_w_up, ffn_conv_w, ffn_conv_b, ffn_w_down):
    xp, xs = x_prompt, x_sample
    later_weights = [ffn_w_up.reshape(-1, 2 * D_FF), ffn_w_down.reshape(-1, D_MODEL),
                     w_in_o.reshape(-1, 2 * C_WIDTH), w_out_o.reshape(-1, D_MODEL)]
    even_p, even_s, ffn_p, ffn_s, sgu_v = [], [], [], [], []
    for layer in range(DEPTH):
        if layer % 2 == 0:
            e = layer // 2
            prm = _even_params(w_in_e[e], b_f[e], rg_conv_w[e], rg_conv_b[e], rg_wa[e], rg_ba[e],
                               rg_wx[e], rg_bx[e], rg_lam[e], w_out_e[e],
                               ln_mix_g[layer], ln_mix_b[layer])
            branches_p, st_p, cast16 = _even_layer(xp, None, prm, later_weights if layer == 0 else ())
            if layer == 0:
                w_up16_all = cast16[0].reshape(ffn_w_up.shape)
                w_down16_all = cast16[1].reshape(ffn_w_down.shape)
                w_in_o16 = cast16[2].reshape(w_in_o.shape)
                w_out_o16 = cast16[3].reshape(w_out_o.shape)
            past = (cache_k[e], cache_v[e], cache_logf[e], state_rglru_conv[e], state_rglru_h[e])
            branches_s, st_s, _ = _even_layer(xs, past, prm)
            mix_p, mix_s = branches_p + (prm,), branches_s + (prm,)
            even_p.append(st_p)
            even_s.append(st_s)
        else:
            o = layer // 2
            prm = _odd_params(w_in_o16[o], sgu_g[o], sgu_b[o], sgu_w[o], sgu_bias[o], w_out_o16[o],
                              ln_mix_g[layer], ln_mix_b[layer])
            xp, _ = _odd_layer(xp, prm, emit_v=False)
            xs, sv = _odd_layer(xs, prm, emit_v=True)
            sgu_v.append(sv)
            mix_p = mix_s = None
        fprm = _ffn_params(layer, w_up16_all, ffn_conv_w[layer], ffn_conv_b[layer], w_down16_all,
                           ln_ffn_g[layer], ln_ffn_b[layer])
        zero_fc = jnp.zeros((xp.shape[0], FFN_CONV - 1, 2 * D_FF), F32)
        xp, fcp = _ffn_layer(xp, zero_fc, fprm, mix_p)
        xs, fcs = _ffn_layer(xs, state_ffn_conv[layer], fprm, mix_s)
        ffn_p.append(fcp)
        ffn_s.append(fcs)
    stack = lambda lst, idx: jnp.stack([item[idx] for item in lst])
    return (xp, xs,
            stack(even_p, 0), stack(even_p, 1), stack(even_p, 2),
            stack(even_s, 0), stack(even_s, 1), stack(even_s, 2),
            stack(even_p, 3), stack(even_p, 4), stack(even_s, 3), stack(even_s, 4),
            jnp.stack(ffn_p), jnp.stack(ffn_s), jnp.stack(sgu_v))
```

```python
import functools

import jax
import jax.numpy as jnp
from jax import lax
from jax.experimental import pallas as pl
from jax.experimental.pallas import tpu as pltpu

F32 = jnp.float32
BF16 = jnp.bfloat16

D_MODEL = 1024
N_HEADS = 8
HEAD_DIM = 64
A_WIDTH = N_HEADS * HEAD_DIM
B_WIDTH = 512
RG_CONV = 4
RG_C = 8.0
C_WIDTH = 1024
C_GROUPS = 8
C_LEN = 128
D_FF = 2816
FFN_CONV = 3
DEPTH = 2
ALPHA = (2 * DEPTH) ** 0.25
LN_EPS = 1e-5

LANES = 128
SUBLANES = 8
HALO = SUBLANES
NEG = -1e30
LOG2E = 1.4426950408889634
ROW_BLOCK = 512
FFN_CHUNK = 256
VMEM_LIMIT = 48 * 1024 * 1024


def _layer_norm(x, g, b):
    mu = jnp.mean(x, axis=-1, keepdims=True)
    xc = x - mu
    var = jnp.mean(xc * xc, axis=-1, keepdims=True)
    return xc * lax.rsqrt(var + LN_EPS) * g + b


def _log_sigmoid(x):
    return jnp.minimum(x, 0.0) - jnp.log1p(jnp.exp(-jnp.abs(x)))


def _softplus(x):
    return jnp.maximum(x, 0.0) + jnp.log1p(jnp.exp(-jnp.abs(x)))


def _dot(a, b):
    return jnp.dot(a, b, preferred_element_type=F32)


def _dot_nt(a, b):
    return lax.dot_general(a, b, (((1,), (1,)), ((), ())), preferred_element_type=F32)


def _const_spec(shape):
    nd = len(shape)
    return pl.BlockSpec(shape, lambda *_: (0,) * nd, pipeline_mode=pl.Buffered(1))


def _params(sem):
    return pltpu.CompilerParams(dimension_semantics=sem, vmem_limit_bytes=VMEM_LIMIT)


def _plan(n_rows, seq_len):
    tm = min(ROW_BLOCK, n_rows)
    nseq = max(1, tm // seq_len)
    bps = max(1, seq_len // tm)
    assert n_rows % tm == 0 and (seq_len % tm == 0 or tm % seq_len == 0)
    return tm, nseq, bps


def _even_in_kernel(x_ref, wqkv_ref, wf_ref, wxg_ref, bf_ref, cs_ref, h0_ref,
                    cw_ref, cb_ref, wa_ref, ba_ref, wx_ref, bx_ref, lam_ref,
                    qkv16_ref, k_ref, v_ref, logf_ref, cum_ref, ck_ref, gated_ref, xtail_ref, hl_ref,
                    carry_ref, ccarry_ref, hcarry_ref, *, tm, nseq, bps, kv_t):
    i = pl.program_id(0)
    lb = tm // nseq
    first = (i % bps) == 0
    if bps > 1:
        @pl.when(i == 0)
        def _():
            carry_ref[...] = jnp.zeros_like(carry_ref)
            ccarry_ref[...] = jnp.zeros_like(ccarry_ref)
            hcarry_ref[...] = jnp.zeros_like(hcarry_ref)
    x16 = x_ref[...].astype(BF16)
    xg = _dot(x16, wxg_ref[...])
    half = B_WIDTH // 2
    for cols in (slice(0, half), slice(half, B_WIDTH)):
        xb = xg[:, cols]
        if bps > 1:
            halos = [jnp.where(first, cs_ref[0, :, cols], ccarry_ref[:, cols])]
            inits = [jnp.where(first, h0_ref[0, :, cols], hcarry_ref[:, cols])]
            ccarry_ref[:, cols] = xb[tm - HALO:, :]
        else:
            halos = [cs_ref[s, :, cols] for s in range(nseq)]
            inits = [h0_ref[s, :, cols] for s in range(nseq)]
        pieces = []
        for s in range(nseq):
            xs = xb[s * lb:(s + 1) * lb, :]
            xtail_ref[s, :, cols] = xs[lb - HALO:, :]
            xc = cb_ref[:, cols] + cw_ref[RG_CONV - 1:RG_CONV, cols] * xs
            for d in range(1, RG_CONV):
                xc = xc + cw_ref[RG_CONV - 1 - d:RG_CONV - d, cols] * _shift_rows(xs, halos[s], d)
            pieces.append(xc)
        xc = pieces[0] if nseq == 1 else jnp.concatenate(pieces, axis=0)
        xc16 = xc.astype(BF16)
        gate_r = jax.nn.sigmoid(_dot(xc16, wa_ref[cols, cols]) + ba_ref[:, cols])
        gate_i = jax.nn.sigmoid(_dot(xc16, wx_ref[cols, cols]) + bx_ref[:, cols])
        log_a = -RG_C * gate_r * _softplus(-lam_ref[:, cols])
        a = jnp.exp(log_a)
        u = jnp.sqrt(jnp.tanh(-log_a) * (a * a + 1.0)) * (gate_i * xc)
        h = _linear_scan(a, u, inits, lb)
        if bps > 1:
            hcarry_ref[:, cols] = h[tm - 1:tm, :]
        for s in range(nseq):
            hl_ref[0, s:s + 1, cols] = h[(s + 1) * lb - 1:(s + 1) * lb, :]
        gb = xg[:, B_WIDTH + cols.start:B_WIDTH + cols.stop]
        gated_ref[:, cols] = (jax.nn.gelu(gb) * h).astype(BF16)

    qkv = _dot(x16, wqkv_ref[...])
    qkv16_ref[:, :A_WIDTH] = (qkv[:, :A_WIDTH] * (HEAD_DIM ** -0.5 * LOG2E)).astype(BF16)
    qkv16_ref[:, A_WIDTH:] = qkv[:, A_WIDTH:].astype(BF16)
    if kv_t:
        k_ref[0] = qkv[:, A_WIDTH:2 * A_WIDTH].T
        v_ref[0] = qkv[:, 2 * A_WIDTH:].T
    else:
        k_ref[...] = qkv[:, A_WIDTH:2 * A_WIDTH]
        v_ref[...] = qkv[:, 2 * A_WIDTH:]
    logf = _log_sigmoid(_dot(x16, wf_ref[...]) + bf_ref[...])
    logf_ref[...] = logf
    zero = jnp.zeros((1, LANES), F32)
    if bps > 1:
        cs = _cumsum_rows(logf, [jnp.where(first, zero, carry_ref[...])], lb)
        carry_ref[...] = cs[tm - 1:tm, :]
    else:
        cs = _cumsum_rows(logf, [zero] * nseq, lb)
    cum_ref[...] = cs
    cst = cs.T
    for g in range(tm // LANES):
        ck_ref[g] = cst[:N_HEADS, g * LANES:(g + 1) * LANES]


def _even_in(x2d, seq_len, conv_state, h0, prm, kv_t):
    n = x2d.shape[0]
    tm, nseq, bps = _plan(n, seq_len)
    nblk = n // tm
    cs = jnp.pad(conv_state, ((0, 0), (HALO - (RG_CONV - 1), 0), (0, 0)))
    h03 = h0.astype(F32)[:, None, :]
    state = lambda r: pl.BlockSpec((nseq, r, B_WIDTH), lambda i: (i // bps, 0, 0))
    consts = [prm[k] for k in ("wqkv", "wf", "wxg", "bf")]
    rg = [prm[k] for k in ("rg_cw", "rg_cb", "rg_wa", "rg_ba", "rg_wx", "rg_bx", "rg_lam")]
    if kv_t:
        assert nseq == 1
        kv_shape = jax.ShapeDtypeStruct((n // seq_len, A_WIDTH, seq_len), F32)
        kv_spec = pl.BlockSpec((1, A_WIDTH, tm), lambda i: (i // bps, 0, i % bps))
    else:
        kv_shape = jax.ShapeDtypeStruct((n, A_WIDTH), F32)
        kv_spec = pl.BlockSpec((tm, A_WIDTH), lambda i: (i, 0))
    row = lambda w: pl.BlockSpec((tm, w), lambda i: (i, 0))
    out_shapes = (
        jax.ShapeDtypeStruct((n, 3 * A_WIDTH), BF16),
        kv_shape,
        kv_shape,
        jax.ShapeDtypeStruct((n, LANES), F32),
        jax.ShapeDtypeStruct((n, LANES), F32),
        jax.ShapeDtypeStruct((n // LANES, N_HEADS, LANES), F32),
        jax.ShapeDtypeStruct((n, B_WIDTH), BF16),
        jax.ShapeDtypeStruct((nblk * nseq, HALO, B_WIDTH), F32),
        jax.ShapeDtypeStruct((nblk, nseq, B_WIDTH), F32),
    )
    return pl.pallas_call(
        functools.partial(_even_in_kernel, tm=tm, nseq=nseq, bps=bps, kv_t=kv_t),
        grid=(nblk,),
        in_specs=[row(D_MODEL)] + [_const_spec(c.shape) for c in consts]
                 + [state(HALO), state(1)] + [_const_spec(c.shape) for c in rg],
        out_specs=(row(3 * A_WIDTH), kv_spec, kv_spec, row(LANES), row(LANES),
                   pl.BlockSpec((tm // LANES, N_HEADS, LANES), lambda i: (i, 0, 0)), row(B_WIDTH),
                   pl.BlockSpec((nseq, HALO, B_WIDTH), lambda i: (i, 0, 0)),
                   pl.BlockSpec((1, nseq, B_WIDTH), lambda i: (i, 0, 0))),
        out_shape=out_shapes,
        scratch_shapes=[pltpu.VMEM((1, LANES), F32),
                        pltpu.VMEM((HALO, B_WIDTH), F32),
                        pltpu.VMEM((1, B_WIDTH), F32)],
        compiler_params=_params(("arbitrary",)),
        name="even_in_proj",
    )(x2d, *consts, cs, h03, *rg)


def _fox_prompt_kernel(q_ref, k_ref, v_ref, cq_ref, ck_ref, *rest, blk, ncast):
    cast_in, o_ref, cast_out = rest[:ncast], rest[ncast], rest[ncast + 1:2 * ncast + 1]
    qm_ref, cqr_ref, m_ref, acc_ref = rest[2 * ncast + 1:]
    for src, dst in zip(cast_in, cast_out):
        dst[...] = src[...].astype(BF16)

    i = pl.program_id(1)
    lane = lax.broadcasted_iota(jnp.int32, (blk, LANES), 1)
    ngrp = blk // LANES

    for h in range(N_HEADS):
        hp, hh = divmod(h, 2)
        q = q_ref[0, :, hp * LANES:(hp + 1) * LANES]
        keep = (lane < HEAD_DIM) if hh == 0 else (lane >= HEAD_DIM)
        qm_ref[h] = jnp.where(keep, q, jnp.zeros_like(q))
        cqr_ref[h] = jnp.broadcast_to(cq_ref[0, :, h:h + 1] * LOG2E, (blk, LANES))
    m_ref[...] = jnp.full(m_ref.shape, NEG, F32)
    acc_ref[...] = jnp.zeros_like(acc_ref)

    def update(kg0, ngrp, diagonal):
        koff = pl.multiple_of(kg0 * LANES, LANES)
        vlane = lax.broadcasted_iota(jnp.int32, (ngrp * LANES, LANES), 1)
        if diagonal:
            rows = lax.broadcasted_iota(jnp.int32, (blk, LANES), 0)
            own = ngrp - blk // LANES
            visible = [None] * own + [lane + g * LANES <= rows for g in range(blk // LANES)]
        for h in range(N_HEADS):
            hp, hh = divmod(h, 2)
            k16 = k_ref[0, pl.ds(koff, ngrp * LANES), hp * LANES:(hp + 1) * LANES]
            v16 = v_ref[0, pl.ds(koff, ngrp * LANES), hp * LANES:(hp + 1) * LANES]
            vkeep = (vlane < HEAD_DIM) if hh == 0 else (vlane >= HEAD_DIM)
            v16 = jnp.where(vkeep, v16, jnp.ones_like(v16))
            s = _dot_nt(qm_ref[h], k16)
            grp = [s[:, g * LANES:(g + 1) * LANES] - ck_ref[0, kg0 + g, h:h + 1, :] * LOG2E
                   for g in range(ngrp)]
            if diagonal:
                grp = [g if vis is None else jnp.where(vis, g, NEG) for vis, g in zip(visible, grp)]
            mc = grp[0]
            for g in grp[1:]:
                mc = jnp.maximum(mc, g)
            cqr = cqr_ref[h]
            m_old = m_ref[h]
            m_new = jnp.maximum(m_old, jnp.max(mc, axis=1, keepdims=True) + cqr)
            shift = m_new - cqr
            p16 = jnp.concatenate([jnp.exp2(g - shift).astype(BF16) for g in grp], axis=1)
            acc_ref[h] = jnp.exp2(m_old - m_new) * acc_ref[h] + _dot(p16, v16)
            m_ref[h] = m_new

    def body(j, carry):
        update(j * (2 * ngrp), 2 * ngrp, False)
        return carry

    lax.fori_loop(0, i // 2, body, 0)

    @pl.when(i % 2 == 1)
    def _():
        update((i - 1) * ngrp, 2 * ngrp, True)

    @pl.when(i % 2 == 0)
    def _():
        update(i * ngrp, ngrp, True)

    for hp in range(N_HEADS // 2):
        a0 = acc_ref[2 * hp]
        a1 = acc_ref[2 * hp + 1]
        o = jnp.where(lane < HEAD_DIM, a0 / a0[:, HEAD_DIM:HEAD_DIM + 1], a1 / a1[:, 0:1])
        o_ref[0, :, hp * LANES:(hp + 1) * LANES] = o.astype(o_ref.dtype)


def _fox_prompt(qkv16, cum, ck, blk, casts):
    bsz, t, _ = qkv16.shape
    nb = t // blk
    ng = t // LANES
    nsteps = bsz * nb
    ck = ck.reshape(bsz, ng, N_HEADS, LANES)
    slab = lambda w: pl.BlockSpec((w.shape[0] // nsteps, w.shape[1]), lambda b, i: (b * nb + i, 0))
    for w in casts:
        assert w.shape[0] % (nsteps * 2 * SUBLANES) == 0
    outs = pl.pallas_call(
        functools.partial(_fox_prompt_kernel, blk=blk, ncast=len(casts)),
        grid=(bsz, nb),
        in_specs=[pl.BlockSpec((1, blk, A_WIDTH), lambda b, i: (b, i, 0)),
                  pl.BlockSpec((1, t, A_WIDTH), lambda b, i: (b, 0, 1)),
                  pl.BlockSpec((1, t, A_WIDTH), lambda b, i: (b, 0, 2)),
                  pl.BlockSpec((1, blk, LANES), lambda b, i: (b, i, 0)),
                  pl.BlockSpec((1, ng, N_HEADS, LANES), lambda b, i: (b, 0, 0, 0))]
                 + [slab(w) for w in casts],
        out_specs=[pl.BlockSpec((1, blk, A_WIDTH), lambda b, i: (b, i, 0))] + [slab(w) for w in casts],
        out_shape=[jax.ShapeDtypeStruct((bsz, t, A_WIDTH), BF16)]
                  + [jax.ShapeDtypeStruct(w.shape, BF16) for w in casts],
        scratch_shapes=[pltpu.VMEM((N_HEADS, blk, LANES), BF16),
                        pltpu.VMEM((N_HEADS, blk, LANES), F32),
                        pltpu.VMEM((N_HEADS, blk, LANES), F32),
                        pltpu.VMEM((N_HEADS, blk, LANES), F32)],
        compiler_params=_params(("arbitrary", "arbitrary")),
        name="fox_prompt",
    )(qkv16, qkv16, qkv16, cum, ck, *casts)
    return outs[0], outs[1:]


def _fox_sample_kernel(qkv_ref, cq_ref, ckn_ref, lf_ref, kt_ref, vt_ref, o_ref, *, t, plen):
    lane = lax.broadcasted_iota(jnp.int32, (t, LANES), 1)
    lf = lf_ref[0]
    pos = lax.broadcasted_iota(jnp.int32, lf.shape, 1)
    suffix = lf
    d = 1
    while d < plen:
        suffix = suffix + jnp.where(pos + d < plen, pltpu.roll(suffix, plen - d, axis=1), 0.0)
        d *= 2
    ckc = lf - suffix
    rows = lax.broadcasted_iota(jnp.int32, (t, t), 0)
    cols = lax.broadcasted_iota(jnp.int32, (t, t), 1)
    visible = cols <= rows

    def per_head(s, bias_ref_rows, mask):
        halves = []
        for hh in range(2):
            part = s[hh * t:(hh + 1) * t, :] - bias_ref_rows[hh] * LOG2E
            halves.append(jnp.where(visible, part, NEG) if mask else part)
        return jnp.concatenate(halves, axis=0)

    for hp in range(N_HEADS // 2):
        h0 = 2 * hp
        sl = slice(hp * LANES, (hp + 1) * LANES)
        q = qkv_ref[0, :, hp * LANES:(hp + 1) * LANES]
        zero = jnp.zeros_like(q)
        qs = jnp.concatenate([jnp.where(lane < HEAD_DIM, q, zero),
                              jnp.where(lane >= HEAD_DIM, q, zero)], axis=0)
        kn16 = qkv_ref[0, :, A_WIDTH + hp * LANES:A_WIDTH + (hp + 1) * LANES]
        vn16 = qkv_ref[0, :, 2 * A_WIDTH + hp * LANES:2 * A_WIDTH + (hp + 1) * LANES]
        cq = jnp.concatenate([cq_ref[0, :, h0:h0 + 1], cq_ref[0, :, h0 + 1:h0 + 2]], axis=0) * LOG2E
        kt16 = kt_ref[0, h0:h0 + 2].reshape(LANES, plen).astype(BF16)
        vt16 = vt_ref[0, h0:h0 + 2].reshape(LANES, plen).astype(BF16)
        s_new = per_head(_dot_nt(qs, kn16), [ckn_ref[0, h0:h0 + 1, :], ckn_ref[0, h0 + 1:h0 + 2, :]], True)
        s_old = per_head(_dot(qs, kt16), [ckc[h0:h0 + 1, :], ckc[h0 + 1:h0 + 2, :]], False)
        m = jnp.maximum(jnp.max(s_new, axis=1, keepdims=True), jnp.max(s_old, axis=1, keepdims=True)) + cq
        shift = m - cq
        p_new = jnp.exp2(s_new - shift)
        p_old = jnp.exp2(s_old - shift)
        l = jnp.sum(p_new, axis=1, keepdims=True) + jnp.sum(p_old, axis=1, keepdims=True)
        acc = _dot(p_new.astype(BF16), vn16) + _dot_nt(p_old.astype(BF16), vt16)
        o = acc / l
        o_ref[0, :, sl] = jnp.where(lane < HEAD_DIM, o[:t, :], o[t:, :]).astype(o_ref.dtype)


def _fox_sample(qkv16, cum, cum_t, kt, vt, lf_t):
    bsz, t, _ = qkv16.shape
    plen = kt.shape[-1]
    per_b = lambda shape: pl.BlockSpec((1,) + shape, lambda b: (b,) + (0,) * len(shape))
    return pl.pallas_call(
        functools.partial(_fox_sample_kernel, t=t, plen=plen),
        grid=(bsz,),
        in_specs=[per_b((t, 3 * A_WIDTH)), per_b((t, LANES)), per_b((N_HEADS, t)),
                  per_b((N_HEADS, plen)),
                  per_b((N_HEADS, HEAD_DIM, plen)), per_b((N_HEADS, HEAD_DIM, plen))],
        out_specs=per_b((t, A_WIDTH)),
        out_shape=jax.ShapeDtypeStruct((bsz, t, A_WIDTH), BF16),
        compiler_params=_params(("arbitrary",)),
        name="fox_sample",
    )(qkv16, cum, cum_t, lf_t, kt, vt)


def _shift_rows(h, halo, d):
    rolled = pltpu.roll(h, d, axis=0)
    rowid = lax.broadcasted_iota(jnp.int32, halo.shape, 0)
    head = jnp.where(rowid < d, pltpu.roll(halo, d, axis=0), rolled[:HALO, :])
    return jnp.concatenate([head, rolled[HALO:, :]], axis=0)


def _linear_scan(a, u, inits, lb):
    rows, width = a.shape
    ngroups = rows // SUBLANES
    a3 = a.reshape(ngroups, SUBLANES, width)
    u3 = u.reshape(ngroups, SUBLANES, width)
    sub = lax.broadcasted_iota(jnp.int32, a3.shape, 1)
    d = 1
    while d < SUBLANES:
        inside = sub >= d
        u3 = jnp.where(inside, a3 * pltpu.roll(u3, d, axis=1) + u3, u3)
        a3 = jnp.where(inside, a3 * pltpu.roll(a3, d, axis=1), a3)
        d *= 2
    out = []
    carry = None
    for g in range(ngroups):
        if g % (lb // SUBLANES) == 0:
            carry = inits[g // (lb // SUBLANES)]
        hg = u3[g] + a3[g] * carry
        out.append(hg)
        carry = hg[SUBLANES - 1:, :]
    return jnp.concatenate(out, axis=0)


def _cumsum_rows(x, inits, lb):
    rows, width = x.shape
    ngroups = rows // SUBLANES
    x3 = x.reshape(ngroups, SUBLANES, width)
    sub = lax.broadcasted_iota(jnp.int32, x3.shape, 1)
    d = 1
    while d < SUBLANES:
        x3 = x3 + jnp.where(sub >= d, pltpu.roll(x3, d, axis=1), 0.0)
        d *= 2
    out = []
    carry = None
    for g in range(ngroups):
        if g % (lb // SUBLANES) == 0:
            carry = inits[g // (lb // SUBLANES)]
        cg = x3[g] + carry
        out.append(cg)
        carry = cg[SUBLANES - 1:, :]
    return jnp.concatenate(out, axis=0)


def _ffn_kernel(*refs, tm, nseq, bps, tn, mix):
    if mix:
        x_ref, attn_ref, gated_ref, woa_ref, wob_ref, mg_ref, mb_ref = refs[:7]
        refs = refs[7:]
    else:
        x_ref = refs[0]
        refs = refs[1:]
    st_ref, wup_ref, taps_ref, wd_ref, g_ref, b_ref, y_ref, so_ref, a_ref, carry_ref = refs
    i = pl.program_id(0)
    lb = tm // nseq
    if bps > 1:
        @pl.when(i == 0)
        def _():
            carry_ref[...] = jnp.zeros_like(carry_ref)
    x = x_ref[...]
    if mix:
        mixed = _dot(attn_ref[...], woa_ref[...]) + _dot(gated_ref[...], wob_ref[...])
        x = _layer_norm(ALPHA * x + mixed, mg_ref[...], mb_ref[...])
    x16 = x.astype(BF16)
    first = (i % bps) == 0

    def conv_branch(h, cols):
        taps = taps_ref[:, cols]
        outs = []
        for s in range(nseq):
            hs = h[s * lb:(s + 1) * lb, :]
            if bps > 1:
                halo = jnp.where(first, st_ref[0, :, cols], carry_ref[:, cols])
            else:
                halo = st_ref[s, :, cols]
            so_ref[s, :, cols] = hs[lb - HALO:, :]
            y = taps[3:4, :] + taps[2:3, :] * hs
            y = y + taps[1:2, :] * _shift_rows(hs, halo, 1)
            y = y + taps[0:1, :] * _shift_rows(hs, halo, 2)
            outs.append(y)
        if bps > 1:
            carry_ref[:, cols] = h[tm - HALO:, :]
        return outs

    for c in range(D_FF // tn):
        cg = slice(c * tn, (c + 1) * tn)
        cu = slice(D_FF + c * tn, D_FF + (c + 1) * tn)
        yg = conv_branch(_dot(x16, wup_ref[:, cg]), cg)
        yu = conv_branch(_dot(x16, wup_ref[:, cu]), cu)
        for s in range(nseq):
            a_ref[s * lb:(s + 1) * lb, cg] = (jax.nn.gelu(yg[s]) * yu[s]).astype(BF16)
    y = _dot(a_ref[...], wd_ref[...])
    y_ref[...] = _layer_norm(ALPHA * x + y, g_ref[...], b_ref[...])


def _ffn(x2d, state, seq_len, prm, mix=None):
    n = x2d.shape[0]
    tm, nseq, bps = _plan(n, seq_len)
    nblk = n // tm
    bsz = state.shape[0]
    st = jnp.pad(state, ((0, 0), (HALO - (FFN_CONV - 1), 0), (0, 0)))
    row = pl.BlockSpec((tm, D_MODEL), lambda i: (i, 0))
    consts = [prm[k] for k in ("w_up", "taps", "w_d", "ln_g", "ln_b")]
    layer = prm["layer"]
    of_layer = lambda w: pl.BlockSpec((None,) + w.shape[1:], lambda i: (layer, 0, 0), pipeline_mode=pl.Buffered(1))
    const_specs = [of_layer(c) if c.ndim == 3 else _const_spec(c.shape) for c in consts]
    lead, lead_specs = [x2d], [row]
    if mix is not None:
        attn16, gated16, eprm = mix
        mix_consts = [eprm[k] for k in ("wo_a", "wo_b", "ln_g", "ln_b")]
        lead += [attn16, gated16] + mix_consts
        lead_specs += [pl.BlockSpec((tm, A_WIDTH), lambda i: (i, 0)), pl.BlockSpec((tm, B_WIDTH), lambda i: (i, 0))]
        lead_specs += [_const_spec(c.shape) for c in mix_consts]
    y, so = pl.pallas_call(
        functools.partial(_ffn_kernel, tm=tm, nseq=nseq, bps=bps, tn=FFN_CHUNK, mix=mix is not None),
        grid=(nblk,),
        in_specs=lead_specs + [pl.BlockSpec((nseq, HALO, 2 * D_FF), lambda i: (i // bps, 0, 0))] + const_specs,
        out_specs=(row, pl.BlockSpec((nseq, HALO, 2 * D_FF), lambda i: (i, 0, 0))),
        out_shape=(jax.ShapeDtypeStruct((n, D_MODEL), F32),
                   jax.ShapeDtypeStruct((nblk * nseq, HALO, 2 * D_FF), F32)),
        scratch_shapes=[pltpu.VMEM((tm, D_FF), BF16),
                        pltpu.VMEM((HALO, 2 * D_FF), F32)],
        compiler_params=_params(("arbitrary",)),
        name="conv_ffn",
    )(*lead, st, *consts)
    so = so.reshape(bsz, -1, HALO, 2 * D_FF)[:, -1]
    return y, so[:, HALO - (FFN_CONV - 1):, :]


def _odd_kernel(x_ref, wu_ref, wv_ref, sg_ref, sb_ref, ws_ref, bias_ref, wo_ref, g_ref, b_ref,
                y_ref, *rest, tm, lc, emit_v):
    gate_ref = rest[-1]
    x = x_ref[...]
    x16 = x.astype(BF16)
    u = jax.nn.gelu(_dot(x16, wu_ref[...]))
    v = _layer_norm(jax.nn.gelu(_dot(x16, wv_ref[...])), sg_ref[...], sb_ref[...])
    if emit_v:
        rest[0][...] = v
    v16 = v.astype(BF16)
    gw = C_WIDTH // C_GROUPS
    for c in range(tm // lc):
        rows = slice(c * lc, (c + 1) * lc)
        for g in range(C_GROUPS):
            cols = slice(g * gw, (g + 1) * gw)
            mixed = _dot(ws_ref[g], v16[rows, cols]) + bias_ref[:, cols]
            gate_ref[rows, cols] = (u[rows, cols] * mixed).astype(BF16)
    y = _dot(gate_ref[...], wo_ref[...])
    y_ref[...] = _layer_norm(ALPHA * x + y, g_ref[...], b_ref[...])


def _odd(x2d, seq_len, prm, emit_v):
    n = x2d.shape[0]
    tm = min(ROW_BLOCK, n)
    lc = min(seq_len, C_LEN)
    gw = C_WIDTH // C_GROUPS
    ws16 = (prm["sgu_w"][:, :lc, :lc] * jnp.tril(jnp.ones((lc, lc), F32))).astype(BF16)
    bias = jnp.repeat(prm["sgu_bias"][:, :lc].T, gw, axis=1)
    consts = [prm["sgu_g"], prm["sgu_b"], ws16, bias, prm["w_o"], prm["ln_g"], prm["ln_b"]]
    w_half = lambda j: pl.BlockSpec((D_MODEL, C_WIDTH), lambda i: (0, j), pipeline_mode=pl.Buffered(1))
    row = pl.BlockSpec((tm, D_MODEL), lambda i: (i, 0))
    n_out = 2 if emit_v else 1
    return pl.pallas_call(
        functools.partial(_odd_kernel, tm=tm, lc=lc, emit_v=emit_v),
        grid=(n // tm,),
        in_specs=[row, w_half(0), w_half(1)] + [_const_spec(c.shape) for c in consts],
        out_specs=(row, pl.BlockSpec((tm, C_WIDTH), lambda i: (i, 0)))[:n_out],
        out_shape=(jax.ShapeDtypeStruct((n, D_MODEL), F32),
                   jax.ShapeDtypeStruct((n, C_WIDTH), F32))[:n_out],
        scratch_shapes=[pltpu.VMEM((tm, C_WIDTH), BF16)],
        compiler_params=_params(("arbitrary",)),
        name="odd_mixer",
    )(x2d, prm["w_in"], prm["w_in"], *consts)


def _row(v):
    return v.astype(F32)[None, :]


def _block_diag(w):
    nb, bd, _ = w.shape
    eye = jnp.eye(nb, dtype=w.dtype)
    return jnp.einsum("ncd,nm->ncmd", w, eye).reshape(nb * bd, nb * bd)


def _even_params(w_in, b_f, rg_conv_w, rg_conv_b, rg_wa, rg_ba, rg_wx, rg_bx, rg_lam, w_out, ln_g, ln_b):
    nf = 3 * A_WIDTH
    wf = jnp.pad(w_in[:, nf:nf + N_HEADS], ((0, 0), (0, LANES - N_HEADS)))
    return {
        "wqkv": w_in[:, :nf].astype(BF16),
        "wf": wf.astype(BF16),
        "wxg": w_in[:, nf + N_HEADS:].astype(BF16),
        "bf": jnp.pad(b_f, (0, LANES - N_HEADS))[None, :],
        "rg_cw": rg_conv_w, "rg_cb": _row(rg_conv_b),
        "rg_wa": _block_diag(rg_wa).astype(BF16), "rg_ba": _row(rg_ba),
        "rg_wx": _block_diag(rg_wx).astype(BF16), "rg_bx": _row(rg_bx),
        "rg_lam": _row(rg_lam),
        "wo_a": w_out[:A_WIDTH].astype(BF16), "wo_b": w_out[A_WIDTH:].astype(BF16),
        "ln_g": _row(ln_g), "ln_b": _row(ln_b),
    }


def _ffn_params(layer, w_up16_all, conv_w, conv_b, w_down16_all, ln_g, ln_b):
    taps = jnp.concatenate([conv_w, conv_b[None, :],
                            jnp.zeros((HALO - FFN_CONV - 1, 2 * D_FF), F32)], axis=0)
    return {"layer": layer, "w_up": w_up16_all, "taps": taps, "w_d": w_down16_all,
            "ln_g": _row(ln_g), "ln_b": _row(ln_b)}


def _odd_params(w_in16, sgu_g, sgu_b, sgu_w, sgu_bias, w_out16, ln_g, ln_b):
    return {
        "w_in": w_in16,
        "sgu_g": _row(sgu_g), "sgu_b": _row(sgu_b), "sgu_w": sgu_w, "sgu_bias": sgu_bias,
        "w_o": w_out16, "ln_g": _row(ln_g), "ln_b": _row(ln_b),
    }


def _even_layer(x3d, past, prm, casts=()):
    bsz, t, _ = x3d.shape
    x2d = x3d.reshape(bsz * t, D_MODEL)
    assert t >= HALO
    if past is None:
        conv_s = jnp.zeros((bsz, RG_CONV - 1, B_WIDTH), F32)
        h0 = jnp.zeros((bsz, B_WIDTH), F32)
    else:
        k_c, v_c, lf_c, conv_s, h0 = past
    kv_t = t >= ROW_BLOCK
    qkv16, k, v, logf, cum, ck, gated16, xtail, hl = _even_in(x2d, t, conv_s, h0, prm, kv_t)
    qkv16 = qkv16.reshape(bsz, t, 3 * A_WIDTH)
    cum3 = cum.reshape(bsz, t, LANES)
    cast16 = ()
    if past is None:
        attn, cast16 = _fox_prompt(qkv16, cum3, ck, ROW_BLOCK, list(casts))
    else:
        cum_t = cum3[:, :, :N_HEADS].transpose(0, 2, 1)
        attn = _fox_sample(qkv16, cum3, cum_t, k_c.transpose(0, 2, 3, 1), v_c.transpose(0, 2, 3, 1),
                           lf_c.transpose(0, 2, 1))
    if kv_t:
        new_k = k.reshape(bsz, N_HEADS, HEAD_DIM, t).transpose(0, 3, 1, 2)
        new_v = v.reshape(bsz, N_HEADS, HEAD_DIM, t).transpose(0, 3, 1, 2)
    else:
        new_k = k.reshape(bsz, t, N_HEADS, HEAD_DIM)
        new_v = v.reshape(bsz, t, N_HEADS, HEAD_DIM)
    new_logf = logf.reshape(bsz, t, LANES)[:, :, :N_HEADS]
    h_last = hl.reshape(bsz, -1, B_WIDTH)[:, -1, :]
    conv_new = xtail.reshape(bsz, -1, HALO, B_WIDTH)[:, -1, HALO - (RG_CONV - 1):, :]
    return (attn.reshape(bsz * t, A_WIDTH), gated16), (new_k, new_v, new_logf, conv_new, h_last), cast16


def _ffn_layer(x3d, state, prm, mix=None):
    bsz, t, _ = x3d.shape
    y, new_state = _ffn(x3d.reshape(bsz * t, D_MODEL), state, t, prm, mix)
    return y.reshape(bsz, t, D_MODEL), new_state


def _odd_layer(x3d, prm, emit_v):
    bsz, t, _ = x3d.shape
    outs = _odd(x3d.reshape(bsz * t, D_MODEL), t, prm, emit_v)
    v = outs[1].reshape(bsz, t, C_WIDTH) if emit_v else None
    return outs[0].reshape(bsz, t, D_MODEL), v


def kernel(x_prompt, x_sample, cache_k, cache_v, cache_logf, state_rglru_conv, state_rglru_h, state_ffn_conv, w_in_e, b_f, rg_conv_w, rg_conv_b, rg_wa, rg_ba, rg_wx, rg_bx, rg_lam, w_out_e, w_in_o, sgu_g, sgu_b, sgu_w, sgu_bias, w_out_o, ln_mix_g, ln_mix_b, ln_ffn_g, ln_ffn_b, ffn_w_up, ffn_conv_w, ffn_conv_b, ffn_w_down):
    xp, xs = x_prompt, x_sample
    later_weights = [ffn_w_up.reshape(-1, 2 * D_FF), ffn_w_down.reshape(-1, D_MODEL),
                     w_in_o.reshape(-1, 2 * C_WIDTH), w_out_o.reshape(-1, D_MODEL)]
    even_p, even_s, ffn_p, ffn_s, sgu_v = [], [], [], [], []
    for layer in range(DEPTH):
        if layer % 2 == 0:
            e = layer // 2
            prm = _even_params(w_in_e[e], b_f[e], rg_conv_w[e], rg_conv_b[e], rg_wa[e], rg_ba[e],
                               rg_wx[e], rg_bx[e], rg_lam[e], w_out_e[e],
                               ln_mix_g[layer], ln_mix_b[layer])
            branches_p, st_p, cast16 = _even_layer(xp, None, prm, later_weights if layer == 0 else ())
            if layer == 0:
                w_up16_all = cast16[0].reshape(ffn_w_up.shape)
                w_down16_all = cast16[1].reshape(ffn_w_down.shape)
                w_in_o16 = cast16[2].reshape(w_in_o.shape)
                w_out_o16 = cast16[3].reshape(w_out_o.shape)
            past = (cache_k[e], cache_v[e], cache_logf[e], state_rglru_conv[e], state_rglru_h[e])
            branches_s, st_s, _ = _even_layer(xs, past, prm)
            mix_p, mix_s = branches_p + (prm,), branches_s + (prm,)
            even_p.append(st_p)
            even_s.append(st_s)
        else:
            o = layer // 2
            prm = _odd_params(w_in_o16[o], sgu_g[o], sgu_b[o], sgu_w[o], sgu_bias[o], w_out_o16[o],
                              ln_mix_g[layer], ln_mix_b[layer])
            xp, _ = _odd_layer(xp, prm, emit_v=False)
            xs, sv = _odd_layer(xs, prm, emit_v=True)
            sgu_v.append(sv)
            mix_p = mix_s = None
        fprm = _ffn_params(layer, w_up16_all, ffn_conv_w[layer], ffn_conv_b[layer], w_down16_all,
                           ln_ffn_g[layer], ln_ffn_b[layer])
        zero_fc = jnp.zeros((xp.shape[0], FFN_CONV - 1, 2 * D_FF), F32)
        xp, fcp = _ffn_layer(xp, zero_fc, fprm, mix_p)
        xs, fcs = _ffn_layer(xs, state_ffn_conv[layer], fprm, mix_s)
        ffn_p.append(fcp)
        ffn_s.append(fcs)
    stack = lambda lst, idx: jnp.stack([item[idx] for item in lst])
    return (xp, xs,
            stack(even_p, 0), stack(even_p, 1), stack(even_p, 2),
            stack(even_s, 0), stack(even_s, 1), stack(even_s, 2),
            stack(even_p, 3), stack(even_p, 4), stack(even_s, 3), stack(even_s, 4),
            jnp.stack(ffn_p), jnp.stack(ffn_s), jnp.stack(sgu_v))
```

```python
import functools

import jax
import jax.numpy as jnp
from jax import lax
from jax.experimental import pallas as pl
from jax.experimental.pallas import tpu as pltpu

F32 = jnp.float32
BF16 = jnp.bfloat16

D_MODEL = 1024
N_HEADS = 8
HEAD_DIM = 64
A_WIDTH = N_HEADS * HEAD_DIM
B_WIDTH = 512
RG_CONV = 4
RG_C = 8.0
C_WIDTH = 1024
C_GROUPS = 8
C_LEN = 128
D_FF = 2816
FFN_CONV = 3
DEPTH = 2
ALPHA = (2 * DEPTH) ** 0.25
LN_EPS = 1e-5

LANES = 128
SUBLANES = 8
HALO = SUBLANES
NEG = -1e30
LOG2E = 1.4426950408889634
ROW_BLOCK = 512
FFN_CHUNK = 256
VMEM_LIMIT = 48 * 1024 * 1024


def _layer_norm(x, g, b):
    mu = jnp.mean(x, axis=-1, keepdims=True)
    xc = x - mu
    var = jnp.mean(xc * xc, axis=-1, keepdims=True)
    return xc * lax.rsqrt(var + LN_EPS) * g + b


def _log_sigmoid(x):
    return jnp.minimum(x, 0.0) - jnp.log1p(jnp.exp(-jnp.abs(x)))


def _softplus(x):
    return jnp.maximum(x, 0.0) + jnp.log1p(jnp.exp(-jnp.abs(x)))


def _dot(a, b):
    return jnp.dot(a, b, preferred_element_type=F32)


def _dot_nt(a, b):
    return lax.dot_general(a, b, (((1,), (1,)), ((), ())), preferred_element_type=F32)


def _const_spec(shape):
    nd = len(shape)
    return pl.BlockSpec(shape, lambda *_: (0,) * nd, pipeline_mode=pl.Buffered(1))


def _params(sem):
    return pltpu.CompilerParams(dimension_semantics=sem, vmem_limit_bytes=VMEM_LIMIT)


def _plan(n_rows, seq_len):
    tm = min(ROW_BLOCK, n_rows)
    nseq = max(1, tm // seq_len)
    bps = max(1, seq_len // tm)
    assert n_rows % tm == 0 and (seq_len % tm == 0 or tm % seq_len == 0)
    return tm, nseq, bps


def _even_in_kernel(x_ref, wqkv_ref, wf_ref, wxg_ref, bf_ref, cs_ref, h0_ref,
                    cw_ref, cb_ref, wa_ref, ba_ref, wx_ref, bx_ref, lam_ref,
                    qkv16_ref, k_ref, v_ref, logf_ref, cum_ref, ck_ref, gated_ref, xtail_ref, hl_ref,
                    carry_ref, ccarry_ref, hcarry_ref, *, tm, nseq, bps, kv_t):
    i = pl.program_id(0)
    lb = tm // nseq
    first = (i % bps) == 0
    if bps > 1:
        @pl.when(i == 0)
        def _():
            carry_ref[...] = jnp.zeros_like(carry_ref)
            ccarry_ref[...] = jnp.zeros_like(ccarry_ref)
            hcarry_ref[...] = jnp.zeros_like(hcarry_ref)
    x16 = x_ref[...].astype(BF16)
    xg = _dot(x16, wxg_ref[...])
    half = B_WIDTH // 2
    for cols in (slice(0, half), slice(half, B_WIDTH)):
        xb = xg[:, cols]
        if bps > 1:
            halos = [jnp.where(first, cs_ref[0, :, cols], ccarry_ref[:, cols])]
            inits = [jnp.where(first, h0_ref[0, :, cols], hcarry_ref[:, cols])]
            ccarry_ref[:, cols] = xb[tm - HALO:, :]
        else:
            halos = [cs_ref[s, :, cols] for s in range(nseq)]
            inits = [h0_ref[s, :, cols] for s in range(nseq)]
        pieces = []
        for s in range(nseq):
            xs = xb[s * lb:(s + 1) * lb, :]
            xtail_ref[s, :, cols] = xs[lb - HALO:, :]
            xc = cb_ref[:, cols] + cw_ref[RG_CONV - 1:RG_CONV, cols] * xs
            for d in range(1, RG_CONV):
                xc = xc + cw_ref[RG_CONV - 1 - d:RG_CONV - d, cols] * _shift_rows(xs, halos[s], d)
            pieces.append(xc)
        xc = pieces[0] if nseq == 1 else jnp.concatenate(pieces, axis=0)
        xc16 = xc.astype(BF16)
        gate_r = jax.nn.sigmoid(_dot(xc16, wa_ref[cols, cols]) + ba_ref[:, cols])
        gate_i = jax.nn.sigmoid(_dot(xc16, wx_ref[cols, cols]) + bx_ref[:, cols])
        log_a = -RG_C * gate_r * _softplus(-lam_ref[:, cols])
        a = jnp.exp(log_a)
        u = jnp.sqrt(jnp.tanh(-log_a) * (a * a + 1.0)) * (gate_i * xc)
        h = _linear_scan(a, u, inits, lb)
        if bps > 1:
            hcarry_ref[:, cols] = h[tm - 1:tm, :]
        for s in range(nseq):
            hl_ref[0, s:s + 1, cols] = h[(s + 1) * lb - 1:(s + 1) * lb, :]
        gb = xg[:, B_WIDTH + cols.start:B_WIDTH + cols.stop]
        gated_ref[:, cols] = (jax.nn.gelu(gb) * h).astype(BF16)

    qkv = _dot(x16, wqkv_ref[...])
    qkv16_ref[:, :A_WIDTH] = (qkv[:, :A_WIDTH] * (HEAD_DIM ** -0.5 * LOG2E)).astype(BF16)
    qkv16_ref[:, A_WIDTH:] = qkv[:, A_WIDTH:].astype(BF16)
    if kv_t:
        k_ref[0] = qkv[:, A_WIDTH:2 * A_WIDTH].T
        v_ref[0] = qkv[:, 2 * A_WIDTH:].T
    else:
        k_ref[...] = qkv[:, A_WIDTH:2 * A_WIDTH]
        v_ref[...] = qkv[:, 2 * A_WIDTH:]
    logf = _log_sigmoid(_dot(x16, wf_ref[...]) + bf_ref[...])
    logf_ref[...] = logf
    zero = jnp.zeros((1, LANES), F32)
    if bps > 1:
        cs = _cumsum_rows(logf, [jnp.where(first, zero, carry_ref[...])], lb)
        carry_ref[...] = cs[tm - 1:tm, :]
    else:
        cs = _cumsum_rows(logf, [zero] * nseq, lb)
    cum_ref[...] = cs
    cst = cs.T
    for g in range(tm // LANES):
        ck_ref[g] = cst[:N_HEADS, g * LANES:(g + 1) * LANES]


def _even_in(x2d, seq_len, conv_state, h0, prm, kv_t):
    n = x2d.shape[0]
    tm, nseq, bps = _plan(n, seq_len)
    nblk = n // tm
    cs = jnp.pad(conv_state, ((0, 0), (HALO - (RG_CONV - 1), 0), (0, 0)))
    h03 = h0.astype(F32)[:, None, :]
    state = lambda r: pl.BlockSpec((nseq, r, B_WIDTH), lambda i: (i // bps, 0, 0))
    consts = [prm[k] for k in ("wqkv", "wf", "wxg", "bf")]
    rg = [prm[k] for k in ("rg_cw", "rg_cb", "rg_wa", "rg_ba", "rg_wx", "rg_bx", "rg_lam")]
    if kv_t:
        assert nseq == 1
        kv_shape = jax.ShapeDtypeStruct((n // seq_len, A_WIDTH, seq_len), F32)
        kv_spec = pl.BlockSpec((1, A_WIDTH, tm), lambda i: (i // bps, 0, i % bps))
    else:
        kv_shape = jax.ShapeDtypeStruct((n, A_WIDTH), F32)
        kv_spec = pl.BlockSpec((tm, A_WIDTH), lambda i: (i, 0))
    row = lambda w: pl.BlockSpec((tm, w), lambda i: (i, 0))
    out_shapes = (
        jax.ShapeDtypeStruct((n, 3 * A_WIDTH), BF16),
        kv_shape,
        kv_shape,
        jax.ShapeDtypeStruct((n, LANES), F32),
        jax.ShapeDtypeStruct((n, LANES), F32),
        jax.ShapeDtypeStruct((n // LANES, N_HEADS, LANES), F32),
        jax.ShapeDtypeStruct((n, B_WIDTH), BF16),
        jax.ShapeDtypeStruct((nblk * nseq, HALO, B_WIDTH), F32),
        jax.ShapeDtypeStruct((nblk, nseq, B_WIDTH), F32),
    )
    return pl.pallas_call(
        functools.partial(_even_in_kernel, tm=tm, nseq=nseq, bps=bps, kv_t=kv_t),
        grid=(nblk,),
        in_specs=[row(D_MODEL)] + [_const_spec(c.shape) for c in consts]
                 + [state(HALO), state(1)] + [_const_spec(c.shape) for c in rg],
        out_specs=(row(3 * A_WIDTH), kv_spec, kv_spec, row(LANES), row(LANES),
                   pl.BlockSpec((tm // LANES, N_HEADS, LANES), lambda i: (i, 0, 0)), row(B_WIDTH),
                   pl.BlockSpec((nseq, HALO, B_WIDTH), lambda i: (i, 0, 0)),
                   pl.BlockSpec((1, nseq, B_WIDTH), lambda i: (i, 0, 0))),
        out_shape=out_shapes,
        scratch_shapes=[pltpu.VMEM((1, LANES), F32),
                        pltpu.VMEM((HALO, B_WIDTH), F32),
                        pltpu.VMEM((1, B_WIDTH), F32)],
        compiler_params=_params(("arbitrary",)),
        name="even_in_proj",
    )(x2d, *consts, cs, h03, *rg)


def _fox_prompt_kernel(q_ref, k_ref, v_ref, cq_ref, ck_ref, *rest, blk, ncast):
    cast_in, o_ref, cast_out = rest[:ncast], rest[ncast], rest[ncast + 1:2 * ncast + 1]
    qm_ref, cqr_ref, m_ref, acc_ref = rest[2 * ncast + 1:]
    for src, dst in zip(cast_in, cast_out):
        dst[...] = src[...].astype(BF16)

    i = pl.program_id(1)
    lane = lax.broadcasted_iota(jnp.int32, (blk, LANES), 1)
    ngrp = blk // LANES

    for h in range(N_HEADS):
        hp, hh = divmod(h, 2)
        q = q_ref[0, :, hp * LANES:(hp + 1) * LANES]
        keep = (lane < HEAD_DIM) if hh == 0 else (lane >= HEAD_DIM)
        qm_ref[h] = jnp.where(keep, q, jnp.zeros_like(q))
        cqr_ref[h] = jnp.broadcast_to(cq_ref[0, :, h:h + 1] * LOG2E, (blk, LANES))

    def update(kg0, ngrp, diagonal):
        koff = pl.multiple_of(kg0 * LANES, LANES)
        vlane = lax.broadcasted_iota(jnp.int32, (ngrp * LANES, LANES), 1)
        if diagonal:
            rows = lax.broadcasted_iota(jnp.int32, (blk, LANES), 0)
            own = ngrp - blk // LANES
            visible = [None] * own + [lane + g * LANES <= rows for g in range(blk // LANES)]
        for h in range(N_HEADS):
            hp, hh = divmod(h, 2)
            k16 = k_ref[0, pl.ds(koff, ngrp * LANES), hp * LANES:(hp + 1) * LANES]
            v16 = v_ref[0, pl.ds(koff, ngrp * LANES), hp * LANES:(hp + 1) * LANES]
            vkeep = (vlane < HEAD_DIM) if hh == 0 else (vlane >= HEAD_DIM)
            v16 = jnp.where(vkeep, v16, jnp.ones_like(v16))
            s = _dot_nt(qm_ref[h], k16)
            grp = [s[:, g * LANES:(g + 1) * LANES] - ck_ref[0, kg0 + g, h:h + 1, :] * LOG2E
                   for g in range(ngrp)]
            if diagonal:
                grp = [g if vis is None else jnp.where(vis, g, NEG) for vis, g in zip(visible, grp)]
            mc = grp[0]
            for g in grp[1:]:
                mc = jnp.maximum(mc, g)
            cqr = cqr_ref[h]
            m_blk = jnp.max(mc, axis=1, keepdims=True) + cqr
            m_new = m_blk if diagonal else jnp.maximum(m_ref[h], m_blk)
            shift = m_new - cqr
            p16 = jnp.concatenate([jnp.exp2(g - shift).astype(BF16) for g in grp], axis=1)
            pv = _dot(p16, v16)
            acc_ref[h] = pv if diagonal else jnp.exp2(m_ref[h] - m_new) * acc_ref[h] + pv
            m_ref[h] = m_new

    @pl.when(i % 2 == 1)
    def _():
        update((i - 1) * ngrp, 2 * ngrp, True)

    @pl.when(i % 2 == 0)
    def _():
        update(i * ngrp, ngrp, True)

    def body(j, carry):
        update(j * (2 * ngrp), 2 * ngrp, False)
        return carry

    lax.fori_loop(0, i // 2, body, 0)

    for hp in range(N_HEADS // 2):
        a0 = acc_ref[2 * hp]
        a1 = acc_ref[2 * hp + 1]
        o = jnp.where(lane < HEAD_DIM, a0 / a0[:, HEAD_DIM:HEAD_DIM + 1], a1 / a1[:, 0:1])
        o_ref[0, :, hp * LANES:(hp + 1) * LANES] = o.astype(o_ref.dtype)


def _fox_prompt(qkv16, cum, ck, blk, casts):
    bsz, t, _ = qkv16.shape
    nb = t // blk
    ng = t // LANES
    nsteps = bsz * nb
    ck = ck.reshape(bsz, ng, N_HEADS, LANES)
    slab = lambda w: pl.BlockSpec((w.shape[0] // nsteps, w.shape[1]), lambda b, i: (b * nb + i, 0))
    for w in casts:
        assert w.shape[0] % (nsteps * 2 * SUBLANES) == 0
    outs = pl.pallas_call(
        functools.partial(_fox_prompt_kernel, blk=blk, ncast=len(casts)),
        grid=(bsz, nb),
        in_specs=[pl.BlockSpec((1, blk, A_WIDTH), lambda b, i: (b, i, 0)),
                  pl.BlockSpec((1, t, A_WIDTH), lambda b, i: (b, 0, 1)),
                  pl.BlockSpec((1, t, A_WIDTH), lambda b, i: (b, 0, 2)),
                  pl.BlockSpec((1, blk, LANES), lambda b, i: (b, i, 0)),
                  pl.BlockSpec((1, ng, N_HEADS, LANES), lambda b, i: (b, 0, 0, 0))]
                 + [slab(w) for w in casts],
        out_specs=[pl.BlockSpec((1, blk, A_WIDTH), lambda b, i: (b, i, 0))] + [slab(w) for w in casts],
        out_shape=[jax.ShapeDtypeStruct((bsz, t, A_WIDTH), BF16)]
                  + [jax.ShapeDtypeStruct(w.shape, BF16) for w in casts],
        scratch_shapes=[pltpu.VMEM((N_HEADS, blk, LANES), BF16),
                        pltpu.VMEM((N_HEADS, blk, LANES), F32),
                        pltpu.VMEM((N_HEADS, blk, LANES), F32),
                        pltpu.VMEM((N_HEADS, blk, LANES), F32)],
        compiler_params=_params(("arbitrary", "arbitrary")),
        name="fox_prompt",
    )(qkv16, qkv16, qkv16, cum, ck, *casts)
    return outs[0], outs[1:]


def _fox_sample_kernel(qkv_ref, cq_ref, ckn_ref, lf_ref, kt_ref, vt_ref, o_ref, *, t, plen):
    lane = lax.broadcasted_iota(jnp.int32, (t, LANES), 1)
    lf = lf_ref[0]
    pos = lax.broadcasted_iota(jnp.int32, lf.shape, 1)
    suffix = lf
    d = 1
    while d < plen:
        suffix = suffix + jnp.where(pos + d < plen, pltpu.roll(suffix, plen - d, axis=1), 0.0)
        d *= 2
    ckc = lf - suffix
    rows = lax.broadcasted_iota(jnp.int32, (t, t), 0)
    cols = lax.broadcasted_iota(jnp.int32, (t, t), 1)
    visible = cols <= rows

    def per_head(s, bias_ref_rows, mask):
        halves = []
        for hh in range(2):
            part = s[hh * t:(hh + 1) * t, :] - bias_ref_rows[hh] * LOG2E
            halves.append(jnp.where(visible, part, NEG) if mask else part)
        return jnp.concatenate(halves, axis=0)

    for hp in range(N_HEADS // 2):
        h0 = 2 * hp
        sl = slice(hp * LANES, (hp + 1) * LANES)
        q = qkv_ref[0, :, hp * LANES:(hp + 1) * LANES]
        zero = jnp.zeros_like(q)
        qs = jnp.concatenate([jnp.where(lane < HEAD_DIM, q, zero),
                              jnp.where(lane >= HEAD_DIM, q, zero)], axis=0)
        kn16 = qkv_ref[0, :, A_WIDTH + hp * LANES:A_WIDTH + (hp + 1) * LANES]
        vn16 = qkv_ref[0, :, 2 * A_WIDTH + hp * LANES:2 * A_WIDTH + (hp + 1) * LANES]
        cq = jnp.concatenate([cq_ref[0, :, h0:h0 + 1], cq_ref[0, :, h0 + 1:h0 + 2]], axis=0) * LOG2E
        kt16 = kt_ref[0, h0:h0 + 2].reshape(LANES, plen).astype(BF16)
        vt16 = vt_ref[0, h0:h0 + 2].reshape(LANES, plen).astype(BF16)
        s_new = per_head(_dot_nt(qs, kn16), [ckn_ref[0, h0:h0 + 1, :], ckn_ref[0, h0 + 1:h0 + 2, :]], True)
        s_old = per_head(_dot(qs, kt16), [ckc[h0:h0 + 1, :], ckc[h0 + 1:h0 + 2, :]], False)
        m = jnp.maximum(jnp.max(s_new, axis=1, keepdims=True), jnp.max(s_old, axis=1, keepdims=True)) + cq
        shift = m - cq
        p_new = jnp.exp2(s_new - shift)
        p_old = jnp.exp2(s_old - shift)
        l = jnp.sum(p_new, axis=1, keepdims=True) + jnp.sum(p_old, axis=1, keepdims=True)
        acc = _dot(p_new.astype(BF16), vn16) + _dot_nt(p_old.astype(BF16), vt16)
        o = acc / l
        o_ref[0, :, sl] = jnp.where(lane < HEAD_DIM, o[:t, :], o[t:, :]).astype(o_ref.dtype)


def _fox_sample(qkv16, cum, cum_t, kt, vt, lf_t):
    bsz, t, _ = qkv16.shape
    plen = kt.shape[-1]
    per_b = lambda shape: pl.BlockSpec((1,) + shape, lambda b: (b,) + (0,) * len(shape))
    return pl.pallas_call(
        functools.partial(_fox_sample_kernel, t=t, plen=plen),
        grid=(bsz,),
        in_specs=[per_b((t, 3 * A_WIDTH)), per_b((t, LANES)), per_b((N_HEADS, t)),
                  per_b((N_HEADS, plen)),
                  per_b((N_HEADS, HEAD_DIM, plen)), per_b((N_HEADS, HEAD_DIM, plen))],
        out_specs=per_b((t, A_WIDTH)),
        out_shape=jax.ShapeDtypeStruct((bsz, t, A_WIDTH), BF16),
        compiler_params=_params(("arbitrary",)),
        name="fox_sample",
    )(qkv16, cum, cum_t, lf_t, kt, vt)


def _shift_rows(h, halo, d):
    rolled = pltpu.roll(h, d, axis=0)
    rowid = lax.broadcasted_iota(jnp.int32, halo.shape, 0)
    head = jnp.where(rowid < d, pltpu.roll(halo, d, axis=0), rolled[:HALO, :])
    return jnp.concatenate([head, rolled[HALO:, :]], axis=0)


def _linear_scan(a, u, inits, lb):
    rows, width = a.shape
    ngroups = rows // SUBLANES
    a3 = a.reshape(ngroups, SUBLANES, width)
    u3 = u.reshape(ngroups, SUBLANES, width)
    sub = lax.broadcasted_iota(jnp.int32, a3.shape, 1)
    d = 1
    while d < SUBLANES:
        inside = sub >= d
        u3 = jnp.where(inside, a3 * pltpu.roll(u3, d, axis=1) + u3, u3)
        a3 = jnp.where(inside, a3 * pltpu.roll(a3, d, axis=1), a3)
        d *= 2
    out = []
    carry = None
    for g in range(ngroups):
        if g % (lb // SUBLANES) == 0:
            carry = inits[g // (lb // SUBLANES)]
        hg = u3[g] + a3[g] * carry
        out.append(hg)
        carry = hg[SUBLANES - 1:, :]
    return jnp.concatenate(out, axis=0)


def _cumsum_rows(x, inits, lb):
    rows, width = x.shape
    ngroups = rows // SUBLANES
    x3 = x.reshape(ngroups, SUBLANES, width)
    sub = lax.broadcasted_iota(jnp.int32, x3.shape, 1)
    d = 1
    while d < SUBLANES:
        x3 = x3 + jnp.where(sub >= d, pltpu.roll(x3, d, axis=1), 0.0)
        d *= 2
    out = []
    carry = None
    for g in range(ngroups):
        if g % (lb // SUBLANES) == 0:
            carry = inits[g // (lb // SUBLANES)]
        cg = x3[g] + carry
        out.append(cg)
        carry = cg[SUBLANES - 1:, :]
    return jnp.concatenate(out, axis=0)


def _ffn_kernel(*refs, tm, nseq, bps, tn, mix):
    if mix:
        x_ref, attn_ref, gated_ref, woa_ref, wob_ref, mg_ref, mb_ref = refs[:7]
        refs = refs[7:]
    else:
        x_ref = refs[0]
        refs = refs[1:]
    st_ref, wup_ref, taps_ref, wd_ref, g_ref, b_ref, y_ref, so_ref, a_ref, carry_ref = refs
    i = pl.program_id(0)
    lb = tm // nseq
    if bps > 1:
        @pl.when(i == 0)
        def _():
            carry_ref[...] = jnp.zeros_like(carry_ref)
    x = x_ref[...]
    if mix:
        mixed = _dot(attn_ref[...], woa_ref[...]) + _dot(gated_ref[...], wob_ref[...])
        x = _layer_norm(ALPHA * x + mixed, mg_ref[...], mb_ref[...])
    x16 = x.astype(BF16)
    first = (i % bps) == 0

    def conv_branch(h, cols):
        taps = taps_ref[:, cols]
        outs = []
        for s in range(nseq):
            hs = h[s * lb:(s + 1) * lb, :]
            if bps > 1:
                halo = jnp.where(first, st_ref[0, :, cols], carry_ref[:, cols])
            else:
                halo = st_ref[s, :, cols]
            so_ref[s, :, cols] = hs[lb - HALO:, :]
            y = taps[3:4, :] + taps[2:3, :] * hs
            y = y + taps[1:2, :] * _shift_rows(hs, halo, 1)
            y = y + taps[0:1, :] * _shift_rows(hs, halo, 2)
            outs.append(y)
        if bps > 1:
            carry_ref[:, cols] = h[tm - HALO:, :]
        return outs

    for c in range(D_FF // tn):
        cg = slice(c * tn, (c + 1) * tn)
        cu = slice(D_FF + c * tn, D_FF + (c + 1) * tn)
        yg = conv_branch(_dot(x16, wup_ref[:, cg]), cg)
        yu = conv_branch(_dot(x16, wup_ref[:, cu]), cu)
        for s in range(nseq):
            a_ref[s * lb:(s + 1) * lb, cg] = (jax.nn.gelu(yg[s]) * yu[s]).astype(BF16)
    y = _dot(a_ref[...], wd_ref[...])
    y_ref[...] = _layer_norm(ALPHA * x + y, g_ref[...], b_ref[...])


def _ffn(x2d, state, seq_len, prm, mix=None):
    n = x2d.shape[0]
    tm, nseq, bps = _plan(n, seq_len)
    nblk = n // tm
    bsz = state.shape[0]
    st = jnp.pad(state, ((0, 0), (HALO - (FFN_CONV - 1), 0), (0, 0)))
    row = pl.BlockSpec((tm, D_MODEL), lambda i: (i, 0))
    consts = [prm[k] for k in ("w_up", "taps", "w_d", "ln_g", "ln_b")]
    layer = prm["layer"]
    of_layer = lambda w: pl.BlockSpec((None,) + w.shape[1:], lambda i: (layer, 0, 0), pipeline_mode=pl.Buffered(1))
    const_specs = [of_layer(c) if c.ndim == 3 else _const_spec(c.shape) for c in consts]
    lead, lead_specs = [x2d], [row]
    if mix is not None:
        attn16, gated16, eprm = mix
        mix_consts = [eprm[k] for k in ("wo_a", "wo_b", "ln_g", "ln_b")]
        lead += [attn16, gated16] + mix_consts
        lead_specs += [pl.BlockSpec((tm, A_WIDTH), lambda i: (i, 0)), pl.BlockSpec((tm, B_WIDTH), lambda i: (i, 0))]
        lead_specs += [_const_spec(c.shape) for c in mix_consts]
    y, so = pl.pallas_call(
        functools.partial(_ffn_kernel, tm=tm, nseq=nseq, bps=bps, tn=FFN_CHUNK, mix=mix is not None),
        grid=(nblk,),
        in_specs=lead_specs + [pl.BlockSpec((nseq, HALO, 2 * D_FF), lambda i: (i // bps, 0, 0))] + const_specs,
        out_specs=(row, pl.BlockSpec((nseq, HALO, 2 * D_FF), lambda i: (i, 0, 0))),
        out_shape=(jax.ShapeDtypeStruct((n, D_MODEL), F32),
                   jax.ShapeDtypeStruct((nblk * nseq, HALO, 2 * D_FF), F32)),
        scratch_shapes=[pltpu.VMEM((tm, D_FF), BF16),
                        pltpu.VMEM((HALO, 2 * D_FF), F32)],
        compiler_params=_params(("arbitrary",)),
        name="conv_ffn",
    )(*lead, st, *consts)
    so = so.reshape(bsz, -1, HALO, 2 * D_FF)[:, -1]
    return y, so[:, HALO - (FFN_CONV - 1):, :]


def _odd_kernel(x_ref, wu_ref, wv_ref, sg_ref, sb_ref, ws_ref, bias_ref, wo_ref, g_ref, b_ref,
                y_ref, *rest, tm, lc, emit_v):
    gate_ref = rest[-1]
    x = x_ref[...]
    x16 = x.astype(BF16)
    u = jax.nn.gelu(_dot(x16, wu_ref[...]))
    v = _layer_norm(jax.nn.gelu(_dot(x16, wv_ref[...])), sg_ref[...], sb_ref[...])
    if emit_v:
        rest[0][...] = v
    v16 = v.astype(BF16)
    gw = C_WIDTH // C_GROUPS
    for c in range(tm // lc):
        rows = slice(c * lc, (c + 1) * lc)
        for g in range(C_GROUPS):
            cols = slice(g * gw, (g + 1) * gw)
            mixed = _dot(ws_ref[g], v16[rows, cols]) + bias_ref[:, cols]
            gate_ref[rows, cols] = (u[rows, cols] * mixed).astype(BF16)
    y = _dot(gate_ref[...], wo_ref[...])
    y_ref[...] = _layer_norm(ALPHA * x + y, g_ref[...], b_ref[...])


def _odd(x2d, seq_len, prm, emit_v):
    n = x2d.shape[0]
    tm = min(ROW_BLOCK, n)
    lc = min(seq_len, C_LEN)
    gw = C_WIDTH // C_GROUPS
    ws16 = (prm["sgu_w"][:, :lc, :lc] * jnp.tril(jnp.ones((lc, lc), F32))).astype(BF16)
    bias = jnp.repeat(prm["sgu_bias"][:, :lc].T, gw, axis=1)
    consts = [prm["sgu_g"], prm["sgu_b"], ws16, bias, prm["w_o"], prm["ln_g"], prm["ln_b"]]
    w_half = lambda j: pl.BlockSpec((D_MODEL, C_WIDTH), lambda i: (0, j), pipeline_mode=pl.Buffered(1))
    row = pl.BlockSpec((tm, D_MODEL), lambda i: (i, 0))
    n_out = 2 if emit_v else 1
    return pl.pallas_call(
        functools.partial(_odd_kernel, tm=tm, lc=lc, emit_v=emit_v),
        grid=(n // tm,),
        in_specs=[row, w_half(0), w_half(1)] + [_const_spec(c.shape) for c in consts],
        out_specs=(row, pl.BlockSpec((tm, C_WIDTH), lambda i: (i, 0)))[:n_out],
        out_shape=(jax.ShapeDtypeStruct((n, D_MODEL), F32),
                   jax.ShapeDtypeStruct((n, C_WIDTH), F32))[:n_out],
        scratch_shapes=[pltpu.VMEM((tm, C_WIDTH), BF16)],
        compiler_params=_params(("arbitrary",)),
        name="odd_mixer",
    )(x2d, prm["w_in"], prm["w_in"], *consts)


def _row(v):
    return v.astype(F32)[None, :]


def _block_diag(w):
    nb, bd, _ = w.shape
    eye = jnp.eye(nb, dtype=w.dtype)
    return jnp.einsum("ncd,nm->ncmd", w, eye).reshape(nb * bd, nb * bd)


def _even_params(w_in, b_f, rg_conv_w, rg_conv_b, rg_wa, rg_ba, rg_wx, rg_bx, rg_lam, w_out, ln_g, ln_b):
    nf = 3 * A_WIDTH
    wf = jnp.pad(w_in[:, nf:nf + N_HEADS], ((0, 0), (0, LANES - N_HEADS)))
    return {
        "wqkv": w_in[:, :nf].astype(BF16),
        "wf": wf.astype(BF16),
        "wxg": w_in[:, nf + N_HEADS:].astype(BF16),
        "bf": jnp.pad(b_f, (0, LANES - N_HEADS))[None, :],
        "rg_cw": rg_conv_w, "rg_cb": _row(rg_conv_b),
        "rg_wa": _block_diag(rg_wa).astype(BF16), "rg_ba": _row(rg_ba),
        "rg_wx": _block_diag(rg_wx).astype(BF16), "rg_bx": _row(rg_bx),
        "rg_lam": _row(rg_lam),
        "wo_a": w_out[:A_WIDTH].astype(BF16), "wo_b": w_out[A_WIDTH:].astype(BF16),
        "ln_g": _row(ln_g), "ln_b": _row(ln_b),
    }


def _ffn_params(layer, w_up16_all, conv_w, conv_b, w_down16_all, ln_g, ln_b):
    taps = jnp.concatenate([conv_w, conv_b[None, :],
                            jnp.zeros((HALO - FFN_CONV - 1, 2 * D_FF), F32)], axis=0)
    return {"layer": layer, "w_up": w_up16_all, "taps": taps, "w_d": w_down16_all,
            "ln_g": _row(ln_g), "ln_b": _row(ln_b)}


def _odd_params(w_in16, sgu_g, sgu_b, sgu_w, sgu_bias, w_out16, ln_g, ln_b):
    return {
        "w_in": w_in16,
        "sgu_g": _row(sgu_g), "sgu_b": _row(sgu_b), "sgu_w": sgu_w, "sgu_bias": sgu_bias,
        "w_o": w_out16, "ln_g": _row(ln_g), "ln_b": _row(ln_b),
    }


def _even_layer(x3d, past, prm, casts=()):
    bsz, t, _ = x3d.shape
    x2d = x3d.reshape(bsz * t, D_MODEL)
    assert t >= HALO
    if past is None:
        conv_s = jnp.zeros((bsz, RG_CONV - 1, B_WIDTH), F32)
        h0 = jnp.zeros((bsz, B_WIDTH), F32)
    else:
        k_c, v_c, lf_c, conv_s, h0 = past
    kv_t = t >= ROW_BLOCK
    qkv16, k, v, logf, cum, ck, gated16, xtail, hl = _even_in(x2d, t, conv_s, h0, prm, kv_t)
    qkv16 = qkv16.reshape(bsz, t, 3 * A_WIDTH)
    cum3 = cum.reshape(bsz, t, LANES)
    cast16 = ()
    if past is None:
        attn, cast16 = _fox_prompt(qkv16, cum3, ck, ROW_BLOCK, list(casts))
    else:
        cum_t = cum3[:, :, :N_HEADS].transpose(0, 2, 1)
        attn = _fox_sample(qkv16, cum3, cum_t, k_c.transpose(0, 2, 3, 1), v_c.transpose(0, 2, 3, 1),
                           lf_c.transpose(0, 2, 1))
    if kv_t:
        new_k = k.reshape(bsz, N_HEADS, HEAD_DIM, t).transpose(0, 3, 1, 2)
        new_v = v.reshape(bsz, N_HEADS, HEAD_DIM, t).transpose(0, 3, 1, 2)
    else:
        new_k = k.reshape(bsz, t, N_HEADS, HEAD_DIM)
        new_v = v.reshape(bsz, t, N_HEADS, HEAD_DIM)
    new_logf = logf.reshape(bsz, t, LANES)[:, :, :N_HEADS]
    h_last = hl.reshape(bsz, -1, B_WIDTH)[:, -1, :]
    conv_new = xtail.reshape(bsz, -1, HALO, B_WIDTH)[:, -1, HALO - (RG_CONV - 1):, :]
    return (attn.reshape(bsz * t, A_WIDTH), gated16), (new_k, new_v, new_logf, conv_new, h_last), cast16


def _ffn_layer(x3d, state, prm, mix=None):
    bsz, t, _ = x3d.shape
    y, new_state = _ffn(x3d.reshape(bsz * t, D_MODEL), state, t, prm, mix)
    return y.reshape(bsz, t, D_MODEL), new_state


def _odd_layer(x3d, prm, emit_v):
    bsz, t, _ = x3d.shape
    outs = _odd(x3d.reshape(bsz * t, D_MODEL), t, prm, emit_v)
    v = outs[1].reshape(bsz, t, C_WIDTH) if emit_v else None
    return outs[0].reshape(bsz, t, D_MODEL), v


def kernel(x_prompt, x_sample, cache_k, cache_v, cache_logf, state_rglru_conv, state_rglru_h, state_ffn_conv, w_in_e, b_f, rg_conv_w, rg_conv_b, rg_wa, rg_ba, rg_wx, rg_bx, rg_lam, w_out_e, w_in_o, sgu_g, sgu_b, sgu_w, sgu_bias, w_out_o, ln_mix_g, ln_mix_b, ln_ffn_g, ln_ffn_b, ffn_w_up, ffn_conv_w, ffn_conv_b, ffn_w_down):
    xp, xs = x_prompt, x_sample
    later_weights = [ffn_w_up.reshape(-1, 2 * D_FF), ffn_w_down.reshape(-1, D_MODEL),
                     w_in_o.reshape(-1, 2 * C_WIDTH), w_out_o.reshape(-1, D_MODEL)]
    even_p, even_s, ffn_p, ffn_s, sgu_v = [], [], [], [], []
    for layer in range(DEPTH):
        if layer % 2 == 0:
            e = layer // 2
            prm = _even_params(w_in_e[e], b_f[e], rg_conv_w[e], rg_conv_b[e], rg_wa[e], rg_ba[e],
                               rg_wx[e], rg_bx[e], rg_lam[e], w_out_e[e],
                               ln_mix_g[layer], ln_mix_b[layer])
            branches_p, st_p, cast16 = _even_layer(xp, None, prm, later_weights if layer == 0 else ())
            if layer == 0:
                w_up16_all = cast16[0].reshape(ffn_w_up.shape)
                w_down16_all = cast16[1].reshape(ffn_w_down.shape)
                w_in_o16 = cast16[2].reshape(w_in_o.shape)
                w_out_o16 = cast16[3].reshape(w_out_o.shape)
            past = (cache_k[e], cache_v[e], cache_logf[e], state_rglru_conv[e], state_rglru_h[e])
            branches_s, st_s, _ = _even_layer(xs, past, prm)
            mix_p, mix_s = branches_p + (prm,), branches_s + (prm,)
            even_p.append(st_p)
            even_s.append(st_s)
        else:
            o = layer // 2
            prm = _odd_params(w_in_o16[o], sgu_g[o], sgu_b[o], sgu_w[o], sgu_bias[o], w_out_o16[o],
                              ln_mix_g[layer], ln_mix_b[layer])
            xp, _ = _odd_layer(xp, prm, emit_v=False)
            xs, sv = _odd_layer(xs, prm, emit_v=True)
            sgu_v.append(sv)
            mix_p = mix_s = None
        fprm = _ffn_params(layer, w_up16_all, ffn_conv_w[layer], ffn_conv_b[layer], w_down16_all,
                           ln_ffn_g[layer], ln_ffn_b[layer])
        zero_fc = jnp.zeros((xp.shape[0], FFN_CONV - 1, 2 * D_FF), F32)
        xp, fcp = _ffn_layer(xp, zero_fc, fprm, mix_p)
        xs, fcs = _ffn_layer(xs, state_ffn_conv[layer], fprm, mix_s)
        ffn_p.append(fcp)
        ffn_s.append(fcs)
    stack = lambda lst, idx: jnp.stack([item[idx] for item in lst])
    return (xp, xs,
            stack(even_p, 0), stack(even_p, 1), stack(even_p, 2),
            stack(even_s, 0), stack(even_s, 1), stack(even_s, 2),
            stack(even_p, 3), stack(even_p, 4), stack(even_s, 3), stack(even_s, 4),
            jnp.stack(ffn_p), jnp.stack(ffn_s), jnp.stack(sgu_v))
```
